```python
import jax, jax.numpy as jnp
from jax import lax
import numpy as np

D_MODEL = 1024
BATCH = 16
SEQ = 2048
DEPTH = 4

GRID_W = 64
CTX_LEN = 256
N_MIXERS = 3
N_HEADS = 16
N_KV_HEADS = 4
HEAD_DIM = D_MODEL // N_HEADS
Q_PER_KV = N_HEADS // N_KV_HEADS
QKV_DIM = (N_HEADS + 2 * N_KV_HEADS) * HEAD_DIM
ATTN_SCALE = HEAD_DIM ** -0.5
ROPE_THETA = 10000.0
BLOCK_Q = 128
WINDOW = 128
NA_WIN_H = 8
NA_WIN_W = 16
N_GROUPS = 4
EXPERTS_PER_GROUP = 8
N_EXPERTS = N_GROUPS * EXPERTS_PER_GROUP
TOP_K_IN_GROUP = 2
EXPERT_HIDDEN = D_MODEL // 2
MOE_BLOCK = 128
NORM_EPS = 1e-6
NEG_INF = -1e30
N_A_LAYERS = (DEPTH + N_MIXERS - 1) // N_MIXERS
N_B_LAYERS = (DEPTH + N_MIXERS - 2) // N_MIXERS
N_C_LAYERS = DEPTH // N_MIXERS

kernel_name = 'hybrid_dit_window_dense_natten_hmoe'


def _rmsnorm(x, g):
    x32 = x.astype(jnp.float32)
    y = x32 * lax.rsqrt(jnp.mean(x32 * x32, axis=-1, keepdims=True) + NORM_EPS)
    return (y * g.astype(jnp.float32)).astype(x.dtype)


def _modulate(h, shift, scale):
    return h * (1 + scale) + shift


def _axial_rope_tables(n, dtype):
    t = jnp.arange(n)
    row = (t // GRID_W).astype(jnp.float32)
    col = (t % GRID_W).astype(jnp.float32)
    quarter = HEAD_DIM // 4
    inv = ROPE_THETA ** (-jnp.arange(quarter, dtype=jnp.float32) / quarter)
    ar = row[:, None] * inv
    ac = col[:, None] * inv
    ang = jnp.concatenate([ar, ar, ac, ac], axis=-1)
    return jnp.cos(ang).astype(dtype), jnp.sin(ang).astype(dtype)


def _rot_axial(x):
    r1, r2, c1, c2 = jnp.split(x, 4, axis=-1)
    return jnp.concatenate([-r2, r1, -c2, c1], axis=-1)


def _apply_rope(x, cos, sin):
    shp = (1, x.shape[1]) + (1,) * (x.ndim - 3) + (HEAD_DIM,)
    return x * cos.reshape(shp) + _rot_axial(x) * sin.reshape(shp)


def _split_qkv(h, w_qkv):
    b, n, _ = h.shape
    qkv = h @ w_qkv
    q, k, v = jnp.split(qkv, [N_HEADS * HEAD_DIM, (N_HEADS + N_KV_HEADS) * HEAD_DIM], axis=-1)
    return (q.reshape(b, n, N_KV_HEADS, Q_PER_KV, HEAD_DIM),
            k.reshape(b, n, N_KV_HEADS, HEAD_DIM),
            v.reshape(b, n, N_KV_HEADS, HEAD_DIM))


def _sink_column(sink, lead_shape):
    return jnp.broadcast_to(sink.astype(jnp.float32).reshape(1, N_KV_HEADS, Q_PER_KV, 1, 1), lead_shape + (1,))


def _joint_softmax(s_lat, s_ctx, sink):
    parts = [s_lat, s_ctx]
    if sink is not None:
        parts.append(_sink_column(sink, s_lat.shape[:-1]))
    p = jax.nn.softmax(jnp.concatenate(parts, axis=-1), axis=-1)
    k1 = s_lat.shape[-1]
    c = s_ctx.shape[-1]
    return p[..., :k1], p[..., k1:k1 + c]


def _ctx_self_attention(qc, kc, vc, sink):
    b, n = qc.shape[:2]
    s = jnp.einsum('bqhgd,bkhd->bhgqk', qc, kc).astype(jnp.float32)
    if sink is not None:
        s = jnp.concatenate([s, _sink_column(sink, s.shape[:-1])], axis=-1)
    p = jax.nn.softmax(s, axis=-1)[..., :kc.shape[1]]
    o = jnp.einsum('bhgqk,bkhd->bqhgd', p.astype(vc.dtype), vc)
    return o.reshape(b, n, N_HEADS * HEAD_DIM)


def _mixer_window(q, k, v, kc, vc, sink, cos, sin):
    b, n = q.shape[:2]
    nb = n // BLOCK_Q
    span = BLOCK_Q + 2 * WINDOW
    q_rot = _apply_rope(q, cos, sin)
    pad = ((0, 0), (WINDOW, WINDOW), (0, 0), (0, 0))
    k_pad = jnp.pad(_apply_rope(k, cos, sin), pad)
    v_pad = jnp.pad(v, pad)

    def block(i):
        start = i * BLOCK_Q
        qb = lax.dynamic_slice_in_dim(q_rot, start, BLOCK_Q, 1)
        qpb = lax.dynamic_slice_in_dim(q, start, BLOCK_Q, 1)
        kb = lax.dynamic_slice_in_dim(k_pad, start, span, 1)
        vb = lax.dynamic_slice_in_dim(v_pad, start, span, 1)
        qpos = start + jnp.arange(BLOCK_Q)
        kpos = start - WINDOW + jnp.arange(span)
        mask = ((jnp.abs(kpos[None, :] - qpos[:, None]) <= WINDOW)
                & (kpos >= 0)[None, :] & (kpos < n)[None, :])
        s_lat = jnp.einsum('bqhgd,bkhd->bhgqk', qb, kb).astype(jnp.float32)
        s_lat = jnp.where(mask, s_lat, NEG_INF)
        s_ctx = jnp.einsum('bqhgd,bkhd->bhgqk', qpb, kc).astype(jnp.float32)
        p_lat, p_ctx = _joint_softmax(s_lat, s_ctx, sink)
        return (jnp.einsum('bhgqk,bkhd->bqhgd', p_lat.astype(v.dtype), vb)
                + jnp.einsum('bhgqk,bkhd->bqhgd', p_ctx.astype(vc.dtype), vc))

    o = lax.map(block, jnp.arange(nb))
    return jnp.moveaxis(o, 0, 1).reshape(b, n, N_HEADS * HEAD_DIM)


def _mixer_global(q, k, v, kc, vc, cos, sin):
    b, n = q.shape[:2]
    nb = n // BLOCK_Q
    q_rot = _apply_rope(q, cos, sin)
    k_rot = _apply_rope(k, cos, sin)

    def block(i):
        start = i * BLOCK_Q
        qb = lax.dynamic_slice_in_dim(q_rot, start, BLOCK_Q, 1)
        qpb = lax.dynamic_slice_in_dim(q, start, BLOCK_Q, 1)
        s_lat = jnp.einsum('bqhgd,bkhd->bhgqk', qb, k_rot).astype(jnp.float32)
        s_ctx = jnp.einsum('bqhgd,bkhd->bhgqk', qpb, kc).astype(jnp.float32)
        p_lat, p_ctx = _joint_softmax(s_lat, s_ctx, None)
        return (jnp.einsum('bhgqk,bkhd->bqhgd', p_lat.astype(v.dtype), v)
                + jnp.einsum('bhgqk,bkhd->bqhgd', p_ctx.astype(vc.dtype), vc))

    o = lax.map(block, jnp.arange(nb))
    return jnp.moveaxis(o, 0, 1).reshape(b, n, N_HEADS * HEAD_DIM)


def _na_tables(n):
    rows = n // GRID_W
    wh = min(NA_WIN_H, rows)
    ww = min(NA_WIN_W, GRID_W)
    t = np.arange(n)
    r = t // GRID_W
    cl = t % GRID_W
    rs = np.clip(r - wh // 2, 0, rows - wh)
    cs = np.clip(cl - ww // 2, 0, GRID_W - ww)
    kr = rs[:, None, None] + np.arange(wh)[None, :, None]
    kcol = cs[:, None, None] + np.arange(ww)[None, None, :]
    idx = (kr * GRID_W + kcol).reshape(n, wh * ww)
    rel = ((kr - r[:, None, None] + NA_WIN_H - 1) * (2 * NA_WIN_W - 1)
           + (kcol - cl[:, None, None] + NA_WIN_W - 1)).reshape(n, wh * ww)
    return idx.astype(np.int32), rel.astype(np.int32)


def _mixer_neighbourhood(q, k, v, kc, vc, rpb):
    b, n = q.shape[:2]
    nb = n // BLOCK_Q
    idx, rel = _na_tables(n)
    n_keys = idx.shape[1]
    idx_blocks = jnp.asarray(idx).reshape(nb, BLOCK_Q, n_keys)
    rel_blocks = jnp.asarray(rel).reshape(nb, BLOCK_Q, n_keys)
    rpb_flat = rpb.reshape(N_HEADS, -1)

    def block(args):
        i, idx_b, rel_b = args
        qb = lax.dynamic_slice_in_dim(q, i * BLOCK_Q, BLOCK_Q, 1)
        kg = k[:, idx_b]
        vg = v[:, idx_b]
        bias = rpb_flat[:, rel_b].reshape(N_KV_HEADS, Q_PER_KV, BLOCK_Q, n_keys).astype(jnp.float32)
        s_lat = jnp.einsum('bqhgd,bqkhd->bhgqk', qb, kg).astype(jnp.float32) + bias
        s_ctx = jnp.einsum('bqhgd,bkhd->bhgqk', qb, kc).astype(jnp.float32)
        p_lat, p_ctx = _joint_softmax(s_lat, s_ctx, None)
        return (jnp.einsum('bhgqk,bqkhd->bqhgd', p_lat.astype(v.dtype), vg)
                + jnp.einsum('bhgqk,bkhd->bqhgd', p_ctx.astype(vc.dtype), vc))

    o = lax.map(block, (jnp.arange(nb), idx_blocks, rel_blocks))
    return jnp.moveaxis(o, 0, 1).reshape(b, n, N_HEADS * HEAD_DIM)


def _hier_moe(h, w_group, b_group, w_router, b_router, w_gate, w_up, w_down):
    t, d = h.shape
    p_grp = jax.nn.softmax((h @ w_group + b_group).astype(jnp.float32), axis=-1)
    grp = jnp.argmax(p_grp, axis=-1)
    p_top = jnp.take_along_axis(p_grp, grp[:, None], axis=-1)
    lg = (h @ w_router + b_router).astype(jnp.float32).reshape(t, N_GROUPS, EXPERTS_PER_GROUP)
    lg = jnp.take_along_axis(lg, grp[:, None, None], axis=1)[:, 0]
    p_in, e_in = lax.top_k(jax.nn.softmax(lg, axis=-1), TOP_K_IN_GROUP)
    wts = (p_top * p_in / jnp.sum(p_in, axis=-1, keepdims=True)).reshape(-1)
    eid = (grp[:, None] * EXPERTS_PER_GROUP + e_in).reshape(-1).astype(jnp.int32)
    tok = jnp.repeat(jnp.arange(t, dtype=jnp.int32), TOP_K_IN_GROUP)
    n_asg = t * TOP_K_IN_GROUP
    cap = -(-n_asg // MOE_BLOCK) * MOE_BLOCK + N_EXPERTS * MOE_BLOCK
    n_blocks = cap // MOE_BLOCK
    order = jnp.argsort(eid)
    eid_s = eid[order]
    tok_s = tok[order]
    wts_s = wts[order]
    counts = jnp.bincount(eid, length=N_EXPERTS).astype(jnp.int32)
    padded = (counts + MOE_BLOCK - 1) // MOE_BLOCK * MOE_BLOCK
    pad_end = jnp.cumsum(padded)
    pad_start = pad_end - padded
    start = jnp.cumsum(counts) - counts
    dest = pad_start[eid_s] + jnp.arange(n_asg, dtype=jnp.int32) - start[eid_s]
    tok_buf = jnp.full((cap,), t, jnp.int32).at[dest].set(tok_s)
    w_buf = jnp.zeros((cap,), jnp.float32).at[dest].set(wts_s)
    blk_e = jnp.minimum(jnp.searchsorted(pad_end, jnp.arange(n_blocks, dtype=jnp.int32) * MOE_BLOCK, side='right'),
                        N_EXPERTS - 1)
    h_pad = jnp.concatenate([h, jnp.zeros((1, d), h.dtype)], axis=0)
    xb = h_pad[tok_buf].reshape(n_blocks, MOE_BLOCK, d)

    def expert_block(args):
        xe, e = args
        return (jax.nn.silu(xe @ w_gate[e]) * (xe @ w_up[e])) @ w_down[e]

    y = lax.map(expert_block, (xb, blk_e)).reshape(cap, d)
    out = jax.ops.segment_sum(y.astype(jnp.float32) * w_buf[:, None], tok_buf, num_segments=t + 1)
    return out[:t]


def setup_inputs(seed: int = 0) -> dict:
    key = jax.random.key(seed)
    ks = jax.random.split(key, 24)

    def nrm(k, shape, s):
        return jax.random.normal(k, shape, jnp.float32) * s

    d, f, hd = D_MODEL, EXPERT_HIDDEN, N_HEADS * HEAD_DIM
    return {
        'x': nrm(ks[0], (BATCH, SEQ, d), 1.0),
        'c': nrm(ks[1], (BATCH, d), 1.0),
        'ctx': nrm(ks[2], (BATCH, CTX_LEN, d), 1.0),
        'c_ctx': nrm(ks[3], (d,), 1.0),
        'w_ada': nrm(ks[4], (DEPTH, d, 6 * d), 0.5 * d ** -0.5),
        'b_ada': nrm(ks[5], (DEPTH, 6 * d), 0.02),
        'g_attn': 1.0 + nrm(ks[6], (DEPTH, d), 0.05),
        'w_qkv': nrm(ks[7], (DEPTH, d, QKV_DIM), d ** -0.5),
        'w_o': nrm(ks[8], (DEPTH, hd, d), hd ** -0.5),
        'sink_a': nrm(ks[9], (N_A_LAYERS, N_HEADS), 1.0),
        'gq_b': 1.0 + nrm(ks[10], (N_B_LAYERS, HEAD_DIM), 0.05),
        'gk_b': 1.0 + nrm(ks[11], (N_B_LAYERS, HEAD_DIM), 0.05),
        'rpb_c': nrm(ks[12], (N_C_LAYERS, N_HEADS, 2 * NA_WIN_H - 1, 2 * NA_WIN_W - 1), 0.5),
        'g_ffn': 1.0 + nrm(ks[13], (DEPTH, d), 0.05),
        'w_group': nrm(ks[14], (DEPTH, d, N_GROUPS), d ** -0.5),
        'b_group': nrm(ks[15], (DEPTH, N_GROUPS), 0.01),
        'w_router': nrm(ks[16], (DEPTH, d, N_EXPERTS), d ** -0.5),
        'b_router': nrm(ks[17], (DEPTH, N_EXPERTS), 0.01),
        'w_gate': nrm(ks[18], (DEPTH, N_EXPERTS, d, f), d ** -0.5),
        'w_up': nrm(ks[19], (DEPTH, N_EXPERTS, d, f), d ** -0.5),
        'w_down': nrm(ks[20], (DEPTH, N_EXPERTS, f, d), f ** -0.5),
        'g_final': 1.0 + nrm(ks[21], (d,), 0.05),
    }


def reference(x, c, ctx, c_ctx, w_ada, b_ada, g_attn, w_qkv, w_o, sink_a, gq_b, gk_b, rpb_c, g_ffn,
              w_group, b_group, w_router, b_router, w_gate, w_up, w_down, g_final):
    b, n, d = x.shape
    cos, sin = _axial_rope_tables(n, x.dtype)
    silu_c = jax.nn.silu(c)
    silu_cc = jax.nn.silu(c_ctx)
    xc = ctx
    for i in range(DEPTH):
        m = i % N_MIXERS
        j = i // N_MIXERS
        last = i == DEPTH - 1
        mod = (silu_c @ w_ada[i] + b_ada[i])[:, None, :]
        modc = (silu_cc @ w_ada[i] + b_ada[i])[None, None, :]
        sh_a, sc_a, gt_a, sh_f, sc_f, gt_f = jnp.split(mod, 6, axis=-1)
        shc_a, scc_a, gtc_a, shc_f, scc_f, gtc_f = jnp.split(modc, 6, axis=-1)

        h = _modulate(_rmsnorm(x, g_attn[i]), sh_a, sc_a)
        hc = _modulate(_rmsnorm(xc, g_attn[i]), shc_a, scc_a)
        q, k, v = _split_qkv(h, w_qkv[i])
        qc, kc, vc = _split_qkv(hc, w_qkv[i])
        sink = None
        if m == 0:
            sink = sink_a[j]
        elif m == 1:
            q, k = _rmsnorm(q, gq_b[j]), _rmsnorm(k, gk_b[j])
            qc, kc = _rmsnorm(qc, gq_b[j]), _rmsnorm(kc, gk_b[j])
        q = q * ATTN_SCALE
        qc = qc * ATTN_SCALE
        if m == 0:
            o = _mixer_window(q, k, v, kc, vc, sink, cos, sin)
        elif m == 1:
            o = _mixer_global(q, k, v, kc, vc, cos, sin)
        else:
            o = _mixer_neighbourhood(q, k, v, kc, vc, rpb_c[j])
        x = x + gt_a * (o @ w_o[i])
        if not last:
            oc = _ctx_self_attention(qc, kc, vc, sink)
            xc = xc + gtc_a * (oc @ w_o[i])

        hf = _modulate(_rmsnorm(x, g_ffn[i]), sh_f, sc_f).reshape(b * n, d)
        if last:
            tokens = hf
        else:
            hfc = _modulate(_rmsnorm(xc, g_ffn[i]), shc_f, scc_f).reshape(-1, d)
            tokens = jnp.concatenate([hf, hfc], axis=0)
        y = _hier_moe(tokens, w_group[i], b_group[i], w_router[i], b_router[i],
                      w_gate[i], w_up[i], w_down[i]).astype(x.dtype)
        x = x + gt_f * y[:b * n].reshape(b, n, d)
        if not last:
            xc = xc + gtc_f * y[b * n:].reshape(xc.shape)
    return _rmsnorm(x, g_final)
```

```python
import functools

import numpy as np
import jax
import jax.numpy as jnp
from jax import lax
from jax.experimental import pallas as pl
from jax.experimental.pallas import tpu as pltpu

F32 = jnp.float32
BF16 = jnp.bfloat16
I32 = jnp.int32

D_MODEL = 1024
GRID_W = 64
N_MIXERS = 3
N_HEADS = 16
N_KV_HEADS = 4
HEAD_DIM = D_MODEL // N_HEADS
Q_PER_KV = N_HEADS // N_KV_HEADS
QKV_DIM = (N_HEADS + 2 * N_KV_HEADS) * HEAD_DIM
ATTN_SCALE = HEAD_DIM ** -0.5
ROPE_THETA = 10000.0
BLOCK_Q = 128
WINDOW = 128
NA_WIN_H = 8
NA_WIN_W = 16
N_GROUPS = 4
EXPERTS_PER_GROUP = 8
N_EXPERTS = N_GROUPS * EXPERTS_PER_GROUP
EXPERT_HIDDEN = D_MODEL // 2
NORM_EPS = 1e-6
NEG_INF = -1e30

LANES = 128
MOD_ROWS = 24
N_MOD = 6
EXPERT_BLOCK = 256
ROUTER_COLS = LANES
NA_KEY_ROWS = 10
NA_TABLE = 18
VMEM_LIMIT = 56 * 1024 * 1024


def _cparams(sem, vmem=VMEM_LIMIT):
    return pltpu.CompilerParams(dimension_semantics=sem, vmem_limit_bytes=vmem)


def _dot(a, b):
    return jnp.dot(a, b, preferred_element_type=F32)


def _dot_nt(a, b):
    return lax.dot_general(a, b, (((1,), (1,)), ((), ())), preferred_element_type=F32)


def _split_bf16(x):
    hi = x.astype(BF16)
    lo = (x - hi.astype(F32)).astype(BF16)
    return hi, lo


def _rms_mod(x, g, sh, sc):
    ms = jnp.mean(x * x, axis=-1, keepdims=True)
    y = x * lax.rsqrt(ms + NORM_EPS) * g
    return y * (1.0 + sc) + sh


def _ada_kernel(c_ref, w_ref, b_ref, o_ref):
    c = c_ref[...]
    s = c * jax.nn.sigmoid(c)
    o_ref[0] = _dot(s.astype(BF16), w_ref[0].astype(BF16)) + b_ref[0]


def _ada_all(cc, w_ada, b_ada):
    depth, d, d6 = w_ada.shape
    tn = 1536
    return pl.pallas_call(
        _ada_kernel,
        grid=(depth, d6 // tn),
        in_specs=[
            pl.BlockSpec((MOD_ROWS, d), lambda i, j: (0, 0)),
            pl.BlockSpec((1, d, tn), lambda i, j: (i, 0, j)),
            pl.BlockSpec((1, 1, tn), lambda i, j: (i, 0, j)),
        ],
        out_specs=pl.BlockSpec((1, MOD_ROWS, tn), lambda i, j: (i, 0, j)),
        out_shape=jax.ShapeDtypeStruct((depth, MOD_ROWS, d6), F32),
        compiler_params=_cparams(("parallel", "parallel")),
        name="ada_mod",
    )(cc, w_ada, b_ada.reshape(depth, 1, d6))


def _mod_spec(layer, chunk, ctx_row):
    if ctx_row is None:
        return pl.BlockSpec((1, 1, D_MODEL), lambda b, t: ((layer * MOD_ROWS + b) * N_MOD + chunk, 0, 0))
    return pl.BlockSpec((1, 1, D_MODEL), lambda b, t: ((layer * MOD_ROWS + ctx_row) * N_MOD + chunk, 0, 0))


def _layer_row_spec(layer):
    return pl.BlockSpec((1, 1, D_MODEL), lambda b, t: (layer, 0, 0))


def _qkv_kernel(*refs, qk_norm, rope):
    x_ref, g_ref, sh_ref, sc_ref, w_ref = refs[:5]
    pos = 5
    if rope:
        cos_ref, sin_ref = refs[pos:pos + 2]
        pos += 2
    if qk_norm:
        gq_ref, gk_ref, bm_ref = refs[pos:pos + 3]
        pos += 3
    if rope:
        qp_ref, qr_ref, k_ref, v_ref = refs[pos:]
    else:
        qp_ref, k_ref, v_ref = refs[pos:]

    h = _rms_mod(x_ref[0], g_ref[0], sh_ref[0], sc_ref[0])
    res = _dot(h.astype(BF16), w_ref[...])
    half = LANES // 2
    n_q = N_HEADS * HEAD_DIM // LANES
    n_k = N_KV_HEADS * HEAD_DIM // LANES
    if rope:
        lane = lax.broadcasted_iota(I32, (1, LANES), 1)
        first_half = (lane & (HEAD_DIM // 2 - 1)) < (HEAD_DIM // 4)
    for cb in range(QKV_DIM // LANES):
        xs = res[:, cb * LANES:(cb + 1) * LANES]
        is_q = cb < n_q
        is_k = n_q <= cb < n_q + n_k
        if qk_norm and (is_q or is_k):
            hi, lo = _split_bf16(xs * xs)
            ms = (_dot(hi, bm_ref[...]) + _dot(lo, bm_ref[...])) * (1.0 / HEAD_DIM)
            gain = gq_ref[...] if is_q else gk_ref[...]
            xs = xs * lax.rsqrt(ms + NORM_EPS) * gain
        if is_q:
            xs = xs * ATTN_SCALE
        if rope and (is_q or is_k):
            rot = jnp.where(first_half, pltpu.roll(xs, LANES - HEAD_DIM // 4, 1), pltpu.roll(xs, HEAD_DIM // 4, 1))
            xr = xs * cos_ref[...] + rot * sin_ref[...]
        if is_q:
            hd = 2 * cb
            qp_ref[0, hd] = xs[:, :half].astype(BF16)
            qp_ref[0, hd + 1] = xs[:, half:].astype(BF16)
            if rope:
                qr_ref[0, hd] = xr[:, :half].astype(BF16)
                qr_ref[0, hd + 1] = xr[:, half:].astype(BF16)
        elif is_k:
            hd = 2 * (cb - n_q)
            kk = xr if rope else xs
            k_ref[0, hd] = kk[:, :half].astype(BF16)
            k_ref[0, hd + 1] = kk[:, half:].astype(BF16)
        else:
            hd = 2 * (cb - n_q - n_k)
            v_ref[0, hd] = xs[:, :half].astype(BF16)
            v_ref[0, hd + 1] = xs[:, half:].astype(BF16)


def _qkv(x, modr, g_rows, w_bf, layer, ctx_row, tm, rope_tabs=None, qk_gains=None):
    b, n, d = x.shape
    rope = rope_tabs is not None
    qk_norm = qk_gains is not None
    args = [x, g_rows, modr, modr, w_bf]
    in_specs = [
        pl.BlockSpec((1, tm, d), lambda bb, t: (bb, t, 0)),
        _layer_row_spec(layer),
        _mod_spec(layer, 0, ctx_row),
        _mod_spec(layer, 1, ctx_row),
        pl.BlockSpec((d, QKV_DIM), lambda bb, t: (0, 0)),
    ]
    if rope:
        args += list(rope_tabs)
        in_specs += [pl.BlockSpec((tm, LANES), lambda bb, t: (t, 0))] * 2
    if qk_norm:
        args += list(qk_gains)
        in_specs += [pl.BlockSpec((1, LANES), lambda bb, t: (0, 0))] * 2
        in_specs += [pl.BlockSpec((LANES, LANES), lambda bb, t: (0, 0))]
    q_shape = jax.ShapeDtypeStruct((b, N_HEADS, n, HEAD_DIM), BF16)
    kv_shape = jax.ShapeDtypeStruct((b, N_KV_HEADS, n, HEAD_DIM), BF16)
    q_spec = pl.BlockSpec((1, N_HEADS, tm, HEAD_DIM), lambda bb, t: (bb, 0, t, 0))
    kv_spec = pl.BlockSpec((1, N_KV_HEADS, tm, HEAD_DIM), lambda bb, t: (bb, 0, t, 0))
    if rope:
        out_shape, out_specs = [q_shape, q_shape, kv_shape, kv_shape], [q_spec, q_spec, kv_spec, kv_spec]
    else:
        out_shape, out_specs = [q_shape, kv_shape, kv_shape], [q_spec, kv_spec, kv_spec]
    return pl.pallas_call(
        functools.partial(_qkv_kernel, qk_norm=qk_norm, rope=rope),
        grid=(b, n // tm),
        in_specs=in_specs,
        out_specs=out_specs,
        out_shape=out_shape,
        compiler_params=_cparams(("parallel", "parallel")),
        name="qkv_rope" if rope else "qkv",
    )(*args)


def _softmax_init(rows, sink_col):
    if sink_col is None:
        return (jnp.full((rows, 1), NEG_INF, F32), jnp.zeros((rows, 1), F32), jnp.zeros((rows, HEAD_DIM), F32))
    return (sink_col, jnp.ones((rows, 1), F32), jnp.zeros((rows, HEAD_DIM), F32))


def _softmax_step(state, s, v):
    m, l, acc = state
    m_new = jnp.maximum(m, jnp.max(s, axis=-1, keepdims=True))
    alpha = jnp.exp(m - m_new)
    p = jnp.exp(s - m_new)
    l = alpha * l + jnp.sum(p, axis=-1, keepdims=True)
    acc = alpha * acc + _dot(p.astype(BF16), v)
    return m_new, l, acc


def _sink_column(sink_ref, kv_head, rows_per_head):
    cols = [jnp.zeros((rows_per_head, 1), F32) + sink_ref[kv_head * Q_PER_KV + g] for g in range(Q_PER_KV)]
    return jnp.concatenate(cols, axis=0)


def _store_heads(o_ref, outs, rows_per_head):
    pieces = []
    for o in outs:
        for g in range(Q_PER_KV):
            pieces.append(o[g * rows_per_head:(g + 1) * rows_per_head])
    o_ref[0] = jnp.concatenate(pieces, axis=-1).astype(o_ref.dtype)


def _attn_kernel(*refs, mode, key_chunk):
    if mode == 0:
        qp_ref, qr_ref, k_ref, v_ref, kc_ref, vc_ref, sink_ref, o_ref = refs
    elif mode == 1:
        qp_ref, qr_ref, k_ref, v_ref, kc_ref, vc_ref, o_ref = refs
    else:
        qp_ref, k_ref, v_ref, kc_ref, vc_ref, t2_ref, o_ref = refs
    qb = pl.program_id(1)
    n = k_ref.shape[2]
    rows = Q_PER_KV * BLOCK_Q

    if mode == 0:
        span = BLOCK_Q + 2 * WINDOW
        kstart = pl.multiple_of(jnp.clip(qb * BLOCK_Q - WINDOW, 0, n - span), BLOCK_Q)
        qpos = qb * BLOCK_Q + (lax.broadcasted_iota(I32, (rows, span), 0) & (BLOCK_Q - 1))
        kpos = kstart + lax.broadcasted_iota(I32, (rows, span), 1)
        band = jnp.abs(kpos - qpos) <= WINDOW
    elif mode == 2:
        rows_grid = n // GRID_W
        r0 = qb * (BLOCK_Q // GRID_W)
        wr = jnp.clip(r0 - NA_WIN_H // 2, 0, rows_grid - NA_KEY_ROWS)
        kstart = pl.multiple_of(wr * GRID_W, GRID_W)
        lane = lax.broadcasted_iota(I32, (GRID_W, LANES), 1)

    outs = []
    for h in range(N_KV_HEADS):
        qp = qp_ref[0, h * Q_PER_KV:(h + 1) * Q_PER_KV].reshape(rows, HEAD_DIM)
        if mode == 0:
            qr = qr_ref[0, h * Q_PER_KV:(h + 1) * Q_PER_KV].reshape(rows, HEAD_DIM)
            state = _softmax_init(rows, _sink_column(sink_ref, h, BLOCK_Q))
            s = _dot_nt(qr, k_ref[0, h, pl.ds(kstart, span), :])
            s = jnp.where(band, s, NEG_INF)
            state = _softmax_step(state, s, v_ref[0, h, pl.ds(kstart, span), :])
        elif mode == 1:
            qr = qr_ref[0, h * Q_PER_KV:(h + 1) * Q_PER_KV].reshape(rows, HEAD_DIM)
            state = _softmax_init(rows, None)
            for c in range(n // key_chunk):
                s = _dot_nt(qr, k_ref[0, h, c * key_chunk:(c + 1) * key_chunk, :])
                state = _softmax_step(state, s, v_ref[0, h, c * key_chunk:(c + 1) * key_chunk, :])
        else:
            state = _softmax_init(rows, None)
            n_keys = NA_KEY_ROWS * GRID_W
            s = _dot_nt(qp, k_ref[0, h, pl.ds(kstart, n_keys), :])
            blocks = []
            for g in range(Q_PER_KV):
                for a in range(BLOCK_Q // GRID_W):
                    r = r0 + a
                    rs = jnp.clip(r - NA_WIN_H // 2, 0, rows_grid - NA_WIN_H)
                    tiles = []
                    for j in range(n_keys // LANES):
                        kr0 = wr + 2 * j
                        tab = t2_ref[h * Q_PER_KV + g, kr0 - r + NA_TABLE // 2]
                        pen0 = jnp.where((kr0 >= rs) & (kr0 < rs + NA_WIN_H), 0.0, NEG_INF)
                        pen1 = jnp.where((kr0 + 1 >= rs) & (kr0 + 1 < rs + NA_WIN_H), 0.0, NEG_INF)
                        tiles.append(tab + jnp.where(lane < GRID_W, pen0, pen1))
                    blocks.append(jnp.concatenate(tiles, axis=1))
            s = s + jnp.concatenate(blocks, axis=0)
            state = _softmax_step(state, s, v_ref[0, h, pl.ds(kstart, n_keys), :])
        state = _softmax_step(state, _dot_nt(qp, kc_ref[0, h]), vc_ref[0, h])
        _, l, acc = state
        outs.append(acc / l)
    _store_heads(o_ref, outs, BLOCK_Q)


def _attention(mode, qp, qr, k, v, kc, vc, extra):
    b, _, n, _ = qp.shape
    c = kc.shape[2]
    q_spec = pl.BlockSpec((1, N_HEADS, BLOCK_Q, HEAD_DIM), lambda bb, i: (bb, 0, i, 0))
    kv_spec = pl.BlockSpec((1, N_KV_HEADS, n, HEAD_DIM), lambda bb, i: (bb, 0, 0, 0))
    ctx_spec = pl.BlockSpec((1, N_KV_HEADS, c, HEAD_DIM), lambda bb, i: (bb, 0, 0, 0))
    if mode == 2:
        args = [qp, k, v, kc, vc, extra]
        in_specs = [q_spec, kv_spec, kv_spec, ctx_spec, ctx_spec,
                    pl.BlockSpec(extra.shape, lambda bb, i: (0, 0, 0, 0))]
    else:
        args = [qp, qr, k, v, kc, vc]
        in_specs = [q_spec, q_spec, kv_spec, kv_spec, ctx_spec, ctx_spec]
        if mode == 0:
            args.append(extra)
            in_specs.append(pl.BlockSpec(memory_space=pltpu.SMEM))
    return pl.pallas_call(
        functools.partial(_attn_kernel, mode=mode, key_chunk=512),
        grid=(b, n // BLOCK_Q),
        in_specs=in_specs,
        out_specs=pl.BlockSpec((1, BLOCK_Q, D_MODEL), lambda bb, i: (bb, i, 0)),
        out_shape=jax.ShapeDtypeStruct((b, n, D_MODEL), BF16),
        compiler_params=_cparams(("parallel", "arbitrary")),
        name=("attn_window", "attn_global", "attn_na")[mode],
    )(*args)


def _ctx_attn_kernel(*refs, has_sink):
    if has_sink:
        q_ref, k_ref, v_ref, sink_ref, o_ref = refs
    else:
        q_ref, k_ref, v_ref, o_ref = refs
    c = k_ref.shape[2]
    rows = Q_PER_KV * c
    outs = []
    for h in range(N_KV_HEADS):
        q = q_ref[0, h * Q_PER_KV:(h + 1) * Q_PER_KV].reshape(rows, HEAD_DIM)
        state = _softmax_init(rows, _sink_column(sink_ref, h, c) if has_sink else None)
        _, l, acc = _softmax_step(state, _dot_nt(q, k_ref[0, h]), v_ref[0, h])
        outs.append(acc / l)
    _store_heads(o_ref, outs, c)


def _ctx_attention(qc, kc, vc, sink):
    b, _, c, _ = qc.shape
    args = [qc, kc, vc]
    in_specs = [pl.BlockSpec((1, N_HEADS, c, HEAD_DIM), lambda bb: (bb, 0, 0, 0)),
                pl.BlockSpec((1, N_KV_HEADS, c, HEAD_DIM), lambda bb: (bb, 0, 0, 0)),
                pl.BlockSpec((1, N_KV_HEADS, c, HEAD_DIM), lambda bb: (bb, 0, 0, 0))]
    if sink is not None:
        args.append(sink)
        in_specs.append(pl.BlockSpec(memory_space=pltpu.SMEM))
    return pl.pallas_call(
        functools.partial(_ctx_attn_kernel, has_sink=sink is not None),
        grid=(b,),
        in_specs=in_specs,
        out_specs=pl.BlockSpec((1, c, D_MODEL), lambda bb: (bb, 0, 0)),
        out_shape=jax.ShapeDtypeStruct((b, c, D_MODEL), BF16),
        compiler_params=_cparams(("parallel",)),
        name="attn_ctx",
    )(*args)


def _post_attn_kernel(o_ref, x_ref, wo_ref, gt_ref, g_ref, sh_ref, sc_ref, wrh_ref, wrl_ref, br_ref,
                      xn_ref, hf_ref, eid_ref, wcol_ref, cnt_ref):
    xn = x_ref[0] + gt_ref[0] * _dot(o_ref[0], wo_ref[...])
    xn_ref[0] = xn
    hf = _rms_mod(xn, g_ref[0], sh_ref[0], sc_ref[0])
    hf_ref[0] = hf
    tm = hf.shape[0]

    hh, hl = _split_bf16(hf)
    lg = _dot(hh, wrh_ref[...]) + _dot(hl, wrh_ref[...]) + _dot(hh, wrl_ref[...]) + br_ref[...]
    lgt = lg.T

    gl = lgt[0:N_GROUPS]
    ge = jnp.exp(gl - jnp.max(gl, axis=0, keepdims=True))
    pg = ge / jnp.sum(ge, axis=0, keepdims=True)
    p_top = jnp.max(pg, axis=0, keepdims=True)
    grow = lax.broadcasted_iota(I32, (N_GROUPS, tm), 0)
    grp = jnp.min(jnp.where(pg == p_top, grow, N_GROUPS), axis=0, keepdims=True)

    el = lgt[8:8 + EXPERTS_PER_GROUP]
    for gi in range(1, N_GROUPS):
        el = jnp.where(grp == gi, lgt[8 + gi * EXPERTS_PER_GROUP:8 + (gi + 1) * EXPERTS_PER_GROUP], el)
    ee = jnp.exp(el - jnp.max(el, axis=0, keepdims=True))
    pe = ee / jnp.sum(ee, axis=0, keepdims=True)
    erow = lax.broadcasted_iota(I32, (EXPERTS_PER_GROUP, tm), 0)
    p1 = jnp.max(pe, axis=0, keepdims=True)
    i1 = jnp.min(jnp.where(pe == p1, erow, EXPERTS_PER_GROUP), axis=0, keepdims=True)
    pe2 = jnp.where(erow == i1, -1.0, pe)
    p2 = jnp.max(pe2, axis=0, keepdims=True)
    i2 = jnp.min(jnp.where(pe2 == p2, erow, EXPERTS_PER_GROUP), axis=0, keepdims=True)
    den = p1 + p2
    w1 = p_top * p1 / den
    w2 = p_top * p2 / den
    e1 = grp * EXPERTS_PER_GROUP + i1
    e2 = grp * EXPERTS_PER_GROUP + i2

    eid_ref[0:1, :] = e1
    eid_ref[1:2, :] = e2
    wrows = jnp.concatenate([w1, w2, jnp.zeros((ROUTER_COLS - 2, tm), F32)], axis=0)
    wcol_ref[...] = wrows.T
    xrow = lax.broadcasted_iota(I32, (N_EXPERTS, tm), 0)
    hits = (xrow == e1).astype(F32) + (xrow == e2).astype(F32)
    cnt_ref[0] = jnp.sum(hits, axis=1, keepdims=True).astype(I32)


def _post_attn(o, x, wo_bf, modr, g_rows, router, layer, ctx_row, tm):
    b, n, d = x.shape
    nt = n // tm
    t_all = b * n
    wrh, wrl, br = router
    tile = lambda bb, t: (bb, t, 0)
    const2 = lambda bb, t: (0, 0)
    return pl.pallas_call(
        _post_attn_kernel,
        grid=(b, nt),
        in_specs=[
            pl.BlockSpec((1, tm, d), tile),
            pl.BlockSpec((1, tm, d), tile),
            pl.BlockSpec((d, d), const2),
            _mod_spec(layer, 2, ctx_row),
            _layer_row_spec(layer),
            _mod_spec(layer, 3, ctx_row),
            _mod_spec(layer, 4, ctx_row),
            pl.BlockSpec((d, ROUTER_COLS), const2),
            pl.BlockSpec((d, ROUTER_COLS), const2),
            pl.BlockSpec((1, ROUTER_COLS), const2),
        ],
        out_specs=[
            pl.BlockSpec((1, tm, d), tile),
            pl.BlockSpec((1, tm, d), tile),
            pl.BlockSpec((2, tm), lambda bb, t: (0, bb * nt + t)),
            pl.BlockSpec((tm, ROUTER_COLS), lambda bb, t: (bb * nt + t, 0)),
            pl.BlockSpec((1, N_EXPERTS, 1), lambda bb, t: (bb * nt + t, 0, 0)),
        ],
        out_shape=[
            jax.ShapeDtypeStruct((b, n, d), F32),
            jax.ShapeDtypeStruct((b, n, d), F32),
            jax.ShapeDtypeStruct((2, t_all), I32),
            jax.ShapeDtypeStruct((t_all, ROUTER_COLS), F32),
            jax.ShapeDtypeStruct((b * nt, N_EXPERTS, 1), I32),
        ],
        compiler_params=_cparams(("parallel", "parallel")),
        name="post_attn_router",
    )(o, x, wo_bf, modr, g_rows, modr, modr, wrh, wrl, br)


def _dest_kernel(eid_ref, base_ref, tri_ref, dest_ref):
    tm = eid_ref.shape[1]
    xrow = lax.broadcasted_iota(I32, (N_EXPERTS, tm), 0)
    oh0 = xrow == eid_ref[0:1, :]
    oh1 = xrow == eid_ref[1:2, :]
    both = oh0.astype(F32) + oh1.astype(F32)
    incl = _dot(both.astype(BF16), tri_ref[...])
    before = incl - both + base_ref[0].astype(F32)
    d0 = jnp.sum(jnp.where(oh0, before, 0.0), axis=0, keepdims=True)
    d1 = jnp.sum(jnp.where(oh1, before, 0.0), axis=0, keepdims=True)
    dest_ref[0, 0:1, :] = d0.astype(I32)
    dest_ref[0, 1:2, :] = d1.astype(I32)


def _dest_rows(eid, base, tri):
    tm = tri.shape[0]
    nt = eid.shape[1] // tm
    return pl.pallas_call(
        _dest_kernel,
        grid=(nt,),
        in_specs=[
            pl.BlockSpec((2, tm), lambda i: (0, i)),
            pl.BlockSpec((1, N_EXPERTS, 1), lambda i: (i, 0, 0)),
            pl.BlockSpec((tm, tm), lambda i: (0, 0)),
        ],
        out_specs=pl.BlockSpec((1, 2, tm), lambda i: (i, 0, 0)),
        out_shape=jax.ShapeDtypeStruct((nt, 2, tm), I32),
        compiler_params=_cparams(("parallel",)),
        name="moe_dest",
    )(eid, base, tri)


def _scatter_kernel(dest_ref, hf_ref, xs_in_ref, xs_ref, sem):
    del xs_in_ref
    tm = hf_ref.shape[0]

    def issue(t, carry):
        for k in range(2):
            d = dest_ref[0, k, t]
            pltpu.make_async_copy(hf_ref.at[pl.ds(t, 1)], xs_ref.at[pl.ds(d, 1)], sem).start()
        return carry

    lax.fori_loop(0, tm, issue, 0, unroll=8)
    for k in range(2):
        pltpu.make_async_copy(hf_ref, xs_ref.at[pl.ds(0, tm)], sem).wait()


def _scatter_rows(dest, hf2d, xs):
    nt, _, tm = dest.shape
    d = hf2d.shape[1]
    return pl.pallas_call(
        _scatter_kernel,
        grid=(nt,),
        in_specs=[
            pl.BlockSpec((1, 2, tm), lambda i: (i, 0, 0), memory_space=pltpu.SMEM),
            pl.BlockSpec((tm, d), lambda i: (i, 0)),
            pl.BlockSpec(memory_space=pl.ANY),
        ],
        out_specs=pl.BlockSpec(memory_space=pl.ANY),
        out_shape=jax.ShapeDtypeStruct(xs.shape, xs.dtype),
        scratch_shapes=[pltpu.SemaphoreType.DMA(())],
        input_output_aliases={2: 0},
        compiler_params=_cparams(("arbitrary",)),
        name="moe_scatter",
    )(dest, hf2d, xs)


def _expert_kernel(blk_e_ref, nused_ref, x_ref, wg_ref, wu_ref, wd_ref, y_ref, wg_s, wu_s, wd_s):
    i = pl.program_id(0)
    prev = blk_e_ref[jnp.maximum(i - 1, 0)]
    fresh = (i == 0) | (blk_e_ref[i] != prev)

    @pl.when(fresh)
    def _():
        wg_s[...] = wg_ref[0].astype(BF16)
        wu_s[...] = wu_ref[0].astype(BF16)
        wd_s[...] = wd_ref[0].astype(BF16)

    @pl.when(i < nused_ref[0])
    def _():
        x = x_ref[...].astype(BF16)
        gate = _dot(x, wg_s[...])
        up = _dot(x, wu_s[...])
        mid = gate * jax.nn.sigmoid(gate) * up
        y_ref[...] = _dot(mid.astype(BF16), wd_s[...])

    @pl.when(i >= nused_ref[0])
    def _():
        y_ref[...] = jnp.zeros_like(y_ref)


def _experts(blk_e, nused, xs, w_gate, w_up, w_down):
    cap, d = xs.shape
    f = w_gate.shape[2]
    grid_spec = pltpu.PrefetchScalarGridSpec(
        num_scalar_prefetch=2,
        grid=(cap // EXPERT_BLOCK,),
        in_specs=[
            pl.BlockSpec((EXPERT_BLOCK, d), lambda i, be, nu: (i, 0)),
            pl.BlockSpec((1, d, f), lambda i, be, nu: (be[i], 0, 0)),
            pl.BlockSpec((1, d, f), lambda i, be, nu: (be[i], 0, 0)),
            pl.BlockSpec((1, f, d), lambda i, be, nu: (be[i], 0, 0)),
        ],
        out_specs=pl.BlockSpec((EXPERT_BLOCK, d), lambda i, be, nu: (i, 0)),
        scratch_shapes=[pltpu.VMEM((d, f), BF16), pltpu.VMEM((d, f), BF16), pltpu.VMEM((f, d), BF16)],
    )
    return pl.pallas_call(
        _expert_kernel,
        grid_spec=grid_spec,
        out_shape=jax.ShapeDtypeStruct((cap, d), F32),
        compiler_params=_cparams(("arbitrary",)),
        name="moe_experts",
    )(blk_e, nused, xs, w_gate, w_up, w_down)


def _combine_kernel(*refs, final_norm):
    if final_norm:
        dest_ref, xn_ref, wcol_ref, gt_ref, gfin_ref, y_ref, o_ref, ybuf, sem = refs
    else:
        dest_ref, xn_ref, wcol_ref, gt_ref, y_ref, o_ref, ybuf, sem = refs
    tm = xn_ref.shape[1]

    def issue(t, carry):
        for k in range(2):
            d = dest_ref[0, k, t]
            pltpu.make_async_copy(y_ref.at[pl.ds(d, 1)], ybuf.at[k, pl.ds(t, 1)], sem).start()
        return carry

    lax.fori_loop(0, tm, issue, 0, unroll=8)
    for k in range(2):
        pltpu.make_async_copy(y_ref.at[pl.ds(0, tm)], ybuf.at[k], sem).wait()
    w = wcol_ref[...]
    moe = w[:, 0:1] * ybuf[0] + w[:, 1:2] * ybuf[1]
    out = xn_ref[0] + gt_ref[0] * moe
    if final_norm:
        ms = jnp.mean(out * out, axis=-1, keepdims=True)
        out = out * lax.rsqrt(ms + NORM_EPS) * gfin_ref[...]
    o_ref[0] = out


def _combine(dest, xn, wcol, modr, y, layer, ctx_row, g_final=None):
    b, n, d = xn.shape
    nt_all, _, tm = dest.shape
    nt = n // tm
    final_norm = g_final is not None
    args = [dest, xn, wcol, modr]
    in_specs = [
        pl.BlockSpec((1, 2, tm), lambda bb, t: (bb * nt + t, 0, 0), memory_space=pltpu.SMEM),
        pl.BlockSpec((1, tm, d), lambda bb, t: (bb, t, 0)),
        pl.BlockSpec((tm, ROUTER_COLS), lambda bb, t: (bb * nt + t, 0)),
        _mod_spec(layer, 5, ctx_row),
    ]
    if final_norm:
        args.append(g_final.reshape(1, d))
        in_specs.append(pl.BlockSpec((1, d), lambda bb, t: (0, 0)))
    args.append(y)
    in_specs.append(pl.BlockSpec(memory_space=pl.ANY))
    return pl.pallas_call(
        functools.partial(_combine_kernel, final_norm=final_norm),
        grid=(b, nt),
        in_specs=in_specs,
        out_specs=pl.BlockSpec((1, tm, d), lambda bb, t: (bb, t, 0)),
        out_shape=jax.ShapeDtypeStruct((b, n, d), F32),
        scratch_shapes=[pltpu.VMEM((2, tm, d), F32), pltpu.SemaphoreType.DMA(())],
        compiler_params=_cparams(("arbitrary", "arbitrary")),
        name="moe_combine",
    )(*args)


def _rope_tables(n):
    t = np.arange(n)
    row = (t // GRID_W).astype(np.float32)
    col = (t % GRID_W).astype(np.float32)
    quarter = HEAD_DIM // 4
    inv = jnp.asarray(ROPE_THETA, F32) ** (-jnp.arange(quarter, dtype=F32) / quarter)
    ar = jnp.asarray(row)[:, None] * inv
    ac = jnp.asarray(col)[:, None] * inv
    ang = jnp.concatenate([ar, ar, ac, ac], axis=-1)
    ang = jnp.concatenate([ang, ang], axis=-1)
    sign = np.where((np.arange(LANES) % (HEAD_DIM // 2)) < quarter, -1.0, 1.0).astype(np.float32)
    return jnp.cos(ang), jnp.sin(ang) * sign


def _na_bias_tables(rpb):
    i = np.arange(NA_TABLE)[:, None, None]
    c = np.arange(GRID_W)[None, :, None]
    lane = np.arange(LANES)[None, None, :]
    dr = i - NA_TABLE // 2 + lane // GRID_W
    kc = lane % GRID_W
    cs = np.clip(c - NA_WIN_W // 2, 0, GRID_W - NA_WIN_W)
    valid = (np.abs(dr) <= NA_WIN_H - 1) & (kc >= cs) & (kc < cs + NA_WIN_W)
    rel = (dr + NA_WIN_H - 1) * (2 * NA_WIN_W - 1) + (kc - c + NA_WIN_W - 1)
    rel = np.where(valid, rel, 0).astype(np.int32)
    flat = rpb.reshape(N_HEADS, -1).astype(F32)
    return jnp.where(jnp.asarray(valid)[None], flat[:, jnp.asarray(rel)], NEG_INF)


def _router_tables(w_group, b_group, w_router, b_router):
    d = w_group.shape[0]
    w = jnp.zeros((d, ROUTER_COLS), F32)
    w = w.at[:, 0:N_GROUPS].set(w_group).at[:, 8:8 + N_EXPERTS].set(w_router)
    bias = jnp.zeros((1, ROUTER_COLS), F32)
    bias = bias.at[0, 0:N_GROUPS].set(b_group).at[0, 8:8 + N_EXPERTS].set(b_router)
    hi = w.astype(BF16)
    lo = (w - hi.astype(F32)).astype(BF16)
    return hi, lo, bias


def _moe_layout(cnt_tiles, cap):
    cnt = cnt_tiles[:, :, 0]
    counts = jnp.sum(cnt, axis=0)
    padded = (counts + EXPERT_BLOCK - 1) // EXPERT_BLOCK * EXPERT_BLOCK
    pad_end = jnp.cumsum(padded)
    pad_start = pad_end - padded
    base = pad_start[None, :] + jnp.cumsum(cnt, axis=0) - cnt
    n_blocks = cap // EXPERT_BLOCK
    blk = jnp.arange(n_blocks, dtype=I32)
    blk_e = jnp.minimum(jnp.searchsorted(pad_end, blk * EXPERT_BLOCK, side="right"), N_EXPERTS - 1).astype(I32)
    nused = (pad_end[-1] // EXPERT_BLOCK).astype(I32)
    last_e = blk_e[jnp.maximum(nused - 1, 0)]
    blk_e = jnp.where(blk < nused, blk_e, last_e)
    return base[:, :, None].astype(I32), blk_e, nused.reshape(1)


def kernel(x, c, ctx, c_ctx, w_ada, b_ada, g_attn, w_qkv, w_o, sink_a, gq_b, gk_b, rpb_c, g_ffn,
           w_group, b_group, w_router, b_router, w_gate, w_up, w_down, g_final):
    b, n, d = x.shape
    n_ctx = ctx.shape[1]
    depth = w_ada.shape[0]
    assert b + 1 <= MOD_ROWS and d == D_MODEL and n % 512 == 0 and n_ctx % 256 == 0
    tm_lat, tm_ctx = 512, 256
    ctx_row = b

    cc = jnp.zeros((MOD_ROWS, d), F32).at[:b].set(c).at[b].set(c_ctx)
    modr = _ada_all(cc, w_ada, b_ada).reshape(depth * MOD_ROWS * N_MOD, 1, d)
    g_attn_rows = g_attn.reshape(depth, 1, d)
    g_ffn_rows = g_ffn.reshape(depth, 1, d)
    rope_tabs = _rope_tables(n)
    head_sum = jnp.asarray(np.kron(np.eye(LANES // HEAD_DIM), np.ones((HEAD_DIM, HEAD_DIM))), BF16)
    tri_lat = jnp.asarray(np.triu(np.ones((tm_lat, tm_lat))), BF16)
    tri_ctx = jnp.asarray(np.triu(np.ones((tm_ctx, tm_ctx))), BF16)

    xc = ctx
    for i in range(depth):
        m = i % N_MIXERS
        j = i // N_MIXERS
        last = i == depth - 1
        wqkv_bf = w_qkv[i].astype(BF16)
        wo_bf = w_o[i].astype(BF16)
        qk_gains = None
        if m == 1:
            qk_gains = (jnp.tile(gq_b[j], LANES // HEAD_DIM).reshape(1, LANES),
                        jnp.tile(gk_b[j], LANES // HEAD_DIM).reshape(1, LANES), head_sum)

        if m == 2:
            qp, k, v = _qkv(x, modr, g_attn_rows, wqkv_bf, i, None, tm_lat, None, qk_gains)
            qr = None
        else:
            qp, qr, k, v = _qkv(x, modr, g_attn_rows, wqkv_bf, i, None, tm_lat, rope_tabs, qk_gains)
        qc, kc, vc = _qkv(xc, modr, g_attn_rows, wqkv_bf, i, ctx_row, tm_ctx, None, qk_gains)
        sink = sink_a[j] if m == 0 else None
        extra = sink if m == 0 else (_na_bias_tables(rpb_c[j]) if m == 2 else None)
        o = _attention(m, qp, qr, k, v, kc, vc, extra)

        router = _router_tables(w_group[i], b_group[i], w_router[i], b_router[i])
        streams = [(o, x, None, tm_lat, tri_lat)]
        if not last:
            oc = _ctx_attention(qc, kc, vc, sink)
            streams.append((oc, xc, ctx_row, tm_ctx, tri_ctx))
        routed = [_post_attn(o_s, x_s, wo_bf, modr, g_ffn_rows, router, i, row, tm)
                  for (o_s, x_s, row, tm, _) in streams]

        n_asg = 2 * sum(r[2].shape[1] for r in routed)
        cap = -(-n_asg // EXPERT_BLOCK) * EXPERT_BLOCK + N_EXPERTS * EXPERT_BLOCK
        base, blk_e, nused = _moe_layout(jnp.concatenate([r[4] for r in routed], axis=0), cap)
        xs = jnp.zeros((cap, d), F32)
        dests = []
        tile0 = 0
        for (xn_s, hf_s, eid_s, _, cnt_s), (_, _, _, _, tri) in zip(routed, streams):
            nt_s = cnt_s.shape[0]
            dest_s = _dest_rows(eid_s, base[tile0:tile0 + nt_s], tri)
            xs = _scatter_rows(dest_s, hf_s.reshape(-1, d), xs)
            dests.append(dest_s)
            tile0 += nt_s
        y = _experts(blk_e, nused, xs, w_gate[i], w_up[i], w_down[i])
        x = _combine(dests[0], routed[0][0], routed[0][3], modr, y, i, None, g_final if last else None)
        if not last:
            xc = _combine(dests[1], routed[1][0], routed[1][3], modr, y, i, ctx_row)
    return x
```

```python
import functools

import numpy as np
import jax
import jax.numpy as jnp
from jax import lax
from jax.experimental import pallas as pl
from jax.experimental.pallas import tpu as pltpu

F32 = jnp.float32
BF16 = jnp.bfloat16
I32 = jnp.int32

D_MODEL = 1024
GRID_W = 64
N_MIXERS = 3
N_HEADS = 16
N_KV_HEADS = 4
HEAD_DIM = D_MODEL // N_HEADS
Q_PER_KV = N_HEADS // N_KV_HEADS
QKV_DIM = (N_HEADS + 2 * N_KV_HEADS) * HEAD_DIM
ATTN_SCALE = HEAD_DIM ** -0.5
ROPE_THETA = 10000.0
BLOCK_Q = 128
WINDOW = 128
NA_WIN_H = 8
NA_WIN_W = 16
N_GROUPS = 4
EXPERTS_PER_GROUP = 8
N_EXPERTS = N_GROUPS * EXPERTS_PER_GROUP
EXPERT_HIDDEN = D_MODEL // 2
NORM_EPS = 1e-6
NEG_INF = -1e30
LOG2E = 1.4426950408889634

LANES = 128
MOD_ROWS = 24
N_MOD = 6
EXPERT_BLOCK = 256
ROUTER_COLS = LANES
NA_KEY_ROWS = 10
NA_TABLE = 18
VMEM_LIMIT = 56 * 1024 * 1024


def _cparams(sem, vmem=VMEM_LIMIT):
    return pltpu.CompilerParams(dimension_semantics=sem, vmem_limit_bytes=vmem)


def _dot(a, b):
    return jnp.dot(a, b, preferred_element_type=F32)


def _dot_nt(a, b):
    return lax.dot_general(a, b, (((1,), (1,)), ((), ())), preferred_element_type=F32)


def _split_bf16(x):
    hi = x.astype(BF16)
    lo = (x - hi.astype(F32)).astype(BF16)
    return hi, lo


def _rms_mod(x, g, sh, sc):
    ms = jnp.mean(x * x, axis=-1, keepdims=True)
    y = x * lax.rsqrt(ms + NORM_EPS) * g
    return y * (1.0 + sc) + sh


def _ada_kernel(c_ref, w_ref, b_ref, o_ref):
    c = c_ref[...]
    s = c * jax.nn.sigmoid(c)
    o_ref[0] = _dot(s.astype(BF16), w_ref[0].astype(BF16)) + b_ref[0]


def _ada_all(cc, w_ada, b_ada):
    depth, d, d6 = w_ada.shape
    tn = 1536
    return pl.pallas_call(
        _ada_kernel,
        grid=(depth, d6 // tn),
        in_specs=[
            pl.BlockSpec((MOD_ROWS, d), lambda i, j: (0, 0)),
            pl.BlockSpec((1, d, tn), lambda i, j: (i, 0, j)),
            pl.BlockSpec((1, 1, tn), lambda i, j: (i, 0, j)),
        ],
        out_specs=pl.BlockSpec((1, MOD_ROWS, tn), lambda i, j: (i, 0, j)),
        out_shape=jax.ShapeDtypeStruct((depth, MOD_ROWS, d6), F32),
        compiler_params=_cparams(("parallel", "parallel")),
        name="ada_mod",
    )(cc, w_ada, b_ada.reshape(depth, 1, d6))


def _mod_spec(layer, chunk, ctx_row):
    if ctx_row is None:
        return pl.BlockSpec((1, 1, D_MODEL), lambda b, t: ((layer * MOD_ROWS + b) * N_MOD + chunk, 0, 0))
    return pl.BlockSpec((1, 1, D_MODEL), lambda b, t: ((layer * MOD_ROWS + ctx_row) * N_MOD + chunk, 0, 0))


def _layer_row_spec(layer):
    return pl.BlockSpec((1, 1, D_MODEL), lambda b, t: (layer, 0, 0))


def _qkv_kernel(*refs, qk_norm, rope):
    x_ref, g_ref, sh_ref, sc_ref, w_ref = refs[:5]
    pos = 5
    if rope:
        cos_ref, sin_ref = refs[pos:pos + 2]
        pos += 2
    if qk_norm:
        gq_ref, gk_ref, bm_ref = refs[pos:pos + 3]
        pos += 3
    if rope:
        qp_ref, qr_ref, k_ref, v_ref = refs[pos:]
    else:
        qp_ref, k_ref, v_ref = refs[pos:]

    h = _rms_mod(x_ref[0], g_ref[0], sh_ref[0], sc_ref[0])
    res = _dot(h.astype(BF16), w_ref[...])
    half = LANES // 2
    n_q = N_HEADS * HEAD_DIM // LANES
    n_k = N_KV_HEADS * HEAD_DIM // LANES
    if rope:
        lane = lax.broadcasted_iota(I32, (1, LANES), 1)
        first_half = (lane & (HEAD_DIM // 2 - 1)) < (HEAD_DIM // 4)
    for cb in range(QKV_DIM // LANES):
        xs = res[:, cb * LANES:(cb + 1) * LANES]
        is_q = cb < n_q
        is_k = n_q <= cb < n_q + n_k
        if qk_norm and (is_q or is_k):
            hi, lo = _split_bf16(xs * xs)
            ms = (_dot(hi, bm_ref[...]) + _dot(lo, bm_ref[...])) * (1.0 / HEAD_DIM)
            gain = gq_ref[...] if is_q else gk_ref[...]
            xs = xs * lax.rsqrt(ms + NORM_EPS) * gain
        if is_q:
            xs = xs * (ATTN_SCALE * LOG2E)
        if rope and (is_q or is_k):
            rot = jnp.where(first_half, pltpu.roll(xs, LANES - HEAD_DIM // 4, 1), pltpu.roll(xs, HEAD_DIM // 4, 1))
            xr = xs * cos_ref[...] + rot * sin_ref[...]
        if is_q:
            hd = 2 * cb
            qp_ref[0, hd] = xs[:, :half].astype(BF16)
            qp_ref[0, hd + 1] = xs[:, half:].astype(BF16)
            if rope:
                qr_ref[0, hd] = xr[:, :half].astype(BF16)
                qr_ref[0, hd + 1] = xr[:, half:].astype(BF16)
        elif is_k:
            hd = 2 * (cb - n_q)
            kk = xr if rope else xs
            k_ref[0, hd] = kk[:, :half].astype(BF16)
            k_ref[0, hd + 1] = kk[:, half:].astype(BF16)
        else:
            hd = 2 * (cb - n_q - n_k)
            vlane = lax.broadcasted_iota(I32, (1, LANES), 1)
            ones_col = (vlane == HEAD_DIM).astype(F32)
            v_ref[0, hd] = jnp.where(vlane < HEAD_DIM, xs, ones_col).astype(BF16)
            v_ref[0, hd + 1] = jnp.where(vlane < HEAD_DIM, pltpu.roll(xs, half, 1), ones_col).astype(BF16)


def _qkv(x, modr, g_rows, w_bf, layer, ctx_row, tm, rope_tabs=None, qk_gains=None):
    b, n, d = x.shape
    rope = rope_tabs is not None
    qk_norm = qk_gains is not None
    args = [x, g_rows, modr, modr, w_bf]
    in_specs = [
        pl.BlockSpec((1, tm, d), lambda bb, t: (bb, t, 0)),
        _layer_row_spec(layer),
        _mod_spec(layer, 0, ctx_row),
        _mod_spec(layer, 1, ctx_row),
        pl.BlockSpec((d, QKV_DIM), lambda bb, t: (0, 0)),
    ]
    if rope:
        args += list(rope_tabs)
        in_specs += [pl.BlockSpec((tm, LANES), lambda bb, t: (t, 0))] * 2
    if qk_norm:
        args += list(qk_gains)
        in_specs += [pl.BlockSpec((1, LANES), lambda bb, t: (0, 0))] * 2
        in_specs += [pl.BlockSpec((LANES, LANES), lambda bb, t: (0, 0))]
    q_shape = jax.ShapeDtypeStruct((b, N_HEADS, n, HEAD_DIM), BF16)
    kv_shape = jax.ShapeDtypeStruct((b, N_KV_HEADS, n, HEAD_DIM), BF16)
    q_spec = pl.BlockSpec((1, N_HEADS, tm, HEAD_DIM), lambda bb, t: (bb, 0, t, 0))
    kv_spec = pl.BlockSpec((1, N_KV_HEADS, tm, HEAD_DIM), lambda bb, t: (bb, 0, t, 0))
    v_shape = jax.ShapeDtypeStruct((b, N_KV_HEADS, n, LANES), BF16)
    v_spec = pl.BlockSpec((1, N_KV_HEADS, tm, LANES), lambda bb, t: (bb, 0, t, 0))
    if rope:
        out_shape, out_specs = [q_shape, q_shape, kv_shape, v_shape], [q_spec, q_spec, kv_spec, v_spec]
    else:
        out_shape, out_specs = [q_shape, kv_shape, v_shape], [q_spec, kv_spec, v_spec]
    return pl.pallas_call(
        functools.partial(_qkv_kernel, qk_norm=qk_norm, rope=rope),
        grid=(b, n // tm),
        in_specs=in_specs,
        out_specs=out_specs,
        out_shape=out_shape,
        compiler_params=_cparams(("parallel", "parallel")),
        name="qkv_rope" if rope else "qkv",
    )(*args)


def _attend(s, v_aug):
    m = jnp.max(s, axis=-1, keepdims=True)
    p = jnp.exp2(s - m)
    return m, _dot(p.astype(BF16), v_aug)


def _normalise(acc):
    return acc[:, :HEAD_DIM] / acc[:, HEAD_DIM:HEAD_DIM + 1]


def _normalise_with_sink(m, acc, sink_ref, kv_head, rows_per_head):
    pieces = []
    for g in range(Q_PER_KV):
        sink = sink_ref[kv_head * Q_PER_KV + g] * LOG2E
        mg = m[g * rows_per_head:(g + 1) * rows_per_head]
        ag = acc[g * rows_per_head:(g + 1) * rows_per_head]
        m2 = jnp.maximum(mg, sink)
        scale = jnp.exp2(mg - m2)
        denom = ag[:, HEAD_DIM:HEAD_DIM + 1] * scale + jnp.exp2(sink - m2)
        pieces.append(ag[:, :HEAD_DIM] * scale / denom)
    return pieces


def _split_heads(o, rows_per_head):
    return [o[g * rows_per_head:(g + 1) * rows_per_head] for g in range(Q_PER_KV)]


def _attn_kernel(*refs, mode, key_chunk):
    if mode == 0:
        qp_ref, qr_ref, k_ref, v_ref, kc_ref, vc_ref, sink_ref, o_ref = refs
    elif mode == 1:
        qp_ref, qr_ref, k_ref, v_ref, kc_ref, vc_ref, o_ref = refs
    else:
        qp_ref, k_ref, v_ref, kc_ref, vc_ref, t2_ref, o_ref = refs
    qb = pl.program_id(1)
    n = k_ref.shape[2]
    rows = Q_PER_KV * BLOCK_Q

    if mode == 0:
        span = BLOCK_Q + 2 * WINDOW
        kstart = pl.multiple_of(jnp.clip(qb * BLOCK_Q - WINDOW, 0, n - span), BLOCK_Q)
        qpos = qb * BLOCK_Q + (lax.broadcasted_iota(I32, (rows, span), 0) & (BLOCK_Q - 1))
        kpos = kstart + lax.broadcasted_iota(I32, (rows, span), 1)
        band = jnp.abs(kpos - qpos) <= WINDOW
    elif mode == 2:
        rows_grid = n // GRID_W
        r0 = qb * (BLOCK_Q // GRID_W)
        wr = jnp.clip(r0 - NA_WIN_H // 2, 0, rows_grid - NA_KEY_ROWS)
        kstart = pl.multiple_of(wr * GRID_W, GRID_W)
        lane = lax.broadcasted_iota(I32, (GRID_W, LANES), 1)

    pieces = []
    for h in range(N_KV_HEADS):
        qp = qp_ref[0, h * Q_PER_KV:(h + 1) * Q_PER_KV].reshape(rows, HEAD_DIM)
        s_ctx = _dot_nt(qp, kc_ref[0, h])
        if mode == 0:
            qr = qr_ref[0, h * Q_PER_KV:(h + 1) * Q_PER_KV].reshape(rows, HEAD_DIM)
            s = jnp.where(band, _dot_nt(qr, k_ref[0, h, pl.ds(kstart, span), :]), NEG_INF)
            v = jnp.concatenate([v_ref[0, h, pl.ds(kstart, span), :], vc_ref[0, h]], axis=0)
            m, acc = _attend(jnp.concatenate([s, s_ctx], axis=1), v)
            pieces += _normalise_with_sink(m, acc, sink_ref, h, BLOCK_Q)
        elif mode == 1:
            qr = qr_ref[0, h * Q_PER_KV:(h + 1) * Q_PER_KV].reshape(rows, HEAD_DIM)
            n_chunks = n // key_chunk
            m, acc = _attend(_dot_nt(qr, k_ref[0, h, 0:key_chunk, :]), v_ref[0, h, 0:key_chunk, :])
            for c in range(1, n_chunks):
                s = _dot_nt(qr, k_ref[0, h, c * key_chunk:(c + 1) * key_chunk, :])
                v = v_ref[0, h, c * key_chunk:(c + 1) * key_chunk, :]
                if c == n_chunks - 1:
                    s = jnp.concatenate([s, s_ctx], axis=1)
                    v = jnp.concatenate([v, vc_ref[0, h]], axis=0)
                m_new = jnp.maximum(m, jnp.max(s, axis=-1, keepdims=True))
                p = jnp.exp2(s - m_new)
                acc = jnp.exp2(m - m_new) * acc + _dot(p.astype(BF16), v)
                m = m_new
            pieces += _split_heads(_normalise(acc), BLOCK_Q)
        else:
            n_keys = NA_KEY_ROWS * GRID_W
            s = _dot_nt(qp, k_ref[0, h, pl.ds(kstart, n_keys), :])
            blocks = []
            for g in range(Q_PER_KV):
                for a in range(BLOCK_Q // GRID_W):
                    r = r0 + a
                    rs = jnp.clip(r - NA_WIN_H // 2, 0, rows_grid - NA_WIN_H)
                    tiles = []
                    for j in range(n_keys // LANES):
                        kr0 = wr + 2 * j
                        tab = t2_ref[h * Q_PER_KV + g, kr0 - r + NA_TABLE // 2]
                        pen0 = jnp.where((kr0 >= rs) & (kr0 < rs + NA_WIN_H), 0.0, NEG_INF)
                        pen1 = jnp.where((kr0 + 1 >= rs) & (kr0 + 1 < rs + NA_WIN_H), 0.0, NEG_INF)
                        tiles.append(tab + jnp.where(lane < GRID_W, pen0, pen1))
                    blocks.append(jnp.concatenate(tiles, axis=1))
            s = s + jnp.concatenate(blocks, axis=0)
            v = jnp.concatenate([v_ref[0, h, pl.ds(kstart, n_keys), :], vc_ref[0, h]], axis=0)
            _, acc = _attend(jnp.concatenate([s, s_ctx], axis=1), v)
            pieces += _split_heads(_normalise(acc), BLOCK_Q)
    o_ref[0] = jnp.concatenate(pieces, axis=-1).astype(o_ref.dtype)


def _attention(mode, qp, qr, k, v, kc, vc, extra):
    b, _, n, _ = qp.shape
    c = kc.shape[2]
    q_spec = pl.BlockSpec((1, N_HEADS, BLOCK_Q, HEAD_DIM), lambda bb, i: (bb, 0, i, 0))
    k_spec = pl.BlockSpec((1, N_KV_HEADS, n, HEAD_DIM), lambda bb, i: (bb, 0, 0, 0))
    v_spec = pl.BlockSpec((1, N_KV_HEADS, n, LANES), lambda bb, i: (bb, 0, 0, 0))
    kc_spec = pl.BlockSpec((1, N_KV_HEADS, c, HEAD_DIM), lambda bb, i: (bb, 0, 0, 0))
    vc_spec = pl.BlockSpec((1, N_KV_HEADS, c, LANES), lambda bb, i: (bb, 0, 0, 0))
    if mode == 2:
        args = [qp, k, v, kc, vc, extra]
        in_specs = [q_spec, k_spec, v_spec, kc_spec, vc_spec,
                    pl.BlockSpec(extra.shape, lambda bb, i: (0, 0, 0, 0))]
    else:
        args = [qp, qr, k, v, kc, vc]
        in_specs = [q_spec, q_spec, k_spec, v_spec, kc_spec, vc_spec]
        if mode == 0:
            args.append(extra)
            in_specs.append(pl.BlockSpec(memory_space=pltpu.SMEM))
    return pl.pallas_call(
        functools.partial(_attn_kernel, mode=mode, key_chunk=512),
        grid=(b, n // BLOCK_Q),
        in_specs=in_specs,
        out_specs=pl.BlockSpec((1, BLOCK_Q, D_MODEL), lambda bb, i: (bb, i, 0)),
        out_shape=jax.ShapeDtypeStruct((b, n, D_MODEL), BF16),
        compiler_params=_cparams(("parallel", "arbitrary")),
        name=("attn_window", "attn_global", "attn_na")[mode],
    )(*args)


def _ctx_attn_kernel(*refs, has_sink):
    if has_sink:
        q_ref, k_ref, v_ref, sink_ref, o_ref = refs
    else:
        q_ref, k_ref, v_ref, o_ref = refs
    c = k_ref.shape[2]
    rows = Q_PER_KV * c
    pieces = []
    for h in range(N_KV_HEADS):
        q = q_ref[0, h * Q_PER_KV:(h + 1) * Q_PER_KV].reshape(rows, HEAD_DIM)
        m, acc = _attend(_dot_nt(q, k_ref[0, h]), v_ref[0, h])
        if has_sink:
            pieces += _normalise_with_sink(m, acc, sink_ref, h, c)
        else:
            pieces += _split_heads(_normalise(acc), c)
    o_ref[0] = jnp.concatenate(pieces, axis=-1).astype(o_ref.dtype)


def _ctx_attention(qc, kc, vc, sink):
    b, _, c, _ = qc.shape
    args = [qc, kc, vc]
    in_specs = [pl.BlockSpec((1, N_HEADS, c, HEAD_DIM), lambda bb: (bb, 0, 0, 0)),
                pl.BlockSpec((1, N_KV_HEADS, c, HEAD_DIM), lambda bb: (bb, 0, 0, 0)),
                pl.BlockSpec((1, N_KV_HEADS, c, LANES), lambda bb: (bb, 0, 0, 0))]
    if sink is not None:
        args.append(sink)
        in_specs.append(pl.BlockSpec(memory_space=pltpu.SMEM))
    return pl.pallas_call(
        functools.partial(_ctx_attn_kernel, has_sink=sink is not None),
        grid=(b,),
        in_specs=in_specs,
        out_specs=pl.BlockSpec((1, c, D_MODEL), lambda bb: (bb, 0, 0)),
        out_shape=jax.ShapeDtypeStruct((b, c, D_MODEL), BF16),
        compiler_params=_cparams(("parallel",)),
        name="attn_ctx",
    )(*args)


def _post_attn_kernel(o_ref, x_ref, wo_ref, gt_ref, g_ref, sh_ref, sc_ref, wrh_ref, wrl_ref, br_ref,
                      xn_ref, hf_ref, eid_ref, wcol_ref, cnt_ref):
    xn = x_ref[0] + gt_ref[0] * _dot(o_ref[0], wo_ref[...])
    xn_ref[0] = xn
    hf = _rms_mod(xn, g_ref[0], sh_ref[0], sc_ref[0])
    hf_ref[0] = hf
    tm = hf.shape[0]

    hh, hl = _split_bf16(hf)
    lg = _dot(hh, wrh_ref[...]) + _dot(hl, wrh_ref[...]) + _dot(hh, wrl_ref[...]) + br_ref[...]
    lgt = lg.T

    gl = lgt[0:N_GROUPS]
    ge = jnp.exp(gl - jnp.max(gl, axis=0, keepdims=True))
    pg = ge / jnp.sum(ge, axis=0, keepdims=True)
    p_top = jnp.max(pg, axis=0, keepdims=True)
    grow = lax.broadcasted_iota(I32, (N_GROUPS, tm), 0)
    grp = jnp.min(jnp.where(pg == p_top, grow, N_GROUPS), axis=0, keepdims=True)

    el = lgt[8:8 + EXPERTS_PER_GROUP]
    for gi in range(1, N_GROUPS):
        el = jnp.where(grp == gi, lgt[8 + gi * EXPERTS_PER_GROUP:8 + (gi + 1) * EXPERTS_PER_GROUP], el)
    ee = jnp.exp(el - jnp.max(el, axis=0, keepdims=True))
    pe = ee / jnp.sum(ee, axis=0, keepdims=True)
    erow = lax.broadcasted_iota(I32, (EXPERTS_PER_GROUP, tm), 0)
    p1 = jnp.max(pe, axis=0, keepdims=True)
    i1 = jnp.min(jnp.where(pe == p1, erow, EXPERTS_PER_GROUP), axis=0, keepdims=True)
    pe2 = jnp.where(erow == i1, -1.0, pe)
    p2 = jnp.max(pe2, axis=0, keepdims=True)
    i2 = jnp.min(jnp.where(pe2 == p2, erow, EXPERTS_PER_GROUP), axis=0, keepdims=True)
    den = p1 + p2
    w1 = p_top * p1 / den
    w2 = p_top * p2 / den
    e1 = grp * EXPERTS_PER_GROUP + i1
    e2 = grp * EXPERTS_PER_GROUP + i2

    eid_ref[0:1, :] = e1
    eid_ref[1:2, :] = e2
    wrows = jnp.concatenate([w1, w2, jnp.zeros((ROUTER_COLS - 2, tm), F32)], axis=0)
    wcol_ref[...] = wrows.T
    xrow = lax.broadcasted_iota(I32, (N_EXPERTS, tm), 0)
    hits = (xrow == e1).astype(F32) + (xrow == e2).astype(F32)
    cnt_ref[0] = jnp.sum(hits, axis=1, keepdims=True).astype(I32)


def _post_attn(o, x, wo_bf, modr, g_rows, router, layer, ctx_row, tm):
    b, n, d = x.shape
    nt = n // tm
    t_all = b * n
    wrh, wrl, br = router
    tile = lambda bb, t: (bb, t, 0)
    const2 = lambda bb, t: (0, 0)
    return pl.pallas_call(
        _post_attn_kernel,
        grid=(b, nt),
        in_specs=[
            pl.BlockSpec((1, tm, d), tile),
            pl.BlockSpec((1, tm, d), tile),
            pl.BlockSpec((d, d), const2),
            _mod_spec(layer, 2, ctx_row),
            _layer_row_spec(layer),
            _mod_spec(layer, 3, ctx_row),
            _mod_spec(layer, 4, ctx_row),
            pl.BlockSpec((d, ROUTER_COLS), const2),
            pl.BlockSpec((d, ROUTER_COLS), const2),
            pl.BlockSpec((1, ROUTER_COLS), const2),
        ],
        out_specs=[
            pl.BlockSpec((1, tm, d), tile),
            pl.BlockSpec((1, tm, d), tile),
            pl.BlockSpec((2, tm), lambda bb, t: (0, bb * nt + t)),
            pl.BlockSpec((tm, ROUTER_COLS), lambda bb, t: (bb * nt + t, 0)),
            pl.BlockSpec((1, N_EXPERTS, 1), lambda bb, t: (bb * nt + t, 0, 0)),
        ],
        out_shape=[
            jax.ShapeDtypeStruct((b, n, d), F32),
            jax.ShapeDtypeStruct((b, n, d), F32),
            jax.ShapeDtypeStruct((2, t_all), I32),
            jax.ShapeDtypeStruct((t_all, ROUTER_COLS), F32),
            jax.ShapeDtypeStruct((b * nt, N_EXPERTS, 1), I32),
        ],
        compiler_params=_cparams(("parallel", "parallel")),
        name="post_attn_router",
    )(o, x, wo_bf, modr, g_rows, modr, modr, wrh, wrl, br)


def _dest_kernel(eid_ref, base_ref, tri_ref, dest_ref):
    tm = eid_ref.shape[1]
    xrow = lax.broadcasted_iota(I32, (N_EXPERTS, tm), 0)
    oh0 = xrow == eid_ref[0:1, :]
    oh1 = xrow == eid_ref[1:2, :]
    both = oh0.astype(F32) + oh1.astype(F32)
    incl = _dot(both.astype(BF16), tri_ref[...])
    before = incl - both + base_ref[0].astype(F32)
    d0 = jnp.sum(jnp.where(oh0, before, 0.0), axis=0, keepdims=True)
    d1 = jnp.sum(jnp.where(oh1, before, 0.0), axis=0, keepdims=True)
    dest_ref[0, 0:1, :] = d0.astype(I32)
    dest_ref[0, 1:2, :] = d1.astype(I32)


def _dest_rows(eid, base, tri):
    tm = tri.shape[0]
    nt = eid.shape[1] // tm
    return pl.pallas_call(
        _dest_kernel,
        grid=(nt,),
        in_specs=[
            pl.BlockSpec((2, tm), lambda i: (0, i)),
            pl.BlockSpec((1, N_EXPERTS, 1), lambda i: (i, 0, 0)),
            pl.BlockSpec((tm, tm), lambda i: (0, 0)),
        ],
        out_specs=pl.BlockSpec((1, 2, tm), lambda i: (i, 0, 0)),
        out_shape=jax.ShapeDtypeStruct((nt, 2, tm), I32),
        compiler_params=_cparams(("parallel",)),
        name="moe_dest",
    )(eid, base, tri)


def _scatter_kernel(*refs, tiles):
    dest_ref = refs[0]
    hf_refs = refs[1:1 + len(tiles)]
    xs_ref, sem = refs[1 + len(tiles):]
    tm = hf_refs[0].shape[0]
    i = pl.program_id(0)

    def scatter_tile(hf_ref):
        def issue(t, carry):
            for k in range(2):
                d = dest_ref[0, k, t]
                pltpu.make_async_copy(hf_ref.at[pl.ds(t, 1)], xs_ref.at[pl.ds(d, 1)], sem).start()
            return carry

        lax.fori_loop(0, tm, issue, 0, unroll=8)
        for k in range(2):
            pltpu.make_async_copy(hf_ref, xs_ref.at[pl.ds(0, tm)], sem).wait()

    first = 0
    for hf_ref, nt in zip(hf_refs, tiles):
        pl.when((i >= first) & (i < first + nt))(functools.partial(scatter_tile, hf_ref))
        first += nt


def _scatter_rows(dest, hf_streams, n_rows):
    nt_all, _, tm = dest.shape
    d = hf_streams[0].shape[1]
    tiles = tuple(h.shape[0] // tm for h in hf_streams)
    assert sum(tiles) == nt_all
    in_specs = [pl.BlockSpec((1, 2, tm), lambda i: (i, 0, 0), memory_space=pltpu.SMEM)]
    first = 0
    for nt in tiles:
        in_specs.append(pl.BlockSpec((tm, d), lambda i, first=first, nt=nt: (jnp.clip(i - first, 0, nt - 1), 0)))
        first += nt
    return pl.pallas_call(
        functools.partial(_scatter_kernel, tiles=tiles),
        grid=(nt_all,),
        in_specs=in_specs,
        out_specs=pl.BlockSpec(memory_space=pl.ANY),
        out_shape=jax.ShapeDtypeStruct((n_rows, d), F32),
        scratch_shapes=[pltpu.SemaphoreType.DMA(())],
        compiler_params=_cparams(("arbitrary",)),
        name="moe_scatter",
    )(dest, *hf_streams)


def _expert_kernel(ve_ref, vblk_ref, lo_ref, hi_ref, x_ref, wg_ref, wu_ref, wd_ref, y_ref, wg_s, wu_s, wd_s):
    v = pl.program_id(0)
    pv = jnp.maximum(v - 1, 0)
    new_expert = (v == 0) | (ve_ref[v] != ve_ref[pv])
    first_visit = (v == 0) | (vblk_ref[v] != vblk_ref[pv])
    lo = lo_ref[v]
    hi = hi_ref[v]

    @pl.when(new_expert)
    def _():
        wg_s[...] = wg_ref[0, 0].astype(BF16)
        wu_s[...] = wu_ref[0, 0].astype(BF16)
        wd_s[...] = wd_ref[0, 0].astype(BF16)

    @pl.when(hi > lo)
    def _():
        x = x_ref[...].astype(BF16)
        gate = _dot(x, wg_s[...])
        up = _dot(x, wu_s[...])
        mid = gate * jax.nn.sigmoid(gate) * up
        y = _dot(mid.astype(BF16), wd_s[...])
        row = lax.broadcasted_iota(I32, (EXPERT_BLOCK, 1), 0)
        mine = (row >= lo) & (row < hi)

        @pl.when(first_visit)
        def _():
            y_ref[...] = jnp.where(mine, y, 0.0)

        @pl.when(jnp.logical_not(first_visit))
        def _():
            y_ref[...] = jnp.where(mine, y, y_ref[...])


def _experts(visits, xs, w_gate, w_up, w_down, layer):
    n_rows, d = xs.shape
    f = w_gate.shape[3]
    ve, vblk, lo, hi = visits
    x_map = lambda v, ve_r, vb_r, lo_r, hi_r: (vb_r[v], 0)
    w_map = lambda v, ve_r, vb_r, lo_r, hi_r: (layer, ve_r[v], 0, 0)
    grid_spec = pltpu.PrefetchScalarGridSpec(
        num_scalar_prefetch=4,
        grid=(ve.shape[0],),
        in_specs=[
            pl.BlockSpec((EXPERT_BLOCK, d), x_map),
            pl.BlockSpec((1, 1, d, f), w_map),
            pl.BlockSpec((1, 1, d, f), w_map),
            pl.BlockSpec((1, 1, f, d), w_map),
        ],
        out_specs=pl.BlockSpec((EXPERT_BLOCK, d), x_map),
        scratch_shapes=[pltpu.VMEM((d, f), BF16), pltpu.VMEM((d, f), BF16), pltpu.VMEM((f, d), BF16)],
    )
    return pl.pallas_call(
        _expert_kernel,
        grid_spec=grid_spec,
        out_shape=jax.ShapeDtypeStruct((n_rows, d), F32),
        compiler_params=_cparams(("arbitrary",)),
        name="moe_experts",
    )(ve, vblk, lo, hi, xs, w_gate, w_up, w_down)


def _combine_kernel(*refs, final_norm):
    if final_norm:
        dest_ref, xn_ref, wcol_ref, gt_ref, gfin_ref, y_ref, o_ref, ybuf, sem = refs
    else:
        dest_ref, xn_ref, wcol_ref, gt_ref, y_ref, o_ref, ybuf, sem = refs
    tm = xn_ref.shape[1]

    def issue(t, carry):
        for k in range(2):
            d = dest_ref[0, k, t]
            pltpu.make_async_copy(y_ref.at[pl.ds(d, 1)], ybuf.at[k, pl.ds(t, 1)], sem).start()
        return carry

    lax.fori_loop(0, tm, issue, 0, unroll=8)
    for k in range(2):
        pltpu.make_async_copy(y_ref.at[pl.ds(0, tm)], ybuf.at[k], sem).wait()
    w = wcol_ref[...]
    moe = w[:, 0:1] * ybuf[0] + w[:, 1:2] * ybuf[1]
    out = xn_ref[0] + gt_ref[0] * moe
    if final_norm:
        ms = jnp.mean(out * out, axis=-1, keepdims=True)
        out = out * lax.rsqrt(ms + NORM_EPS) * gfin_ref[...]
    o_ref[0] = out


def _combine(dest, xn, wcol, modr, y, layer, ctx_row, g_final=None):
    b, n, d = xn.shape
    nt_all, _, tm = dest.shape
    nt = n // tm
    final_norm = g_final is not None
    args = [dest, xn, wcol, modr]
    in_specs = [
        pl.BlockSpec((1, 2, tm), lambda bb, t: (bb * nt + t, 0, 0), memory_space=pltpu.SMEM),
        pl.BlockSpec((1, tm, d), lambda bb, t: (bb, t, 0)),
        pl.BlockSpec((tm, ROUTER_COLS), lambda bb, t: (bb * nt + t, 0)),
        _mod_spec(layer, 5, ctx_row),
    ]
    if final_norm:
        args.append(g_final.reshape(1, d))
        in_specs.append(pl.BlockSpec((1, d), lambda bb, t: (0, 0)))
    args.append(y)
    in_specs.append(pl.BlockSpec(memory_space=pl.ANY))
    return pl.pallas_call(
        functools.partial(_combine_kernel, final_norm=final_norm),
        grid=(b, nt),
        in_specs=in_specs,
        out_specs=pl.BlockSpec((1, tm, d), lambda bb, t: (bb, t, 0)),
        out_shape=jax.ShapeDtypeStruct((b, n, d), F32),
        scratch_shapes=[pltpu.VMEM((2, tm, d), F32), pltpu.SemaphoreType.DMA(())],
        compiler_params=_cparams(("arbitrary", "arbitrary")),
        name="moe_combine",
    )(*args)


def _rope_tables(n):
    t = np.arange(n)
    row = (t // GRID_W).astype(np.float32)
    col = (t % GRID_W).astype(np.float32)
    quarter = HEAD_DIM // 4
    inv = jnp.asarray(ROPE_THETA, F32) ** (-jnp.arange(quarter, dtype=F32) / quarter)
    ar = jnp.asarray(row)[:, None] * inv
    ac = jnp.asarray(col)[:, None] * inv
    ang = jnp.concatenate([ar, ar, ac, ac], axis=-1)
    ang = jnp.concatenate([ang, ang], axis=-1)
    sign = np.where((np.arange(LANES) % (HEAD_DIM // 2)) < quarter, -1.0, 1.0).astype(np.float32)
    return jnp.cos(ang), jnp.sin(ang) * sign


def _na_bias_tables(rpb):
    n_dr, n_dc = 2 * NA_WIN_H - 1, 2 * NA_WIN_W - 1
    blocked = jnp.full((N_HEADS, 2, n_dc), NEG_INF, F32)
    rows = jnp.concatenate([blocked, rpb.astype(F32) * LOG2E, blocked], axis=1)
    assert rows.shape[1] == n_dr + 4 == NA_TABLE + 1
    pair = jnp.stack([rows[:, 0:NA_TABLE], rows[:, 1:NA_TABLE + 1]], axis=2)
    c = np.arange(GRID_W)[:, None]
    kc = np.arange(GRID_W)[None, :]
    cs = np.clip(c - NA_WIN_W // 2, 0, GRID_W - NA_WIN_W)
    in_window = (kc >= cs) & (kc < cs + NA_WIN_W)
    out = jnp.full((N_HEADS, NA_TABLE, GRID_W, 2, GRID_W), NEG_INF, F32)
    for dc in range(-(NA_WIN_W - 1), NA_WIN_W):
        hit = ((kc - c) == dc) & in_window
        if hit.any():
            out = jnp.where(hit[None, None, :, None, :], pair[:, :, None, :, dc + NA_WIN_W - 1, None], out)
    return out.reshape(N_HEADS, NA_TABLE, GRID_W, LANES)


def _router_tables(w_group, b_group, w_router, b_router):
    d = w_group.shape[0]
    w = jnp.zeros((d, ROUTER_COLS), F32)
    w = w.at[:, 0:N_GROUPS].set(w_group).at[:, 8:8 + N_EXPERTS].set(w_router)
    bias = jnp.zeros((1, ROUTER_COLS), F32)
    bias = bias.at[0, 0:N_GROUPS].set(b_group).at[0, 8:8 + N_EXPERTS].set(b_router)
    hi = w.astype(BF16)
    lo = (w - hi.astype(F32)).astype(BF16)
    return hi, lo, bias


def _pick(table, onehot):
    return jnp.sum(jnp.where(onehot, table[None, :], 0), axis=1)


def _moe_layout(cnt_tiles, n_rows):
    cnt = cnt_tiles[:, :, 0]
    counts = jnp.sum(cnt, axis=0)
    ends = jnp.cumsum(counts)
    starts = ends - counts
    base = starts[None, :] + jnp.cumsum(cnt, axis=0) - cnt
    first_blk = starts // EXPERT_BLOCK
    n_blk = jnp.where(counts > 0, (ends - 1) // EXPERT_BLOCK - first_blk + 1, 0)
    v_end = jnp.cumsum(n_blk)
    v_off = v_end - n_blk
    n_visits = n_rows // EXPERT_BLOCK + N_EXPERTS
    v = jnp.arange(n_visits, dtype=I32)
    valid = v < v_end[-1]
    vv = jnp.minimum(v, v_end[-1] - 1)
    ve = jnp.sum((v_end[None, :] <= vv[:, None]).astype(I32), axis=1)
    onehot = ve[:, None] == jnp.arange(N_EXPERTS, dtype=I32)[None, :]
    vblk = _pick(first_blk, onehot) + vv - _pick(v_off, onehot)
    lo = jnp.clip(_pick(starts, onehot) - vblk * EXPERT_BLOCK, 0, EXPERT_BLOCK)
    hi = jnp.clip(_pick(ends, onehot) - vblk * EXPERT_BLOCK, 0, EXPERT_BLOCK)
    hi = jnp.where(valid, hi, lo)
    visits = tuple(a.astype(I32) for a in (ve, vblk, lo, hi))
    return base[:, :, None].astype(I32), visits


def kernel(x, c, ctx, c_ctx, w_ada, b_ada, g_attn, w_qkv, w_o, sink_a, gq_b, gk_b, rpb_c, g_ffn,
           w_group, b_group, w_router, b_router, w_gate, w_up, w_down, g_final):
    b, n, d = x.shape
    n_ctx = ctx.shape[1]
    depth = w_ada.shape[0]
    tm_tok = 512
    tm_ctx = 256
    assert b + 1 <= MOD_ROWS and d == D_MODEL and n % tm_tok == 0 and n_ctx % tm_ctx == 0
    assert (b * n_ctx) % tm_tok == 0
    ctx_row = b

    cc = jnp.zeros((MOD_ROWS, d), F32).at[:b].set(c).at[b].set(c_ctx)
    modr = _ada_all(cc, w_ada, b_ada).reshape(depth * MOD_ROWS * N_MOD, 1, d)
    g_attn_rows = g_attn.reshape(depth, 1, d)
    g_ffn_rows = g_ffn.reshape(depth, 1, d)
    rope_tabs = _rope_tables(n)
    head_sum = jnp.asarray(np.kron(np.eye(LANES // HEAD_DIM), np.ones((HEAD_DIM, HEAD_DIM))), BF16)
    tri = jnp.asarray(np.triu(np.ones((tm_tok, tm_tok))), BF16)

    xc = ctx
    for i in range(depth):
        m = i % N_MIXERS
        j = i // N_MIXERS
        last = i == depth - 1
        wqkv_bf = w_qkv[i].astype(BF16)
        wo_bf = w_o[i].astype(BF16)
        qk_gains = None
        if m == 1:
            qk_gains = (jnp.tile(gq_b[j], LANES // HEAD_DIM).reshape(1, LANES),
                        jnp.tile(gk_b[j], LANES // HEAD_DIM).reshape(1, LANES), head_sum)

        if m == 2:
            qp, k, v = _qkv(x, modr, g_attn_rows, wqkv_bf, i, None, tm_tok, None, qk_gains)
            qr = None
        else:
            qp, qr, k, v = _qkv(x, modr, g_attn_rows, wqkv_bf, i, None, tm_tok, rope_tabs, qk_gains)
        qc, kc, vc = _qkv(xc, modr, g_attn_rows, wqkv_bf, i, ctx_row, tm_ctx, None, qk_gains)
        sink = sink_a[j] if m == 0 else None
        extra = sink if m == 0 else (_na_bias_tables(rpb_c[j]) if m == 2 else None)
        o = _attention(m, qp, qr, k, v, kc, vc, extra)

        router = _router_tables(w_group[i], b_group[i], w_router[i], b_router[i])
        streams = [(o, x, None)]
        if not last:
            oc = _ctx_attention(qc, kc, vc, sink)
            streams.append((oc.reshape(1, b * n_ctx, d), xc.reshape(1, b * n_ctx, d), ctx_row))
        routed = [_post_attn(o_s, x_s, wo_bf, modr, g_ffn_rows, router, i, row, tm_tok) for (o_s, x_s, row) in streams]

        n_rows = 2 * sum(r[2].shape[1] for r in routed)
        assert n_rows % EXPERT_BLOCK == 0
        base, visits = _moe_layout(jnp.concatenate([r[4] for r in routed], axis=0), n_rows)
        dest = _dest_rows(jnp.concatenate([r[2] for r in routed], axis=1), base, tri)
        xs = _scatter_rows(dest, [r[1].reshape(-1, d) for r in routed], n_rows)
        y = _experts(visits, xs, w_gate, w_up, w_down, i)
        nt_lat = routed[0][4].shape[0]
        x = _combine(dest[:nt_lat], routed[0][0], routed[0][3], modr, y, i, None, g_final if last else None)
        if not last:
            xc = _combine(dest[nt_lat:], routed[1][0], routed[1][3], modr, y, i, ctx_row).reshape(b, n_ctx, d)
    return x
```

```python
import functools

import numpy as np
import jax
import jax.numpy as jnp
from jax import lax
from jax.experimental import pallas as pl
from jax.experimental.pallas import tpu as pltpu

F32 = jnp.float32
BF16 = jnp.bfloat16
I32 = jnp.int32

D_MODEL = 1024
GRID_W = 64
N_MIXERS = 3
N_HEADS = 16
N_KV_HEADS = 4
HEAD_DIM = D_MODEL // N_HEADS
Q_PER_KV = N_HEADS // N_KV_HEADS
QKV_DIM = (N_HEADS + 2 * N_KV_HEADS) * HEAD_DIM
ATTN_SCALE = HEAD_DIM ** -0.5
ROPE_THETA = 10000.0
BLOCK_Q = 128
WINDOW = 128
NA_WIN_H = 8
NA_WIN_W = 16
N_GROUPS = 4
EXPERTS_PER_GROUP = 8
N_EXPERTS = N_GROUPS * EXPERTS_PER_GROUP
EXPERT_HIDDEN = D_MODEL // 2
NORM_EPS = 1e-6
NEG_INF = -1e30
LOG2E = 1.4426950408889634

LANES = 128
MOD_ROWS = 24
N_MOD = 6
EXPERT_BLOCK = 512
ROW_UNROLL = 8
ROUTER_COLS = LANES
NA_KEY_ROWS = 10
NA_TABLE = 18
VMEM_LIMIT = 56 * 1024 * 1024


def _cparams(sem, vmem=VMEM_LIMIT):
    return pltpu.CompilerParams(dimension_semantics=sem, vmem_limit_bytes=vmem)


def _dot(a, b):
    return jnp.dot(a, b, preferred_element_type=F32)


def _dot_nt(a, b):
    return lax.dot_general(a, b, (((1,), (1,)), ((), ())), preferred_element_type=F32)


def _split_bf16(x):
    hi = x.astype(BF16)
    lo = (x - hi.astype(F32)).astype(BF16)
    return hi, lo


def _rms_mod(x, g, sh, sc):
    ms = jnp.mean(x * x, axis=-1, keepdims=True)
    y = x * lax.rsqrt(ms + NORM_EPS) * g
    return y * (1.0 + sc) + sh


def _ada_kernel(c_ref, w_ref, b_ref, o_ref):
    c = c_ref[...]
    s = c * jax.nn.sigmoid(c)
    o_ref[0] = _dot(s.astype(BF16), w_ref[0].astype(BF16)) + b_ref[0]


def _ada_all(cc, w_ada, b_ada):
    depth, d, d6 = w_ada.shape
    tn = 1536
    return pl.pallas_call(
        _ada_kernel,
        grid=(depth, d6 // tn),
        in_specs=[
            pl.BlockSpec((MOD_ROWS, d), lambda i, j: (0, 0)),
            pl.BlockSpec((1, d, tn), lambda i, j: (i, 0, j)),
            pl.BlockSpec((1, 1, tn), lambda i, j: (i, 0, j)),
        ],
        out_specs=pl.BlockSpec((1, MOD_ROWS, tn), lambda i, j: (i, 0, j)),
        out_shape=jax.ShapeDtypeStruct((depth, MOD_ROWS, d6), F32),
        compiler_params=_cparams(("parallel", "parallel")),
        name="ada_mod",
    )(cc, w_ada, b_ada.reshape(depth, 1, d6))


def _mod_spec(layer, chunk, ctx_row):
    if ctx_row is None:
        return pl.BlockSpec((1, 1, D_MODEL), lambda b, t: ((layer * MOD_ROWS + b) * N_MOD + chunk, 0, 0))
    return pl.BlockSpec((1, 1, D_MODEL), lambda b, t: ((layer * MOD_ROWS + ctx_row) * N_MOD + chunk, 0, 0))


def _layer_row_spec(layer):
    return pl.BlockSpec((1, 1, D_MODEL), lambda b, t: (layer, 0, 0))


def _qkv_kernel(*refs, qk_norm, rope):
    x_ref, g_ref, sh_ref, sc_ref, w_ref = refs[:5]
    pos = 5
    if rope:
        cos_ref, sin_ref = refs[pos:pos + 2]
        pos += 2
    if qk_norm:
        gq_ref, gk_ref, bm_ref = refs[pos:pos + 3]
        pos += 3
    if rope:
        qp_ref, qr_ref, k_ref, v_ref = refs[pos:]
    else:
        qp_ref, k_ref, v_ref = refs[pos:]

    h = _rms_mod(x_ref[0], g_ref[0], sh_ref[0], sc_ref[0])
    res = _dot(h.astype(BF16), w_ref[...])
    half = LANES // 2
    n_q = N_HEADS * HEAD_DIM // LANES
    n_k = N_KV_HEADS * HEAD_DIM // LANES
    if rope:
        lane = lax.broadcasted_iota(I32, (1, LANES), 1)
        first_half = (lane & (HEAD_DIM // 2 - 1)) < (HEAD_DIM // 4)
    for cb in range(QKV_DIM // LANES):
        xs = res[:, cb * LANES:(cb + 1) * LANES]
        is_q = cb < n_q
        is_k = n_q <= cb < n_q + n_k
        if qk_norm and (is_q or is_k):
            hi, lo = _split_bf16(xs * xs)
            ms = (_dot(hi, bm_ref[...]) + _dot(lo, bm_ref[...])) * (1.0 / HEAD_DIM)
            gain = gq_ref[...] if is_q else gk_ref[...]
            xs = xs * lax.rsqrt(ms + NORM_EPS) * gain
        if is_q:
            xs = xs * (ATTN_SCALE * LOG2E)
        if rope and (is_q or is_k):
            rot = jnp.where(first_half, pltpu.roll(xs, LANES - HEAD_DIM // 4, 1), pltpu.roll(xs, HEAD_DIM // 4, 1))
            xr = xs * cos_ref[...] + rot * sin_ref[...]
        if is_q:
            hd = 2 * cb
            qp_ref[0, hd] = xs[:, :half].astype(BF16)
            qp_ref[0, hd + 1] = xs[:, half:].astype(BF16)
            if rope:
                qr_ref[0, hd] = xr[:, :half].astype(BF16)
                qr_ref[0, hd + 1] = xr[:, half:].astype(BF16)
        elif is_k:
            hd = 2 * (cb - n_q)
            kk = xr if rope else xs
            k_ref[0, hd] = kk[:, :half].astype(BF16)
            k_ref[0, hd + 1] = kk[:, half:].astype(BF16)
        else:
            hd = 2 * (cb - n_q - n_k)
            vlane = lax.broadcasted_iota(I32, (1, LANES), 1)
            ones_col = (vlane == HEAD_DIM).astype(F32)
            v_ref[0, hd] = jnp.where(vlane < HEAD_DIM, xs, ones_col).astype(BF16)
            v_ref[0, hd + 1] = jnp.where(vlane < HEAD_DIM, pltpu.roll(xs, half, 1), ones_col).astype(BF16)


def _qkv(x, modr, g_rows, w_bf, layer, ctx_row, tm, rope_tabs=None, qk_gains=None):
    b, n, d = x.shape
    rope = rope_tabs is not None
    qk_norm = qk_gains is not None
    args = [x, g_rows, modr, modr, w_bf]
    in_specs = [
        pl.BlockSpec((1, tm, d), lambda bb, t: (bb, t, 0)),
        _layer_row_spec(layer),
        _mod_spec(layer, 0, ctx_row),
        _mod_spec(layer, 1, ctx_row),
        pl.BlockSpec((d, QKV_DIM), lambda bb, t: (0, 0)),
    ]
    if rope:
        args += list(rope_tabs)
        in_specs += [pl.BlockSpec((tm, LANES), lambda bb, t: (t, 0))] * 2
    if qk_norm:
        args += list(qk_gains)
        in_specs += [pl.BlockSpec((1, LANES), lambda bb, t: (0, 0))] * 2
        in_specs += [pl.BlockSpec((LANES, LANES), lambda bb, t: (0, 0))]
    q_shape = jax.ShapeDtypeStruct((b, N_HEADS, n, HEAD_DIM), BF16)
    kv_shape = jax.ShapeDtypeStruct((b, N_KV_HEADS, n, HEAD_DIM), BF16)
    q_spec = pl.BlockSpec((1, N_HEADS, tm, HEAD_DIM), lambda bb, t: (bb, 0, t, 0))
    kv_spec = pl.BlockSpec((1, N_KV_HEADS, tm, HEAD_DIM), lambda bb, t: (bb, 0, t, 0))
    v_shape = jax.ShapeDtypeStruct((b, N_KV_HEADS, n, LANES), BF16)
    v_spec = pl.BlockSpec((1, N_KV_HEADS, tm, LANES), lambda bb, t: (bb, 0, t, 0))
    if rope:
        out_shape, out_specs = [q_shape, q_shape, kv_shape, v_shape], [q_spec, q_spec, kv_spec, v_spec]
    else:
        out_shape, out_specs = [q_shape, kv_shape, v_shape], [q_spec, kv_spec, v_spec]
    return pl.pallas_call(
        functools.partial(_qkv_kernel, qk_norm=qk_norm, rope=rope),
        grid=(b, n // tm),
        in_specs=in_specs,
        out_specs=out_specs,
        out_shape=out_shape,
        compiler_params=_cparams(("parallel", "parallel")),
        name="qkv_rope" if rope else "qkv",
    )(*args)


def _attend(s, v_aug):
    m = jnp.max(s, axis=-1, keepdims=True)
    p = jnp.exp2(s - m)
    return m, _dot(p.astype(BF16), v_aug)


def _normalise(acc):
    return acc[:, :HEAD_DIM] / acc[:, HEAD_DIM:HEAD_DIM + 1]


def _normalise_with_sink(m, acc, sink_ref, kv_head, rows_per_head):
    pieces = []
    for g in range(Q_PER_KV):
        sink = sink_ref[kv_head * Q_PER_KV + g] * LOG2E
        mg = m[g * rows_per_head:(g + 1) * rows_per_head]
        ag = acc[g * rows_per_head:(g + 1) * rows_per_head]
        m2 = jnp.maximum(mg, sink)
        scale = jnp.exp2(mg - m2)
        denom = ag[:, HEAD_DIM:HEAD_DIM + 1] * scale + jnp.exp2(sink - m2)
        pieces.append(ag[:, :HEAD_DIM] * scale / denom)
    return pieces


def _split_heads(o, rows_per_head):
    return [o[g * rows_per_head:(g + 1) * rows_per_head] for g in range(Q_PER_KV)]


def _attn_kernel(*refs, mode, key_chunk):
    if mode == 0:
        qp_ref, qr_ref, k_ref, v_ref, kc_ref, vc_ref, sink_ref, o_ref = refs
    elif mode == 1:
        qp_ref, qr_ref, k_ref, v_ref, kc_ref, vc_ref, o_ref = refs
    else:
        qp_ref, k_ref, v_ref, kc_ref, vc_ref, t2_ref, o_ref = refs
    qb = pl.program_id(1)
    n = k_ref.shape[2]
    rows = Q_PER_KV * BLOCK_Q

    if mode == 0:
        span = BLOCK_Q + 2 * WINDOW
        kstart = pl.multiple_of(jnp.clip(qb * BLOCK_Q - WINDOW, 0, n - span), BLOCK_Q)
        qpos = qb * BLOCK_Q + (lax.broadcasted_iota(I32, (rows, span), 0) & (BLOCK_Q - 1))
        kpos = kstart + lax.broadcasted_iota(I32, (rows, span), 1)
        band = jnp.abs(kpos - qpos) <= WINDOW
    elif mode == 2:
        rows_grid = n // GRID_W
        r0 = qb * (BLOCK_Q // GRID_W)
        wr = jnp.clip(r0 - NA_WIN_H // 2, 0, rows_grid - NA_KEY_ROWS)
        kstart = pl.multiple_of(wr * GRID_W, GRID_W)
        lane = lax.broadcasted_iota(I32, (GRID_W, LANES), 1)

    pieces = []
    for h in range(N_KV_HEADS):
        qp = qp_ref[0, h * Q_PER_KV:(h + 1) * Q_PER_KV].reshape(rows, HEAD_DIM)
        s_ctx = _dot_nt(qp, kc_ref[0, h])
        if mode == 0:
            qr = qr_ref[0, h * Q_PER_KV:(h + 1) * Q_PER_KV].reshape(rows, HEAD_DIM)
            s = jnp.where(band, _dot_nt(qr, k_ref[0, h, pl.ds(kstart, span), :]), NEG_INF)
            v = jnp.concatenate([v_ref[0, h, pl.ds(kstart, span), :], vc_ref[0, h]], axis=0)
            m, acc = _attend(jnp.concatenate([s, s_ctx], axis=1), v)
            pieces += _normalise_with_sink(m, acc, sink_ref, h, BLOCK_Q)
        elif mode == 1:
            qr = qr_ref[0, h * Q_PER_KV:(h + 1) * Q_PER_KV].reshape(rows, HEAD_DIM)
            n_chunks = n // key_chunk
            m, acc = _attend(_dot_nt(qr, k_ref[0, h, 0:key_chunk, :]), v_ref[0, h, 0:key_chunk, :])
            for c in range(1, n_chunks):
                s = _dot_nt(qr, k_ref[0, h, c * key_chunk:(c + 1) * key_chunk, :])
                v = v_ref[0, h, c * key_chunk:(c + 1) * key_chunk, :]
                if c == n_chunks - 1:
                    s = jnp.concatenate([s, s_ctx], axis=1)
                    v = jnp.concatenate([v, vc_ref[0, h]], axis=0)
                m_new = jnp.maximum(m, jnp.max(s, axis=-1, keepdims=True))
                p = jnp.exp2(s - m_new)
                acc = jnp.exp2(m - m_new) * acc + _dot(p.astype(BF16), v)
                m = m_new
            pieces += _split_heads(_normalise(acc), BLOCK_Q)
        else:
            n_keys = NA_KEY_ROWS * GRID_W
            s = _dot_nt(qp, k_ref[0, h, pl.ds(kstart, n_keys), :])
            blocks = []
            for g in range(Q_PER_KV):
                for a in range(BLOCK_Q // GRID_W):
                    r = r0 + a
                    rs = jnp.clip(r - NA_WIN_H // 2, 0, rows_grid - NA_WIN_H)
                    tiles = []
                    for j in range(n_keys // LANES):
                        kr0 = wr + 2 * j
                        tab = t2_ref[h * Q_PER_KV + g, kr0 - r + NA_TABLE // 2]
                        pen0 = jnp.where((kr0 >= rs) & (kr0 < rs + NA_WIN_H), 0.0, NEG_INF)
                        pen1 = jnp.where((kr0 + 1 >= rs) & (kr0 + 1 < rs + NA_WIN_H), 0.0, NEG_INF)
                        tiles.append(tab + jnp.where(lane < GRID_W, pen0, pen1))
                    blocks.append(jnp.concatenate(tiles, axis=1))
            s = s + jnp.concatenate(blocks, axis=0)
            v = jnp.concatenate([v_ref[0, h, pl.ds(kstart, n_keys), :], vc_ref[0, h]], axis=0)
            _, acc = _attend(jnp.concatenate([s, s_ctx], axis=1), v)
            pieces += _split_heads(_normalise(acc), BLOCK_Q)
    o_ref[0] = jnp.concatenate(pieces, axis=-1).astype(o_ref.dtype)


def _attention(mode, qp, qr, k, v, kc, vc, extra):
    b, _, n, _ = qp.shape
    c = kc.shape[2]
    q_spec = pl.BlockSpec((1, N_HEADS, BLOCK_Q, HEAD_DIM), lambda bb, i: (bb, 0, i, 0))
    k_spec = pl.BlockSpec((1, N_KV_HEADS, n, HEAD_DIM), lambda bb, i: (bb, 0, 0, 0))
    v_spec = pl.BlockSpec((1, N_KV_HEADS, n, LANES), lambda bb, i: (bb, 0, 0, 0))
    kc_spec = pl.BlockSpec((1, N_KV_HEADS, c, HEAD_DIM), lambda bb, i: (bb, 0, 0, 0))
    vc_spec = pl.BlockSpec((1, N_KV_HEADS, c, LANES), lambda bb, i: (bb, 0, 0, 0))
    if mode == 2:
        args = [qp, k, v, kc, vc, extra]
        in_specs = [q_spec, k_spec, v_spec, kc_spec, vc_spec,
                    pl.BlockSpec(extra.shape, lambda bb, i: (0, 0, 0, 0))]
    else:
        args = [qp, qr, k, v, kc, vc]
        in_specs = [q_spec, q_spec, k_spec, v_spec, kc_spec, vc_spec]
        if mode == 0:
            args.append(extra)
            in_specs.append(pl.BlockSpec(memory_space=pltpu.SMEM))
    return pl.pallas_call(
        functools.partial(_attn_kernel, mode=mode, key_chunk=512),
        grid=(b, n // BLOCK_Q),
        in_specs=in_specs,
        out_specs=pl.BlockSpec((1, BLOCK_Q, D_MODEL), lambda bb, i: (bb, i, 0)),
        out_shape=jax.ShapeDtypeStruct((b, n, D_MODEL), BF16),
        compiler_params=_cparams(("parallel", "arbitrary")),
        name=("attn_window", "attn_global", "attn_na")[mode],
    )(*args)


def _ctx_attn_kernel(*refs, has_sink):
    if has_sink:
        q_ref, k_ref, v_ref, sink_ref, o_ref = refs
    else:
        q_ref, k_ref, v_ref, o_ref = refs
    c = k_ref.shape[2]
    rows = Q_PER_KV * c
    pieces = []
    for h in range(N_KV_HEADS):
        q = q_ref[0, h * Q_PER_KV:(h + 1) * Q_PER_KV].reshape(rows, HEAD_DIM)
        m, acc = _attend(_dot_nt(q, k_ref[0, h]), v_ref[0, h])
        if has_sink:
            pieces += _normalise_with_sink(m, acc, sink_ref, h, c)
        else:
            pieces += _split_heads(_normalise(acc), c)
    o_ref[0] = jnp.concatenate(pieces, axis=-1).astype(o_ref.dtype)


def _ctx_attention(qc, kc, vc, sink):
    b, _, c, _ = qc.shape
    args = [qc, kc, vc]
    in_specs = [pl.BlockSpec((1, N_HEADS, c, HEAD_DIM), lambda bb: (bb, 0, 0, 0)),
                pl.BlockSpec((1, N_KV_HEADS, c, HEAD_DIM), lambda bb: (bb, 0, 0, 0)),
                pl.BlockSpec((1, N_KV_HEADS, c, LANES), lambda bb: (bb, 0, 0, 0))]
    if sink is not None:
        args.append(sink)
        in_specs.append(pl.BlockSpec(memory_space=pltpu.SMEM))
    return pl.pallas_call(
        functools.partial(_ctx_attn_kernel, has_sink=sink is not None),
        grid=(b,),
        in_specs=in_specs,
        out_specs=pl.BlockSpec((1, c, D_MODEL), lambda bb: (bb, 0, 0)),
        out_shape=jax.ShapeDtypeStruct((b, c, D_MODEL), BF16),
        compiler_params=_cparams(("parallel",)),
        name="attn_ctx",
    )(*args)


def _post_attn_kernel(o_ref, x_ref, wo_ref, gt_ref, g_ref, sh_ref, sc_ref, wrh_ref, wrl_ref, br_ref,
                      xn_ref, hf_ref, eid_ref, wcol_ref, cnt_ref):
    xn = x_ref[0] + gt_ref[0] * _dot(o_ref[0], wo_ref[...])
    xn_ref[0] = xn
    hf = _rms_mod(xn, g_ref[0], sh_ref[0], sc_ref[0])
    hf_ref[0] = hf
    tm = hf.shape[0]

    hh, hl = _split_bf16(hf)
    lg = _dot(hh, wrh_ref[...]) + _dot(hl, wrh_ref[...]) + _dot(hh, wrl_ref[...]) + br_ref[...]
    lgt = lg.T

    gl = lgt[0:N_GROUPS]
    ge = jnp.exp(gl - jnp.max(gl, axis=0, keepdims=True))
    pg = ge / jnp.sum(ge, axis=0, keepdims=True)
    p_top = jnp.max(pg, axis=0, keepdims=True)
    grow = lax.broadcasted_iota(I32, (N_GROUPS, tm), 0)
    grp = jnp.min(jnp.where(pg == p_top, grow, N_GROUPS), axis=0, keepdims=True)

    el = lgt[8:8 + EXPERTS_PER_GROUP]
    for gi in range(1, N_GROUPS):
        el = jnp.where(grp == gi, lgt[8 + gi * EXPERTS_PER_GROUP:8 + (gi + 1) * EXPERTS_PER_GROUP], el)
    ee = jnp.exp(el - jnp.max(el, axis=0, keepdims=True))
    pe = ee / jnp.sum(ee, axis=0, keepdims=True)
    erow = lax.broadcasted_iota(I32, (EXPERTS_PER_GROUP, tm), 0)
    p1 = jnp.max(pe, axis=0, keepdims=True)
    i1 = jnp.min(jnp.where(pe == p1, erow, EXPERTS_PER_GROUP), axis=0, keepdims=True)
    pe2 = jnp.where(erow == i1, -1.0, pe)
    p2 = jnp.max(pe2, axis=0, keepdims=True)
    i2 = jnp.min(jnp.where(pe2 == p2, erow, EXPERTS_PER_GROUP), axis=0, keepdims=True)
    den = p1 + p2
    w1 = p_top * p1 / den
    w2 = p_top * p2 / den
    e1 = grp * EXPERTS_PER_GROUP + i1
    e2 = grp * EXPERTS_PER_GROUP + i2

    eid_ref[0:1, :] = e1
    eid_ref[1:2, :] = e2
    wrows = jnp.concatenate([w1, w2, jnp.zeros((ROUTER_COLS - 2, tm), F32)], axis=0)
    wcol_ref[...] = wrows.T
    xrow = lax.broadcasted_iota(I32, (N_EXPERTS, tm), 0)
    hits = (xrow == e1).astype(F32) + (xrow == e2).astype(F32)
    cnt_ref[0] = jnp.sum(hits, axis=1, keepdims=True).astype(I32)


def _post_attn(o, x, wo_bf, modr, g_rows, router, layer, ctx_row, tm):
    b, n, d = x.shape
    nt = n // tm
    t_all = b * n
    wrh, wrl, br = router
    tile = lambda bb, t: (bb, t, 0)
    const2 = lambda bb, t: (0, 0)
    return pl.pallas_call(
        _post_attn_kernel,
        grid=(b, nt),
        in_specs=[
            pl.BlockSpec((1, tm, d), tile),
            pl.BlockSpec((1, tm, d), tile),
            pl.BlockSpec((d, d), const2),
            _mod_spec(layer, 2, ctx_row),
            _layer_row_spec(layer),
            _mod_spec(layer, 3, ctx_row),
            _mod_spec(layer, 4, ctx_row),
            pl.BlockSpec((d, ROUTER_COLS), const2),
            pl.BlockSpec((d, ROUTER_COLS), const2),
            pl.BlockSpec((1, ROUTER_COLS), const2),
        ],
        out_specs=[
            pl.BlockSpec((1, tm, d), tile),
            pl.BlockSpec((1, tm, d), tile),
            pl.BlockSpec((2, tm), lambda bb, t: (0, bb * nt + t)),
            pl.BlockSpec((tm, ROUTER_COLS), lambda bb, t: (bb * nt + t, 0)),
            pl.BlockSpec((1, N_EXPERTS, 1), lambda bb, t: (bb * nt + t, 0, 0)),
        ],
        out_shape=[
            jax.ShapeDtypeStruct((b, n, d), F32),
            jax.ShapeDtypeStruct((b, n, d), F32),
            jax.ShapeDtypeStruct((2, t_all), I32),
            jax.ShapeDtypeStruct((t_all, ROUTER_COLS), F32),
            jax.ShapeDtypeStruct((b * nt, N_EXPERTS, 1), I32),
        ],
        compiler_params=_cparams(("parallel", "parallel")),
        name="post_attn_router",
    )(o, x, wo_bf, modr, g_rows, modr, modr, wrh, wrl, br)


def _dest_kernel(eid_ref, base_ref, tri_ref, dest_ref):
    tm = eid_ref.shape[1]
    xrow = lax.broadcasted_iota(I32, (N_EXPERTS, tm), 0)
    oh0 = xrow == eid_ref[0:1, :]
    oh1 = xrow == eid_ref[1:2, :]
    both = oh0.astype(F32) + oh1.astype(F32)
    incl = _dot(both.astype(BF16), tri_ref[...])
    before = incl - both + base_ref[0].astype(F32)
    d0 = jnp.sum(jnp.where(oh0, before, 0.0), axis=0, keepdims=True)
    d1 = jnp.sum(jnp.where(oh1, before, 0.0), axis=0, keepdims=True)
    dest_ref[0, 0:1, :] = d0.astype(I32)
    dest_ref[0, 1:2, :] = d1.astype(I32)


def _dest_rows(eid, base, tri):
    tm = tri.shape[0]
    nt = eid.shape[1] // tm
    return pl.pallas_call(
        _dest_kernel,
        grid=(nt,),
        in_specs=[
            pl.BlockSpec((2, tm), lambda i: (0, i)),
            pl.BlockSpec((1, N_EXPERTS, 1), lambda i: (i, 0, 0)),
            pl.BlockSpec((tm, tm), lambda i: (0, 0)),
        ],
        out_specs=pl.BlockSpec((1, 2, tm), lambda i: (i, 0, 0)),
        out_shape=jax.ShapeDtypeStruct((nt, 2, tm), I32),
        compiler_params=_cparams(("parallel",)),
        name="moe_dest",
    )(eid, base, tri)


def _scatter_kernel(*refs, tiles):
    dest_ref = refs[0]
    hf_refs = refs[1:1 + len(tiles)]
    xs_ref, sem = refs[1 + len(tiles):]
    tm = hf_refs[0].shape[0]
    i = pl.program_id(0)

    def scatter_tile(hf_ref):
        def issue(t8, carry):
            base = pl.multiple_of(t8 * ROW_UNROLL, ROW_UNROLL)
            rows = hf_ref.at[pl.ds(base, ROW_UNROLL)]
            for u in range(ROW_UNROLL):
                for k in range(2):
                    d = dest_ref.at[0, k, pl.ds(base, ROW_UNROLL)][u]
                    pltpu.make_async_copy(rows.at[pl.ds(u, 1)], xs_ref.at[pl.ds(d, 1)], sem).start(priority=k)
            return carry

        lax.fori_loop(0, tm // ROW_UNROLL, issue, 0)
        for k in range(2):
            pltpu.make_async_copy(hf_ref, xs_ref.at[pl.ds(0, tm)], sem).wait()

    first = 0
    for hf_ref, nt in zip(hf_refs, tiles):
        pl.when((i >= first) & (i < first + nt))(functools.partial(scatter_tile, hf_ref))
        first += nt


def _scatter_rows(dest, hf_streams, n_rows):
    nt_all, _, tm = dest.shape
    d = hf_streams[0].shape[1]
    tiles = tuple(h.shape[0] // tm for h in hf_streams)
    assert sum(tiles) == nt_all
    in_specs = [pl.BlockSpec((1, 2, tm), lambda i: (i, 0, 0), memory_space=pltpu.SMEM)]
    first = 0
    for nt in tiles:
        in_specs.append(pl.BlockSpec((tm, d), lambda i, first=first, nt=nt: (jnp.clip(i - first, 0, nt - 1), 0)))
        first += nt
    return pl.pallas_call(
        functools.partial(_scatter_kernel, tiles=tiles),
        grid=(nt_all,),
        in_specs=in_specs,
        out_specs=pl.BlockSpec(memory_space=pl.ANY),
        out_shape=jax.ShapeDtypeStruct((n_rows, d), F32),
        scratch_shapes=[pltpu.SemaphoreType.DMA(())],
        compiler_params=_cparams(("arbitrary",)),
        name="moe_scatter",
    )(dest, *hf_streams)


def _expert_kernel(ve_ref, vblk_ref, lo_ref, hi_ref, x_ref, wg_ref, wu_ref, wd_ref, y_ref, wg_s, wu_s, wd_s):
    v = pl.program_id(0)
    pv = jnp.maximum(v - 1, 0)
    new_expert = (v == 0) | (ve_ref[v] != ve_ref[pv])
    first_visit = (v == 0) | (vblk_ref[v] != vblk_ref[pv])
    lo = lo_ref[v]
    hi = hi_ref[v]

    @pl.when(new_expert)
    def _():
        wg_s[...] = wg_ref[0, 0].astype(BF16)
        wu_s[...] = wu_ref[0, 0].astype(BF16)
        wd_s[...] = wd_ref[0, 0].astype(BF16)

    @pl.when(hi > lo)
    def _():
        x = x_ref[...].astype(BF16)
        gate = _dot(x, wg_s[...])
        up = _dot(x, wu_s[...])
        mid = gate * jax.nn.sigmoid(gate) * up
        y = _dot(mid.astype(BF16), wd_s[...])
        row = lax.broadcasted_iota(I32, (EXPERT_BLOCK, 1), 0)
        mine = (row >= lo) & (row < hi)

        @pl.when(first_visit)
        def _():
            y_ref[...] = jnp.where(mine, y, 0.0)

        @pl.when(jnp.logical_not(first_visit))
        def _():
            y_ref[...] = jnp.where(mine, y, y_ref[...])


def _experts(visits, xs, w_gate, w_up, w_down, layer):
    n_rows, d = xs.shape
    f = w_gate.shape[3]
    ve, vblk, lo, hi = visits
    x_map = lambda v, ve_r, vb_r, lo_r, hi_r: (vb_r[v], 0)
    w_map = lambda v, ve_r, vb_r, lo_r, hi_r: (layer, ve_r[v], 0, 0)
    grid_spec = pltpu.PrefetchScalarGridSpec(
        num_scalar_prefetch=4,
        grid=(ve.shape[0],),
        in_specs=[
            pl.BlockSpec((EXPERT_BLOCK, d), x_map),
            pl.BlockSpec((1, 1, d, f), w_map),
            pl.BlockSpec((1, 1, d, f), w_map),
            pl.BlockSpec((1, 1, f, d), w_map),
        ],
        out_specs=pl.BlockSpec((EXPERT_BLOCK, d), x_map),
        scratch_shapes=[pltpu.VMEM((d, f), BF16), pltpu.VMEM((d, f), BF16), pltpu.VMEM((f, d), BF16)],
    )
    return pl.pallas_call(
        _expert_kernel,
        grid_spec=grid_spec,
        out_shape=jax.ShapeDtypeStruct((n_rows, d), F32),
        compiler_params=_cparams(("arbitrary",)),
        name="moe_experts",
    )(ve, vblk, lo, hi, xs, w_gate, w_up, w_down)


def _combine_kernel(*refs, final_norm):
    if final_norm:
        dest_ref, dest_next_ref, xn_ref, wcol_ref, gt_ref, gfin_ref, y_ref, o_ref, ybuf, sems = refs
    else:
        dest_ref, dest_next_ref, xn_ref, wcol_ref, gt_ref, y_ref, o_ref, ybuf, sems = refs
    tm = xn_ref.shape[1]
    step = pl.program_id(0) * pl.num_programs(1) + pl.program_id(1)
    n_steps = pl.num_programs(0) * pl.num_programs(1)

    def gather_tile(d_ref, slot):
        def issue(t8, carry):
            base = pl.multiple_of(t8 * ROW_UNROLL, ROW_UNROLL)
            for k in range(2):
                rows = ybuf.at[slot, k, pl.ds(base, ROW_UNROLL)]
                for u in range(ROW_UNROLL):
                    d = d_ref.at[0, k, pl.ds(base, ROW_UNROLL)][u]
                    pltpu.make_async_copy(y_ref.at[pl.ds(d, 1)], rows.at[pl.ds(u, 1)], sems.at[slot]).start(priority=k)
            return carry

        lax.fori_loop(0, tm // ROW_UNROLL, issue, 0)

    def finish_tile(slot):
        for k in range(2):
            pltpu.make_async_copy(y_ref.at[pl.ds(0, tm)], ybuf.at[slot, k], sems.at[slot]).wait()
        w = wcol_ref[...]
        moe = w[:, 0:1] * ybuf[slot, 0] + w[:, 1:2] * ybuf[slot, 1]
        out = xn_ref[0] + gt_ref[0] * moe
        if final_norm:
            ms = jnp.mean(out * out, axis=-1, keepdims=True)
            out = out * lax.rsqrt(ms + NORM_EPS) * gfin_ref[...]
        o_ref[0] = out

    pl.when(step == 0)(functools.partial(gather_tile, dest_ref, 0))
    for slot in range(2):
        @pl.when((step % 2 == slot) & (step + 1 < n_steps))
        def _(slot=slot):
            gather_tile(dest_next_ref, 1 - slot)

        pl.when(step % 2 == slot)(functools.partial(finish_tile, slot))


def _combine(dest, xn, wcol, modr, y, layer, ctx_row, g_final=None):
    b, n, d = xn.shape
    nt_all, _, tm = dest.shape
    nt = n // tm
    final_norm = g_final is not None
    args = [dest, dest, xn, wcol, modr]
    in_specs = [
        pl.BlockSpec((1, 2, tm), lambda bb, t: (bb * nt + t, 0, 0), memory_space=pltpu.SMEM),
        pl.BlockSpec((1, 2, tm), lambda bb, t: (jnp.minimum(bb * nt + t + 1, nt_all - 1), 0, 0),
                     memory_space=pltpu.SMEM),
        pl.BlockSpec((1, tm, d), lambda bb, t: (bb, t, 0)),
        pl.BlockSpec((tm, ROUTER_COLS), lambda bb, t: (bb * nt + t, 0)),
        _mod_spec(layer, 5, ctx_row),
    ]
    if final_norm:
        args.append(g_final.reshape(1, d))
        in_specs.append(pl.BlockSpec((1, d), lambda bb, t: (0, 0)))
    args.append(y)
    in_specs.append(pl.BlockSpec(memory_space=pl.ANY))
    return pl.pallas_call(
        functools.partial(_combine_kernel, final_norm=final_norm),
        grid=(b, nt),
        in_specs=in_specs,
        out_specs=pl.BlockSpec((1, tm, d), lambda bb, t: (bb, t, 0)),
        out_shape=jax.ShapeDtypeStruct((b, n, d), F32),
        scratch_shapes=[pltpu.VMEM((2, 2, tm, d), F32), pltpu.SemaphoreType.DMA((2,))],
        compiler_params=_cparams(("arbitrary", "arbitrary")),
        name="moe_combine",
    )(*args)


def _rope_tables(n):
    t = np.arange(n)
    row = (t // GRID_W).astype(np.float32)
    col = (t % GRID_W).astype(np.float32)
    quarter = HEAD_DIM // 4
    inv = jnp.asarray(ROPE_THETA, F32) ** (-jnp.arange(quarter, dtype=F32) / quarter)
    ar = jnp.asarray(row)[:, None] * inv
    ac = jnp.asarray(col)[:, None] * inv
    ang = jnp.concatenate([ar, ar, ac, ac], axis=-1)
    ang = jnp.concatenate([ang, ang], axis=-1)
    sign = np.where((np.arange(LANES) % (HEAD_DIM // 2)) < quarter, -1.0, 1.0).astype(np.float32)
    return jnp.cos(ang), jnp.sin(ang) * sign


def _na_bias_tables(rpb):
    n_dr, n_dc = 2 * NA_WIN_H - 1, 2 * NA_WIN_W - 1
    blocked = jnp.full((N_HEADS, 2, n_dc), NEG_INF, F32)
    rows = jnp.concatenate([blocked, rpb.astype(F32) * LOG2E, blocked], axis=1)
    assert rows.shape[1] == n_dr + 4 == NA_TABLE + 1
    pair = jnp.stack([rows[:, 0:NA_TABLE], rows[:, 1:NA_TABLE + 1]], axis=2)
    c = np.arange(GRID_W)[:, None]
    kc = np.arange(GRID_W)[None, :]
    cs = np.clip(c - NA_WIN_W // 2, 0, GRID_W - NA_WIN_W)
    in_window = (kc >= cs) & (kc < cs + NA_WIN_W)
    out = jnp.full((N_HEADS, NA_TABLE, GRID_W, 2, GRID_W), NEG_INF, F32)
    for dc in range(-(NA_WIN_W - 1), NA_WIN_W):
        hit = ((kc - c) == dc) & in_window
        if hit.any():
            out = jnp.where(hit[None, None, :, None, :], pair[:, :, None, :, dc + NA_WIN_W - 1, None], out)
    return out.reshape(N_HEADS, NA_TABLE, GRID_W, LANES)


def _router_tables(w_group, b_group, w_router, b_router):
    d = w_group.shape[0]
    w = jnp.zeros((d, ROUTER_COLS), F32)
    w = w.at[:, 0:N_GROUPS].set(w_group).at[:, 8:8 + N_EXPERTS].set(w_router)
    bias = jnp.zeros((1, ROUTER_COLS), F32)
    bias = bias.at[0, 0:N_GROUPS].set(b_group).at[0, 8:8 + N_EXPERTS].set(b_router)
    hi = w.astype(BF16)
    lo = (w - hi.astype(F32)).astype(BF16)
    return hi, lo, bias


def _pick(table, onehot):
    return jnp.sum(jnp.where(onehot, table[None, :], 0), axis=1)


def _moe_layout(cnt_tiles, n_rows):
    cnt = cnt_tiles[:, :, 0]
    counts = jnp.sum(cnt, axis=0)
    ends = jnp.cumsum(counts)
    starts = ends - counts
    base = starts[None, :] + jnp.cumsum(cnt, axis=0) - cnt
    first_blk = starts // EXPERT_BLOCK
    n_blk = jnp.where(counts > 0, (ends - 1) // EXPERT_BLOCK - first_blk + 1, 0)
    v_end = jnp.cumsum(n_blk)
    v_off = v_end - n_blk
    n_visits = n_rows // EXPERT_BLOCK + N_EXPERTS
    v = jnp.arange(n_visits, dtype=I32)
    valid = v < v_end[-1]
    vv = jnp.minimum(v, v_end[-1] - 1)
    ve = jnp.sum((v_end[None, :] <= vv[:, None]).astype(I32), axis=1)
    onehot = ve[:, None] == jnp.arange(N_EXPERTS, dtype=I32)[None, :]
    vblk = _pick(first_blk, onehot) + vv - _pick(v_off, onehot)
    lo = jnp.clip(_pick(starts, onehot) - vblk * EXPERT_BLOCK, 0, EXPERT_BLOCK)
    hi = jnp.clip(_pick(ends, onehot) - vblk * EXPERT_BLOCK, 0, EXPERT_BLOCK)
    hi = jnp.where(valid, hi, lo)
    visits = tuple(a.astype(I32) for a in (ve, vblk, lo, hi))
    return base[:, :, None].astype(I32), visits


def kernel(x, c, ctx, c_ctx, w_ada, b_ada, g_attn, w_qkv, w_o, sink_a, gq_b, gk_b, rpb_c, g_ffn,
           w_group, b_group, w_router, b_router, w_gate, w_up, w_down, g_final):
    b, n, d = x.shape
    n_ctx = ctx.shape[1]
    depth = w_ada.shape[0]
    tm_tok = 512
    tm_ctx = 256
    assert b + 1 <= MOD_ROWS and d == D_MODEL and n % tm_tok == 0 and n_ctx % tm_ctx == 0
    assert (b * n_ctx) % tm_tok == 0
    ctx_row = b

    cc = jnp.zeros((MOD_ROWS, d), F32).at[:b].set(c).at[b].set(c_ctx)
    modr = _ada_all(cc, w_ada, b_ada).reshape(depth * MOD_ROWS * N_MOD, 1, d)
    g_attn_rows = g_attn.reshape(depth, 1, d)
    g_ffn_rows = g_ffn.reshape(depth, 1, d)
    rope_tabs = _rope_tables(n)
    head_sum = jnp.asarray(np.kron(np.eye(LANES // HEAD_DIM), np.ones((HEAD_DIM, HEAD_DIM))), BF16)
    tri = jnp.asarray(np.triu(np.ones((tm_tok, tm_tok))), BF16)

    xc = ctx
    for i in range(depth):
        m = i % N_MIXERS
        j = i // N_MIXERS
        last = i == depth - 1
        wqkv_bf = w_qkv[i].astype(BF16)
        wo_bf = w_o[i].astype(BF16)
        qk_gains = None
        if m == 1:
            qk_gains = (jnp.tile(gq_b[j], LANES // HEAD_DIM).reshape(1, LANES),
                        jnp.tile(gk_b[j], LANES // HEAD_DIM).reshape(1, LANES), head_sum)

        if m == 2:
            qp, k, v = _qkv(x, modr, g_attn_rows, wqkv_bf, i, None, tm_tok, None, qk_gains)
            qr = None
        else:
            qp, qr, k, v = _qkv(x, modr, g_attn_rows, wqkv_bf, i, None, tm_tok, rope_tabs, qk_gains)
        qc, kc, vc = _qkv(xc, modr, g_attn_rows, wqkv_bf, i, ctx_row, tm_ctx, None, qk_gains)
        sink = sink_a[j] if m == 0 else None
        extra = sink if m == 0 else (_na_bias_tables(rpb_c[j]) if m == 2 else None)
        o = _attention(m, qp, qr, k, v, kc, vc, extra)

        router = _router_tables(w_group[i], b_group[i], w_router[i], b_router[i])
        streams = [(o, x, None)]
        if not last:
            oc = _ctx_attention(qc, kc, vc, sink)
            streams.append((oc.reshape(1, b * n_ctx, d), xc.reshape(1, b * n_ctx, d), ctx_row))
        routed = [_post_attn(o_s, x_s, wo_bf, modr, g_ffn_rows, router, i, row, tm_tok) for (o_s, x_s, row) in streams]

        n_rows = 2 * sum(r[2].shape[1] for r in routed)
        assert n_rows % EXPERT_BLOCK == 0
        base, visits = _moe_layout(jnp.concatenate([r[4] for r in routed], axis=0), n_rows)
        dest = _dest_rows(jnp.concatenate([r[2] for r in routed], axis=1), base, tri)
        xs = _scatter_rows(dest, [r[1].reshape(-1, d) for r in routed], n_rows)
        y = _experts(visits, xs, w_gate, w_up, w_down, i)
        nt_lat = routed[0][4].shape[0]
        x = _combine(dest[:nt_lat], routed[0][0], routed[0][3], modr, y, i, None, g_final if last else None)
        if not last:
            xc = _combine(dest[nt_lat:], routed[1][0], routed[1][3], modr, y, i, ctx_row).reshape(b, n_ctx, d)
    return x
```

```python
import functools

import numpy as np
import jax
import jax.numpy as jnp
from jax import lax
from jax.experimental import pallas as pl
from jax.experimental.pallas import tpu as pltpu

F32 = jnp.float32
BF16 = jnp.bfloat16
I32 = jnp.int32
U32 = jnp.uint32

D_MODEL = 1024
GRID_W = 64
N_MIXERS = 3
N_HEADS = 16
N_KV_HEADS = 4
HEAD_DIM = D_MODEL // N_HEADS
Q_PER_KV = N_HEADS // N_KV_HEADS
QKV_DIM = (N_HEADS + 2 * N_KV_HEADS) * HEAD_DIM
ATTN_SCALE = HEAD_DIM ** -0.5
ROPE_THETA = 10000.0
BLOCK_Q = 128
WINDOW = 128
NA_WIN_H = 8
NA_WIN_W = 16
N_GROUPS = 4
EXPERTS_PER_GROUP = 8
N_EXPERTS = N_GROUPS * EXPERTS_PER_GROUP
EXPERT_HIDDEN = D_MODEL // 2
NORM_EPS = 1e-6
NEG_INF = -1e30
LOG2E = 1.4426950408889634

LANES = 128
MOD_ROWS = 24
N_MOD = 6
EXPERT_BLOCK = 512
ROW_UNROLL = 8
ROUTER_COLS = LANES
NA_KEY_ROWS = 10
NA_TABLE = 18
VMEM_LIMIT = 56 * 1024 * 1024


def _cparams(sem, vmem=VMEM_LIMIT):
    return pltpu.CompilerParams(dimension_semantics=sem, vmem_limit_bytes=vmem)


def _dot(a, b):
    return jnp.dot(a, b, preferred_element_type=F32)


def _dot_nt(a, b):
    return lax.dot_general(a, b, (((1,), (1,)), ((), ())), preferred_element_type=F32)


def _split_bf16(x):
    hi = x.astype(BF16)
    lo = (x - hi.astype(F32)).astype(BF16)
    return hi, lo


def _pack_rows(x):
    w = x.shape[1] // 2
    bits = pltpu.bitcast(x.astype(BF16).astype(F32), U32)
    return (bits[:, w:] & jnp.uint32(0xFFFF0000)) | (bits[:, :w] >> 16)


def _unpack_rows(p):
    lo = pltpu.bitcast(p << 16, F32)
    hi = pltpu.bitcast(p & jnp.uint32(0xFFFF0000), F32)
    return lo, hi


def _rms_mod(x, g, sh, sc):
    ms = jnp.mean(x * x, axis=-1, keepdims=True)
    y = x * lax.rsqrt(ms + NORM_EPS) * g
    return y * (1.0 + sc) + sh


def _ada_kernel(c_ref, w_ref, b_ref, o_ref):
    c = c_ref[...]
    s = c * jax.nn.sigmoid(c)
    o_ref[0] = _dot(s.astype(BF16), w_ref[0].astype(BF16)) + b_ref[0]


def _ada_all(cc, w_ada, b_ada):
    depth, d, d6 = w_ada.shape
    tn = 1536
    return pl.pallas_call(
        _ada_kernel,
        grid=(depth, d6 // tn),
        in_specs=[
            pl.BlockSpec((MOD_ROWS, d), lambda i, j: (0, 0)),
            pl.BlockSpec((1, d, tn), lambda i, j: (i, 0, j)),
            pl.BlockSpec((1, 1, tn), lambda i, j: (i, 0, j)),
        ],
        out_specs=pl.BlockSpec((1, MOD_ROWS, tn), lambda i, j: (i, 0, j)),
        out_shape=jax.ShapeDtypeStruct((depth, MOD_ROWS, d6), F32),
        compiler_params=_cparams(("parallel", "parallel")),
        name="ada_mod",
    )(cc, w_ada, b_ada.reshape(depth, 1, d6))


def _mod_spec(layer, chunk, ctx_row):
    if ctx_row is None:
        return pl.BlockSpec((1, 1, D_MODEL), lambda b, t: ((layer * MOD_ROWS + b) * N_MOD + chunk, 0, 0))
    return pl.BlockSpec((1, 1, D_MODEL), lambda b, t: ((layer * MOD_ROWS + ctx_row) * N_MOD + chunk, 0, 0))


def _layer_row_spec(layer):
    return pl.BlockSpec((1, 1, D_MODEL), lambda b, t: (layer, 0, 0))


def _qkv_kernel(*refs, qk_norm, rope):
    x_ref, g_ref, sh_ref, sc_ref, w_ref = refs[:5]
    pos = 5
    if rope:
        cos_ref, sin_ref = refs[pos:pos + 2]
        pos += 2
    if qk_norm:
        gq_ref, gk_ref, bm_ref = refs[pos:pos + 3]
        pos += 3
    if rope:
        qp_ref, qr_ref, k_ref, v_ref = refs[pos:]
    else:
        qp_ref, k_ref, v_ref = refs[pos:]

    h = _rms_mod(x_ref[0], g_ref[0], sh_ref[0], sc_ref[0])
    res = _dot(h.astype(BF16), w_ref[...])
    half = LANES // 2
    n_q = N_HEADS * HEAD_DIM // LANES
    n_k = N_KV_HEADS * HEAD_DIM // LANES
    if rope:
        lane = lax.broadcasted_iota(I32, (1, LANES), 1)
        first_half = (lane & (HEAD_DIM // 2 - 1)) < (HEAD_DIM // 4)
    for cb in range(QKV_DIM // LANES):
        xs = res[:, cb * LANES:(cb + 1) * LANES]
        is_q = cb < n_q
        is_k = n_q <= cb < n_q + n_k
        if qk_norm and (is_q or is_k):
            hi, lo = _split_bf16(xs * xs)
            ms = (_dot(hi, bm_ref[...]) + _dot(lo, bm_ref[...])) * (1.0 / HEAD_DIM)
            gain = gq_ref[...] if is_q else gk_ref[...]
            xs = xs * lax.rsqrt(ms + NORM_EPS) * gain
        if is_q:
            xs = xs * (ATTN_SCALE * LOG2E)
        if rope and (is_q or is_k):
            rot = jnp.where(first_half, pltpu.roll(xs, LANES - HEAD_DIM // 4, 1), pltpu.roll(xs, HEAD_DIM // 4, 1))
            xr = xs * cos_ref[...] + rot * sin_ref[...]
        if is_q:
            hd = 2 * cb
            qp_ref[0, hd] = xs[:, :half].astype(BF16)
            qp_ref[0, hd + 1] = xs[:, half:].astype(BF16)
            if rope:
                qr_ref[0, hd] = xr[:, :half].astype(BF16)
                qr_ref[0, hd + 1] = xr[:, half:].astype(BF16)
        elif is_k:
            hd = 2 * (cb - n_q)
            kk = xr if rope else xs
            k_ref[0, hd] = kk[:, :half].astype(BF16)
            k_ref[0, hd + 1] = kk[:, half:].astype(BF16)
        else:
            hd = 2 * (cb - n_q - n_k)
            vlane = lax.broadcasted_iota(I32, (1, LANES), 1)
            ones_col = (vlane == HEAD_DIM).astype(F32)
            v_ref[0, hd] = jnp.where(vlane < HEAD_DIM, xs, ones_col).astype(BF16)
            v_ref[0, hd + 1] = jnp.where(vlane < HEAD_DIM, pltpu.roll(xs, half, 1), ones_col).astype(BF16)


def _qkv(x, modr, g_rows, w_bf, layer, ctx_row, tm, rope_tabs=None, qk_gains=None):
    b, n, d = x.shape
    rope = rope_tabs is not None
    qk_norm = qk_gains is not None
    args = [x, g_rows, modr, modr, w_bf]
    in_specs = [
        pl.BlockSpec((1, tm, d), lambda bb, t: (bb, t, 0)),
        _layer_row_spec(layer),
        _mod_spec(layer, 0, ctx_row),
        _mod_spec(layer, 1, ctx_row),
        pl.BlockSpec((d, QKV_DIM), lambda bb, t: (0, 0)),
    ]
    if rope:
        args += list(rope_tabs)
        in_specs += [pl.BlockSpec((tm, LANES), lambda bb, t: (t, 0))] * 2
    if qk_norm:
        args += list(qk_gains)
        in_specs += [pl.BlockSpec((1, LANES), lambda bb, t: (0, 0))] * 2
        in_specs += [pl.BlockSpec((LANES, LANES), lambda bb, t: (0, 0))]
    q_shape = jax.ShapeDtypeStruct((b, N_HEADS, n, HEAD_DIM), BF16)
    kv_shape = jax.ShapeDtypeStruct((b, N_KV_HEADS, n, HEAD_DIM), BF16)
    q_spec = pl.BlockSpec((1, N_HEADS, tm, HEAD_DIM), lambda bb, t: (bb, 0, t, 0))
    kv_spec = pl.BlockSpec((1, N_KV_HEADS, tm, HEAD_DIM), lambda bb, t: (bb, 0, t, 0))
    v_shape = jax.ShapeDtypeStruct((b, N_KV_HEADS, n, LANES), BF16)
    v_spec = pl.BlockSpec((1, N_KV_HEADS, tm, LANES), lambda bb, t: (bb, 0, t, 0))
    if rope:
        out_shape, out_specs = [q_shape, q_shape, kv_shape, v_shape], [q_spec, q_spec, kv_spec, v_spec]
    else:
        out_shape, out_specs = [q_shape, kv_shape, v_shape], [q_spec, kv_spec, v_spec]
    return pl.pallas_call(
        functools.partial(_qkv_kernel, qk_norm=qk_norm, rope=rope),
        grid=(b, n // tm),
        in_specs=in_specs,
        out_specs=out_specs,
        out_shape=out_shape,
        compiler_params=_cparams(("parallel", "parallel")),
        name="qkv_rope" if rope else "qkv",
    )(*args)


def _attend(s, v_aug):
    m = jnp.max(s, axis=-1, keepdims=True)
    p = jnp.exp2(s - m)
    return m, _dot(p.astype(BF16), v_aug)


def _normalise(acc):
    return acc[:, :HEAD_DIM] / acc[:, HEAD_DIM:HEAD_DIM + 1]


def _normalise_with_sink(m, acc, sink_ref, kv_head, rows_per_head):
    pieces = []
    for g in range(Q_PER_KV):
        sink = sink_ref[kv_head * Q_PER_KV + g] * LOG2E
        mg = m[g * rows_per_head:(g + 1) * rows_per_head]
        ag = acc[g * rows_per_head:(g + 1) * rows_per_head]
        m2 = jnp.maximum(mg, sink)
        scale = jnp.exp2(mg - m2)
        denom = ag[:, HEAD_DIM:HEAD_DIM + 1] * scale + jnp.exp2(sink - m2)
        pieces.append(ag[:, :HEAD_DIM] * scale / denom)
    return pieces


def _split_heads(o, rows_per_head):
    return [o[g * rows_per_head:(g + 1) * rows_per_head] for g in range(Q_PER_KV)]


def _attn_kernel(*refs, mode, key_chunk):
    if mode == 0:
        qp_ref, qr_ref, k_ref, v_ref, kc_ref, vc_ref, sink_ref, o_ref = refs
    elif mode == 1:
        qp_ref, qr_ref, k_ref, v_ref, kc_ref, vc_ref, o_ref = refs
    else:
        qp_ref, k_ref, v_ref, kc_ref, vc_ref, t2_ref, o_ref = refs
    qb = pl.program_id(1)
    n = k_ref.shape[2]
    rows = Q_PER_KV * BLOCK_Q

    if mode == 0:
        span = BLOCK_Q + 2 * WINDOW
        kstart = pl.multiple_of(jnp.clip(qb * BLOCK_Q - WINDOW, 0, n - span), BLOCK_Q)
        qpos = qb * BLOCK_Q + (lax.broadcasted_iota(I32, (rows, span), 0) & (BLOCK_Q - 1))
        kpos = kstart + lax.broadcasted_iota(I32, (rows, span), 1)
        band = jnp.abs(kpos - qpos) <= WINDOW
    elif mode == 2:
        rows_grid = n // GRID_W
        r0 = qb * (BLOCK_Q // GRID_W)
        wr = jnp.clip(r0 - NA_WIN_H // 2, 0, rows_grid - NA_KEY_ROWS)
        kstart = pl.multiple_of(wr * GRID_W, GRID_W)
        lane = lax.broadcasted_iota(I32, (GRID_W, LANES), 1)

    pieces = []
    for h in range(N_KV_HEADS):
        qp = qp_ref[0, h * Q_PER_KV:(h + 1) * Q_PER_KV].reshape(rows, HEAD_DIM)
        s_ctx = _dot_nt(qp, kc_ref[0, h])
        if mode == 0:
            qr = qr_ref[0, h * Q_PER_KV:(h + 1) * Q_PER_KV].reshape(rows, HEAD_DIM)
            s = jnp.where(band, _dot_nt(qr, k_ref[0, h, pl.ds(kstart, span), :]), NEG_INF)
            v = jnp.concatenate([v_ref[0, h, pl.ds(kstart, span), :], vc_ref[0, h]], axis=0)
            m, acc = _attend(jnp.concatenate([s, s_ctx], axis=1), v)
            pieces += _normalise_with_sink(m, acc, sink_ref, h, BLOCK_Q)
        elif mode == 1:
            qr = qr_ref[0, h * Q_PER_KV:(h + 1) * Q_PER_KV].reshape(rows, HEAD_DIM)
            n_chunks = n // key_chunk
            m, acc = _attend(_dot_nt(qr, k_ref[0, h, 0:key_chunk, :]), v_ref[0, h, 0:key_chunk, :])
            for c in range(1, n_chunks):
                s = _dot_nt(qr, k_ref[0, h, c * key_chunk:(c + 1) * key_chunk, :])
                v = v_ref[0, h, c * key_chunk:(c + 1) * key_chunk, :]
                if c == n_chunks - 1:
                    s = jnp.concatenate([s, s_ctx], axis=1)
                    v = jnp.concatenate([v, vc_ref[0, h]], axis=0)
                m_new = jnp.maximum(m, jnp.max(s, axis=-1, keepdims=True))
                p = jnp.exp2(s - m_new)
                acc = jnp.exp2(m - m_new) * acc + _dot(p.astype(BF16), v)
                m = m_new
            pieces += _split_heads(_normalise(acc), BLOCK_Q)
        else:
            n_keys = NA_KEY_ROWS * GRID_W
            s = _dot_nt(qp, k_ref[0, h, pl.ds(kstart, n_keys), :])
            blocks = []
            for g in range(Q_PER_KV):
                for a in range(BLOCK_Q // GRID_W):
                    r = r0 + a
                    rs = jnp.clip(r - NA_WIN_H // 2, 0, rows_grid - NA_WIN_H)
                    tiles = []
                    for j in range(n_keys // LANES):
                        kr0 = wr + 2 * j
                        tab = t2_ref[h * Q_PER_KV + g, kr0 - r + NA_TABLE // 2]
                        pen0 = jnp.where((kr0 >= rs) & (kr0 < rs + NA_WIN_H), 0.0, NEG_INF)
                        pen1 = jnp.where((kr0 + 1 >= rs) & (kr0 + 1 < rs + NA_WIN_H), 0.0, NEG_INF)
                        tiles.append(tab + jnp.where(lane < GRID_W, pen0, pen1))
                    blocks.append(jnp.concatenate(tiles, axis=1))
            s = s + jnp.concatenate(blocks, axis=0)
            v = jnp.concatenate([v_ref[0, h, pl.ds(kstart, n_keys), :], vc_ref[0, h]], axis=0)
            _, acc = _attend(jnp.concatenate([s, s_ctx], axis=1), v)
            pieces += _split_heads(_normalise(acc), BLOCK_Q)
    o_ref[0] = jnp.concatenate(pieces, axis=-1).astype(o_ref.dtype)


def _attention(mode, qp, qr, k, v, kc, vc, extra):
    b, _, n, _ = qp.shape
    c = kc.shape[2]
    q_spec = pl.BlockSpec((1, N_HEADS, BLOCK_Q, HEAD_DIM), lambda bb, i: (bb, 0, i, 0))
    k_spec = pl.BlockSpec((1, N_KV_HEADS, n, HEAD_DIM), lambda bb, i: (bb, 0, 0, 0))
    v_spec = pl.BlockSpec((1, N_KV_HEADS, n, LANES), lambda bb, i: (bb, 0, 0, 0))
    kc_spec = pl.BlockSpec((1, N_KV_HEADS, c, HEAD_DIM), lambda bb, i: (bb, 0, 0, 0))
    vc_spec = pl.BlockSpec((1, N_KV_HEADS, c, LANES), lambda bb, i: (bb, 0, 0, 0))
    if mode == 2:
        args = [qp, k, v, kc, vc, extra]
        in_specs = [q_spec, k_spec, v_spec, kc_spec, vc_spec,
                    pl.BlockSpec(extra.shape, lambda bb, i: (0, 0, 0, 0))]
    else:
        args = [qp, qr, k, v, kc, vc]
        in_specs = [q_spec, q_spec, k_spec, v_spec, kc_spec, vc_spec]
        if mode == 0:
            args.append(extra)
            in_specs.append(pl.BlockSpec(memory_space=pltpu.SMEM))
    return pl.pallas_call(
        functools.partial(_attn_kernel, mode=mode, key_chunk=512),
        grid=(b, n // BLOCK_Q),
        in_specs=in_specs,
        out_specs=pl.BlockSpec((1, BLOCK_Q, D_MODEL), lambda bb, i: (bb, i, 0)),
        out_shape=jax.ShapeDtypeStruct((b, n, D_MODEL), BF16),
        compiler_params=_cparams(("parallel", "arbitrary")),
        name=("attn_window", "attn_global", "attn_na")[mode],
    )(*args)


def _ctx_attn_kernel(*refs, has_sink):
    if has_sink:
        q_ref, k_ref, v_ref, sink_ref, o_ref = refs
    else:
        q_ref, k_ref, v_ref, o_ref = refs
    c = k_ref.shape[2]
    rows = Q_PER_KV * c
    pieces = []
    for h in range(N_KV_HEADS):
        q = q_ref[0, h * Q_PER_KV:(h + 1) * Q_PER_KV].reshape(rows, HEAD_DIM)
        m, acc = _attend(_dot_nt(q, k_ref[0, h]), v_ref[0, h])
        if has_sink:
            pieces += _normalise_with_sink(m, acc, sink_ref, h, c)
        else:
            pieces += _split_heads(_normalise(acc), c)
    o_ref[0] = jnp.concatenate(pieces, axis=-1).astype(o_ref.dtype)


def _ctx_attention(qc, kc, vc, sink):
    b, _, c, _ = qc.shape
    args = [qc, kc, vc]
    in_specs = [pl.BlockSpec((1, N_HEADS, c, HEAD_DIM), lambda bb: (bb, 0, 0, 0)),
                pl.BlockSpec((1, N_KV_HEADS, c, HEAD_DIM), lambda bb: (bb, 0, 0, 0)),
                pl.BlockSpec((1, N_KV_HEADS, c, LANES), lambda bb: (bb, 0, 0, 0))]
    if sink is not None:
        args.append(sink)
        in_specs.append(pl.BlockSpec(memory_space=pltpu.SMEM))
    return pl.pallas_call(
        functools.partial(_ctx_attn_kernel, has_sink=sink is not None),
        grid=(b,),
        in_specs=in_specs,
        out_specs=pl.BlockSpec((1, c, D_MODEL), lambda bb: (bb, 0, 0)),
        out_shape=jax.ShapeDtypeStruct((b, c, D_MODEL), BF16),
        compiler_params=_cparams(("parallel",)),
        name="attn_ctx",
    )(*args)


def _post_attn_kernel(o_ref, x_ref, wo_ref, gt_ref, g_ref, sh_ref, sc_ref, wrh_ref, wrl_ref, br_ref,
                      xn_ref, hf_ref, eid_ref, wcol_ref, cnt_ref):
    xn = x_ref[0] + gt_ref[0] * _dot(o_ref[0], wo_ref[...])
    xn_ref[0] = xn
    hf = _rms_mod(xn, g_ref[0], sh_ref[0], sc_ref[0])
    hf_ref[0] = _pack_rows(hf)
    tm = hf.shape[0]

    hh, hl = _split_bf16(hf)
    lg = _dot(hh, wrh_ref[...]) + _dot(hl, wrh_ref[...]) + _dot(hh, wrl_ref[...]) + br_ref[...]
    lgt = lg.T

    gl = lgt[0:N_GROUPS]
    ge = jnp.exp(gl - jnp.max(gl, axis=0, keepdims=True))
    pg = ge / jnp.sum(ge, axis=0, keepdims=True)
    p_top = jnp.max(pg, axis=0, keepdims=True)
    grow = lax.broadcasted_iota(I32, (N_GROUPS, tm), 0)
    grp = jnp.min(jnp.where(pg == p_top, grow, N_GROUPS), axis=0, keepdims=True)

    el = lgt[8:8 + EXPERTS_PER_GROUP]
    for gi in range(1, N_GROUPS):
        el = jnp.where(grp == gi, lgt[8 + gi * EXPERTS_PER_GROUP:8 + (gi + 1) * EXPERTS_PER_GROUP], el)
    ee = jnp.exp(el - jnp.max(el, axis=0, keepdims=True))
    pe = ee / jnp.sum(ee, axis=0, keepdims=True)
    erow = lax.broadcasted_iota(I32, (EXPERTS_PER_GROUP, tm), 0)
    p1 = jnp.max(pe, axis=0, keepdims=True)
    i1 = jnp.min(jnp.where(pe == p1, erow, EXPERTS_PER_GROUP), axis=0, keepdims=True)
    pe2 = jnp.where(erow == i1, -1.0, pe)
    p2 = jnp.max(pe2, axis=0, keepdims=True)
    i2 = jnp.min(jnp.where(pe2 == p2, erow, EXPERTS_PER_GROUP), axis=0, keepdims=True)
    den = p1 + p2
    w1 = p_top * p1 / den
    w2 = p_top * p2 / den
    e1 = grp * EXPERTS_PER_GROUP + i1
    e2 = grp * EXPERTS_PER_GROUP + i2

    eid_ref[0:1, :] = e1
    eid_ref[1:2, :] = e2
    wrows = jnp.concatenate([w1, w2, jnp.zeros((ROUTER_COLS - 2, tm), F32)], axis=0)
    wcol_ref[...] = wrows.T
    xrow = lax.broadcasted_iota(I32, (N_EXPERTS, tm), 0)
    hits = (xrow == e1).astype(F32) + (xrow == e2).astype(F32)
    cnt_ref[0] = jnp.sum(hits, axis=1, keepdims=True).astype(I32)


def _post_attn(o, x, wo_bf, modr, g_rows, router, layer, ctx_row, tm):
    b, n, d = x.shape
    nt = n // tm
    t_all = b * n
    wrh, wrl, br = router
    tile = lambda bb, t: (bb, t, 0)
    const2 = lambda bb, t: (0, 0)
    return pl.pallas_call(
        _post_attn_kernel,
        grid=(b, nt),
        in_specs=[
            pl.BlockSpec((1, tm, d), tile),
            pl.BlockSpec((1, tm, d), tile),
            pl.BlockSpec((d, d), const2),
            _mod_spec(layer, 2, ctx_row),
            _layer_row_spec(layer),
            _mod_spec(layer, 3, ctx_row),
            _mod_spec(layer, 4, ctx_row),
            pl.BlockSpec((d, ROUTER_COLS), const2),
            pl.BlockSpec((d, ROUTER_COLS), const2),
            pl.BlockSpec((1, ROUTER_COLS), const2),
        ],
        out_specs=[
            pl.BlockSpec((1, tm, d), tile),
            pl.BlockSpec((1, tm, d // 2), tile),
            pl.BlockSpec((2, tm), lambda bb, t: (0, bb * nt + t)),
            pl.BlockSpec((tm, ROUTER_COLS), lambda bb, t: (bb * nt + t, 0)),
            pl.BlockSpec((1, N_EXPERTS, 1), lambda bb, t: (bb * nt + t, 0, 0)),
        ],
        out_shape=[
            jax.ShapeDtypeStruct((b, n, d), F32),
            jax.ShapeDtypeStruct((b, n, d // 2), U32),
            jax.ShapeDtypeStruct((2, t_all), I32),
            jax.ShapeDtypeStruct((t_all, ROUTER_COLS), F32),
            jax.ShapeDtypeStruct((b * nt, N_EXPERTS, 1), I32),
        ],
        compiler_params=_cparams(("parallel", "parallel")),
        name="post_attn_router",
    )(o, x, wo_bf, modr, g_rows, modr, modr, wrh, wrl, br)


def _dest_kernel(eid_ref, base_ref, tri_ref, dest_ref):
    tm = eid_ref.shape[1]
    xrow = lax.broadcasted_iota(I32, (N_EXPERTS, tm), 0)
    oh0 = xrow == eid_ref[0:1, :]
    oh1 = xrow == eid_ref[1:2, :]
    both = oh0.astype(F32) + oh1.astype(F32)
    incl = _dot(both.astype(BF16), tri_ref[...])
    before = incl - both + base_ref[0].astype(F32)
    d0 = jnp.sum(jnp.where(oh0, before, 0.0), axis=0, keepdims=True)
    d1 = jnp.sum(jnp.where(oh1, before, 0.0), axis=0, keepdims=True)
    dest_ref[0, 0:1, :] = d0.astype(I32)
    dest_ref[0, 1:2, :] = d1.astype(I32)


def _dest_rows(eid, base, tri):
    tm = tri.shape[0]
    nt = eid.shape[1] // tm
    return pl.pallas_call(
        _dest_kernel,
        grid=(nt,),
        in_specs=[
            pl.BlockSpec((2, tm), lambda i: (0, i)),
            pl.BlockSpec((1, N_EXPERTS, 1), lambda i: (i, 0, 0)),
            pl.BlockSpec((tm, tm), lambda i: (0, 0)),
        ],
        out_specs=pl.BlockSpec((1, 2, tm), lambda i: (i, 0, 0)),
        out_shape=jax.ShapeDtypeStruct((nt, 2, tm), I32),
        compiler_params=_cparams(("parallel",)),
        name="moe_dest",
    )(eid, base, tri)


def _scatter_kernel(*refs, tiles):
    dest_ref = refs[0]
    hf_refs = refs[1:1 + len(tiles)]
    xs_ref, sem = refs[1 + len(tiles):]
    tm = hf_refs[0].shape[0]
    i = pl.program_id(0)

    def scatter_tile(hf_ref):
        def issue(t8, carry):
            base = pl.multiple_of(t8 * ROW_UNROLL, ROW_UNROLL)
            rows = hf_ref.at[pl.ds(base, ROW_UNROLL)]
            for u in range(ROW_UNROLL):
                for k in range(2):
                    d = dest_ref.at[0, k, pl.ds(base, ROW_UNROLL)][u]
                    pltpu.make_async_copy(rows.at[pl.ds(u, 1)], xs_ref.at[pl.ds(d, 1)], sem).start(priority=k)
            return carry

        lax.fori_loop(0, tm // ROW_UNROLL, issue, 0)
        for k in range(2):
            pltpu.make_async_copy(hf_ref, xs_ref.at[pl.ds(0, tm)], sem).wait()

    first = 0
    for hf_ref, nt in zip(hf_refs, tiles):
        pl.when((i >= first) & (i < first + nt))(functools.partial(scatter_tile, hf_ref))
        first += nt


def _scatter_rows(dest, hf_streams, n_rows):
    nt_all, _, tm = dest.shape
    d = hf_streams[0].shape[1]
    tiles = tuple(h.shape[0] // tm for h in hf_streams)
    assert sum(tiles) == nt_all
    in_specs = [pl.BlockSpec((1, 2, tm), lambda i: (i, 0, 0), memory_space=pltpu.SMEM)]
    first = 0
    for nt in tiles:
        in_specs.append(pl.BlockSpec((tm, d), lambda i, first=first, nt=nt: (jnp.clip(i - first, 0, nt - 1), 0)))
        first += nt
    return pl.pallas_call(
        functools.partial(_scatter_kernel, tiles=tiles),
        grid=(nt_all,),
        in_specs=in_specs,
        out_specs=pl.BlockSpec(memory_space=pl.ANY),
        out_shape=jax.ShapeDtypeStruct((n_rows, d), hf_streams[0].dtype),
        scratch_shapes=[pltpu.SemaphoreType.DMA(())],
        compiler_params=_cparams(("arbitrary",)),
        name="moe_scatter",
    )(dest, *hf_streams)


def _expert_kernel(ve_ref, vblk_ref, lo_ref, hi_ref, x_ref, wg_ref, wu_ref, wd_ref, y_ref, wg_s, wu_s, wd_s):
    v = pl.program_id(0)
    pv = jnp.maximum(v - 1, 0)
    new_expert = (v == 0) | (ve_ref[v] != ve_ref[pv])
    first_visit = (v == 0) | (vblk_ref[v] != vblk_ref[pv])
    lo = lo_ref[v]
    hi = hi_ref[v]

    @pl.when(new_expert)
    def _():
        wg_s[...] = wg_ref[0, 0].astype(BF16)
        wu_s[...] = wu_ref[0, 0].astype(BF16)
        wd_s[...] = wd_ref[0, 0].astype(BF16)

    @pl.when(hi > lo)
    def _():
        x_lo, x_hi = _unpack_rows(x_ref[...])
        x_lo = x_lo.astype(BF16)
        x_hi = x_hi.astype(BF16)
        half = x_lo.shape[1]
        gate = _dot(x_lo, wg_s[:half]) + _dot(x_hi, wg_s[half:])
        up = _dot(x_lo, wu_s[:half]) + _dot(x_hi, wu_s[half:])
        mid = gate * jax.nn.sigmoid(gate) * up
        y = _pack_rows(_dot(mid.astype(BF16), wd_s[...]))
        row = lax.broadcasted_iota(I32, (EXPERT_BLOCK, 1), 0)
        mine = (row >= lo) & (row < hi)

        @pl.when(first_visit)
        def _():
            y_ref[...] = jnp.where(mine, y, jnp.uint32(0))

        @pl.when(jnp.logical_not(first_visit))
        def _():
            y_ref[...] = jnp.where(mine, y, y_ref[...])


def _experts(visits, xs, w_gate, w_up, w_down, layer):
    n_rows, dp = xs.shape
    _, _, d, f = w_gate.shape
    assert d == 2 * dp
    ve, vblk, lo, hi = visits
    x_map = lambda v, ve_r, vb_r, lo_r, hi_r: (vb_r[v], 0)
    w_map = lambda v, ve_r, vb_r, lo_r, hi_r: (layer, ve_r[v], 0, 0)
    grid_spec = pltpu.PrefetchScalarGridSpec(
        num_scalar_prefetch=4,
        grid=(ve.shape[0],),
        in_specs=[
            pl.BlockSpec((EXPERT_BLOCK, dp), x_map),
            pl.BlockSpec((1, 1, d, f), w_map),
            pl.BlockSpec((1, 1, d, f), w_map),
            pl.BlockSpec((1, 1, f, d), w_map),
        ],
        out_specs=pl.BlockSpec((EXPERT_BLOCK, dp), x_map),
        scratch_shapes=[pltpu.VMEM((d, f), BF16), pltpu.VMEM((d, f), BF16), pltpu.VMEM((f, d), BF16)],
    )
    return pl.pallas_call(
        _expert_kernel,
        grid_spec=grid_spec,
        out_shape=jax.ShapeDtypeStruct((n_rows, dp), U32),
        compiler_params=_cparams(("arbitrary",)),
        name="moe_experts",
    )(ve, vblk, lo, hi, xs, w_gate, w_up, w_down)


def _combine_kernel(*refs, final_norm):
    if final_norm:
        dest_ref, dest_next_ref, xn_ref, wcol_ref, gt_ref, gfin_ref, y_ref, o_ref, ybuf, sems = refs
    else:
        dest_ref, dest_next_ref, xn_ref, wcol_ref, gt_ref, y_ref, o_ref, ybuf, sems = refs
    tm = xn_ref.shape[1]
    step = pl.program_id(0) * pl.num_programs(1) + pl.program_id(1)
    n_steps = pl.num_programs(0) * pl.num_programs(1)

    def gather_tile(d_ref, slot):
        def issue(t8, carry):
            base = pl.multiple_of(t8 * ROW_UNROLL, ROW_UNROLL)
            for k in range(2):
                rows = ybuf.at[slot, k, pl.ds(base, ROW_UNROLL)]
                for u in range(ROW_UNROLL):
                    d = d_ref.at[0, k, pl.ds(base, ROW_UNROLL)][u]
                    pltpu.make_async_copy(y_ref.at[pl.ds(d, 1)], rows.at[pl.ds(u, 1)], sems.at[slot]).start(priority=k)
            return carry

        lax.fori_loop(0, tm // ROW_UNROLL, issue, 0)

    def finish_tile(slot):
        for k in range(2):
            pltpu.make_async_copy(y_ref.at[pl.ds(0, tm)], ybuf.at[slot, k], sems.at[slot]).wait()
        w = wcol_ref[...]
        lo0, hi0 = _unpack_rows(ybuf[slot, 0])
        lo1, hi1 = _unpack_rows(ybuf[slot, 1])
        moe = jnp.concatenate([w[:, 0:1] * lo0 + w[:, 1:2] * lo1, w[:, 0:1] * hi0 + w[:, 1:2] * hi1], axis=1)
        out = xn_ref[0] + gt_ref[0] * moe
        if final_norm:
            ms = jnp.mean(out * out, axis=-1, keepdims=True)
            out = out * lax.rsqrt(ms + NORM_EPS) * gfin_ref[...]
        o_ref[0] = out

    pl.when(step == 0)(functools.partial(gather_tile, dest_ref, 0))
    for slot in range(2):
        @pl.when((step % 2 == slot) & (step + 1 < n_steps))
        def _(slot=slot):
            gather_tile(dest_next_ref, 1 - slot)

        pl.when(step % 2 == slot)(functools.partial(finish_tile, slot))


def _combine(dest, xn, wcol, modr, y, layer, ctx_row, g_final=None):
    b, n, d = xn.shape
    nt_all, _, tm = dest.shape
    nt = n // tm
    final_norm = g_final is not None
    args = [dest, dest, xn, wcol, modr]
    in_specs = [
        pl.BlockSpec((1, 2, tm), lambda bb, t: (bb * nt + t, 0, 0), memory_space=pltpu.SMEM),
        pl.BlockSpec((1, 2, tm), lambda bb, t: (jnp.minimum(bb * nt + t + 1, nt_all - 1), 0, 0),
                     memory_space=pltpu.SMEM),
        pl.BlockSpec((1, tm, d), lambda bb, t: (bb, t, 0)),
        pl.BlockSpec((tm, ROUTER_COLS), lambda bb, t: (bb * nt + t, 0)),
        _mod_spec(layer, 5, ctx_row),
    ]
    if final_norm:
        args.append(g_final.reshape(1, d))
        in_specs.append(pl.BlockSpec((1, d), lambda bb, t: (0, 0)))
    args.append(y)
    in_specs.append(pl.BlockSpec(memory_space=pl.ANY))
    return pl.pallas_call(
        functools.partial(_combine_kernel, final_norm=final_norm),
        grid=(b, nt),
        in_specs=in_specs,
        out_specs=pl.BlockSpec((1, tm, d), lambda bb, t: (bb, t, 0)),
        out_shape=jax.ShapeDtypeStruct((b, n, d), F32),
        scratch_shapes=[pltpu.VMEM((2, 2, tm, y.shape[1]), y.dtype), pltpu.SemaphoreType.DMA((2,))],
        compiler_params=_cparams(("arbitrary", "arbitrary")),
        name="moe_combine",
    )(*args)


def _rope_tables(n):
    t = np.arange(n)
    row = (t // GRID_W).astype(np.float32)
    col = (t % GRID_W).astype(np.float32)
    quarter = HEAD_DIM // 4
    inv = jnp.asarray(ROPE_THETA, F32) ** (-jnp.arange(quarter, dtype=F32) / quarter)
    ar = jnp.asarray(row)[:, None] * inv
    ac = jnp.asarray(col)[:, None] * inv
    ang = jnp.concatenate([ar, ar, ac, ac], axis=-1)
    ang = jnp.concatenate([ang, ang], axis=-1)
    sign = np.where((np.arange(LANES) % (HEAD_DIM // 2)) < quarter, -1.0, 1.0).astype(np.float32)
    return jnp.cos(ang), jnp.sin(ang) * sign


def _na_bias_tables(rpb):
    n_dr, n_dc = 2 * NA_WIN_H - 1, 2 * NA_WIN_W - 1
    blocked = jnp.full((N_HEADS, 2, n_dc), NEG_INF, F32)
    rows = jnp.concatenate([blocked, rpb.astype(F32) * LOG2E, blocked], axis=1)
    assert rows.shape[1] == n_dr + 4 == NA_TABLE + 1
    pair = jnp.stack([rows[:, 0:NA_TABLE], rows[:, 1:NA_TABLE + 1]], axis=2)
    c = np.arange(GRID_W)[:, None]
    kc = np.arange(GRID_W)[None, :]
    cs = np.clip(c - NA_WIN_W // 2, 0, GRID_W - NA_WIN_W)
    in_window = (kc >= cs) & (kc < cs + NA_WIN_W)
    out = jnp.full((N_HEADS, NA_TABLE, GRID_W, 2, GRID_W), NEG_INF, F32)
    for dc in range(-(NA_WIN_W - 1), NA_WIN_W):
        hit = ((kc - c) == dc) & in_window
        if hit.any():
            out = jnp.where(hit[None, None, :, None, :], pair[:, :, None, :, dc + NA_WIN_W - 1, None], out)
    return out.reshape(N_HEADS, NA_TABLE, GRID_W, LANES)


def _router_tables(w_group, b_group, w_router, b_router):
    d = w_group.shape[0]
    w = jnp.zeros((d, ROUTER_COLS), F32)
    w = w.at[:, 0:N_GROUPS].set(w_group).at[:, 8:8 + N_EXPERTS].set(w_router)
    bias = jnp.zeros((1, ROUTER_COLS), F32)
    bias = bias.at[0, 0:N_GROUPS].set(b_group).at[0, 8:8 + N_EXPERTS].set(b_router)
    hi = w.astype(BF16)
    lo = (w - hi.astype(F32)).astype(BF16)
    return hi, lo, bias


def _pick(table, onehot):
    return jnp.sum(jnp.where(onehot, table[None, :], 0), axis=1)


def _moe_layout(cnt_tiles, n_rows):
    cnt = cnt_tiles[:, :, 0]
    counts = jnp.sum(cnt, axis=0)
    ends = jnp.cumsum(counts)
    starts = ends - counts
    base = starts[None, :] + jnp.cumsum(cnt, axis=0) - cnt
    first_blk = starts // EXPERT_BLOCK
    n_blk = jnp.where(counts > 0, (ends - 1) // EXPERT_BLOCK - first_blk + 1, 0)
    v_end = jnp.cumsum(n_blk)
    v_off = v_end - n_blk
    n_visits = n_rows // EXPERT_BLOCK + N_EXPERTS
    v = jnp.arange(n_visits, dtype=I32)
    valid = v < v_end[-1]
    vv = jnp.minimum(v, v_end[-1] - 1)
    ve = jnp.sum((v_end[None, :] <= vv[:, None]).astype(I32), axis=1)
    onehot = ve[:, None] == jnp.arange(N_EXPERTS, dtype=I32)[None, :]
    vblk = _pick(first_blk, onehot) + vv - _pick(v_off, onehot)
    lo = jnp.clip(_pick(starts, onehot) - vblk * EXPERT_BLOCK, 0, EXPERT_BLOCK)
    hi = jnp.clip(_pick(ends, onehot) - vblk * EXPERT_BLOCK, 0, EXPERT_BLOCK)
    hi = jnp.where(valid, hi, lo)
    visits = tuple(a.astype(I32) for a in (ve, vblk, lo, hi))
    return base[:, :, None].astype(I32), visits


def kernel(x, c, ctx, c_ctx, w_ada, b_ada, g_attn, w_qkv, w_o, sink_a, gq_b, gk_b, rpb_c, g_ffn,
           w_group, b_group, w_router, b_router, w_gate, w_up, w_down, g_final):
    b, n, d = x.shape
    n_ctx = ctx.shape[1]
    depth = w_ada.shape[0]
    tm_tok = 512
    tm_ctx = 256
    assert b + 1 <= MOD_ROWS and d == D_MODEL and n % tm_tok == 0 and n_ctx % tm_ctx == 0
    assert (b * n_ctx) % tm_tok == 0
    ctx_row = b

    cc = jnp.zeros((MOD_ROWS, d), F32).at[:b].set(c).at[b].set(c_ctx)
    modr = _ada_all(cc, w_ada, b_ada).reshape(depth * MOD_ROWS * N_MOD, 1, d)
    g_attn_rows = g_attn.reshape(depth, 1, d)
    g_ffn_rows = g_ffn.reshape(depth, 1, d)
    rope_tabs = _rope_tables(n)
    head_sum = jnp.asarray(np.kron(np.eye(LANES // HEAD_DIM), np.ones((HEAD_DIM, HEAD_DIM))), BF16)
    tri = jnp.asarray(np.triu(np.ones((tm_tok, tm_tok))), BF16)

    xc = ctx
    for i in range(depth):
        m = i % N_MIXERS
        j = i // N_MIXERS
        last = i == depth - 1
        wqkv_bf = w_qkv[i].astype(BF16)
        wo_bf = w_o[i].astype(BF16)
        qk_gains = None
        if m == 1:
            qk_gains = (jnp.tile(gq_b[j], LANES // HEAD_DIM).reshape(1, LANES),
                        jnp.tile(gk_b[j], LANES // HEAD_DIM).reshape(1, LANES), head_sum)

        if m == 2:
            qp, k, v = _qkv(x, modr, g_attn_rows, wqkv_bf, i, None, tm_tok, None, qk_gains)
            qr = None
        else:
            qp, qr, k, v = _qkv(x, modr, g_attn_rows, wqkv_bf, i, None, tm_tok, rope_tabs, qk_gains)
        qc, kc, vc = _qkv(xc, modr, g_attn_rows, wqkv_bf, i, ctx_row, tm_ctx, None, qk_gains)
        sink = sink_a[j] if m == 0 else None
        extra = sink if m == 0 else (_na_bias_tables(rpb_c[j]) if m == 2 else None)
        o = _attention(m, qp, qr, k, v, kc, vc, extra)

        router = _router_tables(w_group[i], b_group[i], w_router[i], b_router[i])
        streams = [(o, x, None)]
        if not last:
            oc = _ctx_attention(qc, kc, vc, sink)
            streams.append((oc.reshape(1, b * n_ctx, d), xc.reshape(1, b * n_ctx, d), ctx_row))
        routed = [_post_attn(o_s, x_s, wo_bf, modr, g_ffn_rows, router, i, row, tm_tok) for (o_s, x_s, row) in streams]

        n_rows = 2 * sum(r[2].shape[1] for r in routed)
        assert n_rows % EXPERT_BLOCK == 0
        base, visits = _moe_layout(jnp.concatenate([r[4] for r in routed], axis=0), n_rows)
        dest = _dest_rows(jnp.concatenate([r[2] for r in routed], axis=1), base, tri)
        xs = _scatter_rows(dest, [r[1].reshape(-1, d // 2) for r in routed], n_rows)
        y = _experts(visits, xs, w_gate, w_up, w_down, i)
        nt_lat = routed[0][4].shape[0]
        x = _combine(dest[:nt_lat], routed[0][0], routed[0][3], modr, y, i, None, g_final if last else None)
        if not last:
            xc = _combine(dest[nt_lat:], routed[1][0], routed[1][3], modr, y, i, ctx_row).reshape(b, n_ctx, d)
    return x
```

```python
import functools

import numpy as np
import jax
import jax.numpy as jnp
from jax import lax
from jax.experimental import pallas as pl
from jax.experimental.pallas import tpu as pltpu

F32 = jnp.float32
BF16 = jnp.bfloat16
I32 = jnp.int32
U32 = jnp.uint32

D_MODEL = 1024
GRID_W = 64
N_MIXERS = 3
N_HEADS = 16
N_KV_HEADS = 4
HEAD_DIM = D_MODEL // N_HEADS
Q_PER_KV = N_HEADS // N_KV_HEADS
QKV_DIM = (N_HEADS + 2 * N_KV_HEADS) * HEAD_DIM
ATTN_SCALE = HEAD_DIM ** -0.5
ROPE_THETA = 10000.0
BLOCK_Q = 128
WINDOW = 128
NA_WIN_H = 8
NA_WIN_W = 16
N_GROUPS = 4
EXPERTS_PER_GROUP = 8
N_EXPERTS = N_GROUPS * EXPERTS_PER_GROUP
EXPERT_HIDDEN = D_MODEL // 2
NORM_EPS = 1e-6
NEG_INF = -1e30
LOG2E = 1.4426950408889634

LANES = 128
MOD_ROWS = 24
N_MOD = 6
EXPERT_BLOCK = 512
ROW_UNROLL = 8
ROUTER_COLS = LANES
NA_KEY_ROWS = 10
NA_TABLE = 18
VMEM_LIMIT = 56 * 1024 * 1024


def _cparams(sem, vmem=VMEM_LIMIT):
    return pltpu.CompilerParams(dimension_semantics=sem, vmem_limit_bytes=vmem)


def _dot(a, b):
    return jnp.dot(a, b, preferred_element_type=F32)


def _dot_nt(a, b):
    return lax.dot_general(a, b, (((1,), (1,)), ((), ())), preferred_element_type=F32)


def _split_bf16(x):
    hi = x.astype(BF16)
    lo = (x - hi.astype(F32)).astype(BF16)
    return hi, lo


def _pack_rows(x):
    w = x.shape[1] // 2
    bits = pltpu.bitcast(x.astype(BF16).astype(F32), U32)
    return (bits[:, w:] & jnp.uint32(0xFFFF0000)) | (bits[:, :w] >> 16)


def _unpack_rows(p):
    lo = pltpu.bitcast(p << 16, F32)
    hi = pltpu.bitcast(p & jnp.uint32(0xFFFF0000), F32)
    return lo, hi


def _rms_mod(x, g, sh, sc):
    ms = jnp.mean(x * x, axis=-1, keepdims=True)
    y = x * lax.rsqrt(ms + NORM_EPS) * g
    return y * (1.0 + sc) + sh


def _ada_kernel(c_ref, w_ref, b_ref, o_ref):
    c = c_ref[...]
    s = c * jax.nn.sigmoid(c)
    o_ref[0] = _dot(s.astype(BF16), w_ref[0].astype(BF16)) + b_ref[0]


def _ada_all(cc, w_ada, b_ada):
    depth, d, d6 = w_ada.shape
    tn = 1536
    return pl.pallas_call(
        _ada_kernel,
        grid=(depth, d6 // tn),
        in_specs=[
            pl.BlockSpec((MOD_ROWS, d), lambda i, j: (0, 0)),
            pl.BlockSpec((1, d, tn), lambda i, j: (i, 0, j)),
            pl.BlockSpec((1, 1, tn), lambda i, j: (i, 0, j)),
        ],
        out_specs=pl.BlockSpec((1, MOD_ROWS, tn), lambda i, j: (i, 0, j)),
        out_shape=jax.ShapeDtypeStruct((depth, MOD_ROWS, d6), F32),
        compiler_params=_cparams(("parallel", "parallel")),
        name="ada_mod",
    )(cc, w_ada, b_ada.reshape(depth, 1, d6))


def _mod_spec(layer, chunk, ctx_row):
    if ctx_row is None:
        return pl.BlockSpec((1, 1, D_MODEL), lambda b, t: ((layer * MOD_ROWS + b) * N_MOD + chunk, 0, 0))
    return pl.BlockSpec((1, 1, D_MODEL), lambda b, t: ((layer * MOD_ROWS + ctx_row) * N_MOD + chunk, 0, 0))


def _layer_row_spec(layer):
    return pl.BlockSpec((1, 1, D_MODEL), lambda b, t: (layer, 0, 0))


def _qkv_kernel(*refs, qk_norm, rope):
    x_ref, g_ref, sh_ref, sc_ref, w_ref = refs[:5]
    pos = 5
    if rope:
        cos_ref, sin_ref = refs[pos:pos + 2]
        pos += 2
    if qk_norm:
        gq_ref, gk_ref, bm_ref = refs[pos:pos + 3]
        pos += 3
    if rope:
        qp_ref, qr_ref, k_ref, v_ref = refs[pos:]
    else:
        qp_ref, k_ref, v_ref = refs[pos:]

    h = _rms_mod(x_ref[0], g_ref[0], sh_ref[0], sc_ref[0])
    res = _dot(h.astype(BF16), w_ref[...])
    half = LANES // 2
    n_q = N_HEADS * HEAD_DIM // LANES
    n_k = N_KV_HEADS * HEAD_DIM // LANES
    if rope:
        lane = lax.broadcasted_iota(I32, (1, LANES), 1)
        first_half = (lane & (HEAD_DIM // 2 - 1)) < (HEAD_DIM // 4)
    for cb in range(QKV_DIM // LANES):
        xs = res[:, cb * LANES:(cb + 1) * LANES]
        is_q = cb < n_q
        is_k = n_q <= cb < n_q + n_k
        if qk_norm and (is_q or is_k):
            hi, lo = _split_bf16(xs * xs)
            ms = (_dot(hi, bm_ref[...]) + _dot(lo, bm_ref[...])) * (1.0 / HEAD_DIM)
            gain = gq_ref[...] if is_q else gk_ref[...]
            xs = xs * lax.rsqrt(ms + NORM_EPS) * gain
        if is_q:
            xs = xs * (ATTN_SCALE * LOG2E)
        if rope and (is_q or is_k):
            rot = jnp.where(first_half, pltpu.roll(xs, LANES - HEAD_DIM // 4, 1), pltpu.roll(xs, HEAD_DIM // 4, 1))
            xr = xs * cos_ref[...] + rot * sin_ref[...]
        if is_q:
            hd = 2 * cb
            qp_ref[0, hd] = xs[:, :half].astype(BF16)
            qp_ref[0, hd + 1] = xs[:, half:].astype(BF16)
            if rope:
                qr_ref[0, hd] = xr[:, :half].astype(BF16)
                qr_ref[0, hd + 1] = xr[:, half:].astype(BF16)
        elif is_k:
            hd = 2 * (cb - n_q)
            kk = xr if rope else xs
            k_ref[0, hd] = kk[:, :half].astype(BF16)
            k_ref[0, hd + 1] = kk[:, half:].astype(BF16)
        else:
            hd = 2 * (cb - n_q - n_k)
            low = lax.broadcasted_iota(I32, (1, LANES), 1) < HEAD_DIM
            swapped = pltpu.roll(xs, half, 1)
            for par, (a, b) in enumerate(((xs, swapped), (swapped, xs))):
                v_aug = jnp.concatenate([jnp.where(low, a, 1.0), jnp.where(low, 1.0, b)], axis=1)
                v_ref[0, hd + par] = v_aug.astype(BF16)


def _qkv(x, modr, g_rows, w_bf, layer, ctx_row, tm, rope_tabs=None, qk_gains=None):
    b, n, d = x.shape
    rope = rope_tabs is not None
    qk_norm = qk_gains is not None
    args = [x, g_rows, modr, modr, w_bf]
    in_specs = [
        pl.BlockSpec((1, tm, d), lambda bb, t: (bb, t, 0)),
        _layer_row_spec(layer),
        _mod_spec(layer, 0, ctx_row),
        _mod_spec(layer, 1, ctx_row),
        pl.BlockSpec((d, QKV_DIM), lambda bb, t: (0, 0)),
    ]
    if rope:
        args += list(rope_tabs)
        in_specs += [pl.BlockSpec((tm, LANES), lambda bb, t: (t, 0))] * 2
    if qk_norm:
        args += list(qk_gains)
        in_specs += [pl.BlockSpec((1, LANES), lambda bb, t: (0, 0))] * 2
        in_specs += [pl.BlockSpec((LANES, LANES), lambda bb, t: (0, 0))]
    q_shape = jax.ShapeDtypeStruct((b, N_HEADS, n, HEAD_DIM), BF16)
    kv_shape = jax.ShapeDtypeStruct((b, N_KV_HEADS, n, HEAD_DIM), BF16)
    q_spec = pl.BlockSpec((1, N_HEADS, tm, HEAD_DIM), lambda bb, t: (bb, 0, t, 0))
    kv_spec = pl.BlockSpec((1, N_KV_HEADS, tm, HEAD_DIM), lambda bb, t: (bb, 0, t, 0))
    v_shape = jax.ShapeDtypeStruct((b, N_KV_HEADS, n, 2 * LANES), BF16)
    v_spec = pl.BlockSpec((1, N_KV_HEADS, tm, 2 * LANES), lambda bb, t: (bb, 0, t, 0))
    if rope:
        out_shape, out_specs = [q_shape, q_shape, kv_shape, v_shape], [q_spec, q_spec, kv_spec, v_spec]
    else:
        out_shape, out_specs = [q_shape, kv_shape, v_shape], [q_spec, kv_spec, v_spec]
    return pl.pallas_call(
        functools.partial(_qkv_kernel, qk_norm=qk_norm, rope=rope),
        grid=(b, n // tm),
        in_specs=in_specs,
        out_specs=out_specs,
        out_shape=out_shape,
        compiler_params=_cparams(("parallel", "parallel")),
        name="qkv_rope" if rope else "qkv",
    )(*args)


def _attend(pieces, m_prev=None, acc_prev=None):
    m = m_prev
    for s, _ in pieces:
        m_s = jnp.max(s, axis=-1, keepdims=True)
        m = m_s if m is None else jnp.maximum(m, m_s)
    acc = None if acc_prev is None else jnp.exp2(m_prev - m) * acc_prev
    for s, v_aug in pieces:
        pv = _dot(jnp.exp2(s - m).astype(BF16), v_aug)
        acc = pv if acc is None else acc + pv
    return m, acc


def _head_pair_outputs(m, acc, rows_per_head, sink_ref=None, kv_head=None):
    low = lax.broadcasted_iota(I32, (1, LANES), 1) < HEAD_DIM
    blocks = []
    for pair in range(Q_PER_KV // 2):
        forms = []
        for par in range(2):
            g = 2 * pair + par
            rows = slice(g * rows_per_head, (g + 1) * rows_per_head)
            num = acc[rows, par * LANES:(par + 1) * LANES]
            den = acc[rows, (1 - par) * LANES:(2 - par) * LANES]
            if sink_ref is not None:
                sink = sink_ref[kv_head * Q_PER_KV + g] * LOG2E
                m2 = jnp.maximum(m[rows], sink)
                scale = jnp.exp2(m[rows] - m2)
                num = num * scale
                den = den * scale + jnp.exp2(sink - m2)
            forms.append(num / den)
        blocks.append(jnp.where(low, forms[0], forms[1]))
    return blocks


def _attn_kernel(*refs, mode, key_chunk):
    if mode == 0:
        qp_ref, qr_ref, k_ref, v_ref, kc_ref, vc_ref, sink_ref, o_ref = refs
    elif mode == 1:
        qp_ref, qr_ref, k_ref, v_ref, kc_ref, vc_ref, o_ref = refs
    else:
        qp_ref, k_ref, v_ref, kc_ref, vc_ref, t2_ref, o_ref = refs
    qb = pl.program_id(1)
    n = k_ref.shape[2]
    rows = Q_PER_KV * BLOCK_Q

    if mode == 0:
        span = BLOCK_Q + 2 * WINDOW
        kstart = pl.multiple_of(jnp.clip(qb * BLOCK_Q - WINDOW, 0, n - span), BLOCK_Q)
        qpos = qb * BLOCK_Q + (lax.broadcasted_iota(I32, (rows, span), 0) & (BLOCK_Q - 1))
        kpos = kstart + lax.broadcasted_iota(I32, (rows, span), 1)
        band = jnp.abs(kpos - qpos) <= WINDOW
    elif mode == 2:
        rows_grid = n // GRID_W
        r0 = qb * (BLOCK_Q // GRID_W)
        wr = jnp.clip(r0 - NA_WIN_H // 2, 0, rows_grid - NA_KEY_ROWS)
        kstart = pl.multiple_of(wr * GRID_W, GRID_W)
        lane = lax.broadcasted_iota(I32, (GRID_W, LANES), 1)

    pieces = []
    for h in range(N_KV_HEADS):
        qp = qp_ref[0, h * Q_PER_KV:(h + 1) * Q_PER_KV].reshape(rows, HEAD_DIM)
        ctx_piece = (_dot_nt(qp, kc_ref[0, h]), vc_ref[0, h])
        if mode == 0:
            qr = qr_ref[0, h * Q_PER_KV:(h + 1) * Q_PER_KV].reshape(rows, HEAD_DIM)
            s = jnp.where(band, _dot_nt(qr, k_ref[0, h, pl.ds(kstart, span), :]), NEG_INF)
            m, acc = _attend([(s, v_ref[0, h, pl.ds(kstart, span), :]), ctx_piece])
            pieces += _head_pair_outputs(m, acc, BLOCK_Q, sink_ref, h)
        elif mode == 1:
            qr = qr_ref[0, h * Q_PER_KV:(h + 1) * Q_PER_KV].reshape(rows, HEAD_DIM)
            n_chunks = n // key_chunk
            m, acc = None, None
            for c in range(n_chunks):
                keys = slice(c * key_chunk, (c + 1) * key_chunk)
                step = [(_dot_nt(qr, k_ref[0, h, keys, :]), v_ref[0, h, keys, :])]
                if c == n_chunks - 1:
                    step.append(ctx_piece)
                m, acc = _attend(step, m, acc)
            pieces += _head_pair_outputs(None, acc, BLOCK_Q)
        else:
            n_keys = NA_KEY_ROWS * GRID_W
            s = _dot_nt(qp, k_ref[0, h, pl.ds(kstart, n_keys), :])
            s_ctx, v_ctx = ctx_piece
            p_lat, p_ctx = [], []
            for g in range(Q_PER_KV):
                for a in range(BLOCK_Q // GRID_W):
                    r = r0 + a
                    rs = jnp.clip(r - NA_WIN_H // 2, 0, rows_grid - NA_WIN_H)
                    tiles = []
                    for j in range(n_keys // LANES):
                        kr0 = wr + 2 * j
                        tab = t2_ref[h * Q_PER_KV + g, kr0 - r + NA_TABLE // 2]
                        pen0 = jnp.where((kr0 >= rs) & (kr0 < rs + NA_WIN_H), 0.0, NEG_INF)
                        pen1 = jnp.where((kr0 + 1 >= rs) & (kr0 + 1 < rs + NA_WIN_H), 0.0, NEG_INF)
                        tiles.append(tab + jnp.where(lane < GRID_W, pen0, pen1))
                    qrows = slice(g * BLOCK_Q + a * GRID_W, g * BLOCK_Q + (a + 1) * GRID_W)
                    sb = s[qrows] + jnp.concatenate(tiles, axis=1)
                    sc = s_ctx[qrows]
                    m = jnp.maximum(jnp.max(sb, axis=-1, keepdims=True), jnp.max(sc, axis=-1, keepdims=True))
                    p_lat.append(jnp.exp2(sb - m).astype(BF16))
                    p_ctx.append(jnp.exp2(sc - m).astype(BF16))
            acc = (_dot(jnp.concatenate(p_lat, axis=0), v_ref[0, h, pl.ds(kstart, n_keys), :])
                   + _dot(jnp.concatenate(p_ctx, axis=0), v_ctx))
            pieces += _head_pair_outputs(None, acc, BLOCK_Q)
    o_ref[0] = jnp.concatenate(pieces, axis=-1).astype(o_ref.dtype)


def _attention(mode, qp, qr, k, v, kc, vc, extra):
    b, _, n, _ = qp.shape
    c = kc.shape[2]
    q_spec = pl.BlockSpec((1, N_HEADS, BLOCK_Q, HEAD_DIM), lambda bb, i: (bb, 0, i, 0))
    k_spec = pl.BlockSpec((1, N_KV_HEADS, n, HEAD_DIM), lambda bb, i: (bb, 0, 0, 0))
    v_spec = pl.BlockSpec((1, N_KV_HEADS, n, 2 * LANES), lambda bb, i: (bb, 0, 0, 0))
    kc_spec = pl.BlockSpec((1, N_KV_HEADS, c, HEAD_DIM), lambda bb, i: (bb, 0, 0, 0))
    vc_spec = pl.BlockSpec((1, N_KV_HEADS, c, 2 * LANES), lambda bb, i: (bb, 0, 0, 0))
    if mode == 2:
        args = [qp, k, v, kc, vc, extra]
        in_specs = [q_spec, k_spec, v_spec, kc_spec, vc_spec,
                    pl.BlockSpec(extra.shape, lambda bb, i: (0, 0, 0, 0))]
    else:
        args = [qp, qr, k, v, kc, vc]
        in_specs = [q_spec, q_spec, k_spec, v_spec, kc_spec, vc_spec]
        if mode == 0:
            args.append(extra)
            in_specs.append(pl.BlockSpec(memory_space=pltpu.SMEM))
    return pl.pallas_call(
        functools.partial(_attn_kernel, mode=mode, key_chunk=512),
        grid=(b, n // BLOCK_Q),
        in_specs=in_specs,
        out_specs=pl.BlockSpec((1, BLOCK_Q, D_MODEL), lambda bb, i: (bb, i, 0)),
        out_shape=jax.ShapeDtypeStruct((b, n, D_MODEL), BF16),
        compiler_params=_cparams(("parallel", "arbitrary")),
        name=("attn_window", "attn_global", "attn_na")[mode],
    )(*args)


def _ctx_attn_kernel(*refs, has_sink):
    if has_sink:
        q_ref, k_ref, v_ref, sink_ref, o_ref = refs
    else:
        q_ref, k_ref, v_ref, o_ref = refs
    c = k_ref.shape[2]
    rows = Q_PER_KV * c
    pieces = []
    for h in range(N_KV_HEADS):
        q = q_ref[0, h * Q_PER_KV:(h + 1) * Q_PER_KV].reshape(rows, HEAD_DIM)
        m, acc = _attend([(_dot_nt(q, k_ref[0, h]), v_ref[0, h])])
        if has_sink:
            pieces += _head_pair_outputs(m, acc, c, sink_ref, h)
        else:
            pieces += _head_pair_outputs(None, acc, c)
    o_ref[0] = jnp.concatenate(pieces, axis=-1).astype(o_ref.dtype)


def _ctx_attention(qc, kc, vc, sink):
    b, _, c, _ = qc.shape
    args = [qc, kc, vc]
    in_specs = [pl.BlockSpec((1, N_HEADS, c, HEAD_DIM), lambda bb: (bb, 0, 0, 0)),
                pl.BlockSpec((1, N_KV_HEADS, c, HEAD_DIM), lambda bb: (bb, 0, 0, 0)),
                pl.BlockSpec((1, N_KV_HEADS, c, 2 * LANES), lambda bb: (bb, 0, 0, 0))]
    if sink is not None:
        args.append(sink)
        in_specs.append(pl.BlockSpec(memory_space=pltpu.SMEM))
    return pl.pallas_call(
        functools.partial(_ctx_attn_kernel, has_sink=sink is not None),
        grid=(b,),
        in_specs=in_specs,
        out_specs=pl.BlockSpec((1, c, D_MODEL), lambda bb: (bb, 0, 0)),
        out_shape=jax.ShapeDtypeStruct((b, c, D_MODEL), BF16),
        compiler_params=_cparams(("parallel",)),
        name="attn_ctx",
    )(*args)


def _post_attn_kernel(o_ref, x_ref, wo_ref, gt_ref, g_ref, sh_ref, sc_ref, wrh_ref, wrl_ref, br_ref,
                      xn_ref, hf_ref, eid_ref, wcol_ref, cnt_ref):
    xn = x_ref[0] + gt_ref[0] * _dot(o_ref[0], wo_ref[...])
    xn_ref[0] = xn
    hf = _rms_mod(xn, g_ref[0], sh_ref[0], sc_ref[0])
    hf_ref[0] = _pack_rows(hf)
    tm = hf.shape[0]

    hh, hl = _split_bf16(hf)
    lg = _dot(hh, wrh_ref[...]) + _dot(hl, wrh_ref[...]) + _dot(hh, wrl_ref[...]) + br_ref[...]
    lgt = lg.T

    gl = lgt[0:N_GROUPS]
    ge = jnp.exp(gl - jnp.max(gl, axis=0, keepdims=True))
    pg = ge / jnp.sum(ge, axis=0, keepdims=True)
    p_top = jnp.max(pg, axis=0, keepdims=True)
    grow = lax.broadcasted_iota(I32, (N_GROUPS, tm), 0)
    grp = jnp.min(jnp.where(pg == p_top, grow, N_GROUPS), axis=0, keepdims=True)

    el = lgt[8:8 + EXPERTS_PER_GROUP]
    for gi in range(1, N_GROUPS):
        el = jnp.where(grp == gi, lgt[8 + gi * EXPERTS_PER_GROUP:8 + (gi + 1) * EXPERTS_PER_GROUP], el)
    ee = jnp.exp(el - jnp.max(el, axis=0, keepdims=True))
    pe = ee / jnp.sum(ee, axis=0, keepdims=True)
    erow = lax.broadcasted_iota(I32, (EXPERTS_PER_GROUP, tm), 0)
    p1 = jnp.max(pe, axis=0, keepdims=True)
    i1 = jnp.min(jnp.where(pe == p1, erow, EXPERTS_PER_GROUP), axis=0, keepdims=True)
    pe2 = jnp.where(erow == i1, -1.0, pe)
    p2 = jnp.max(pe2, axis=0, keepdims=True)
    i2 = jnp.min(jnp.where(pe2 == p2, erow, EXPERTS_PER_GROUP), axis=0, keepdims=True)
    den = p1 + p2
    w1 = p_top * p1 / den
    w2 = p_top * p2 / den
    e1 = grp * EXPERTS_PER_GROUP + i1
    e2 = grp * EXPERTS_PER_GROUP + i2

    eid_ref[0:1, :] = e1
    eid_ref[1:2, :] = e2
    wrows = jnp.concatenate([w1, w2, jnp.zeros((ROUTER_COLS - 2, tm), F32)], axis=0)
    wcol_ref[...] = wrows.T
    xrow = lax.broadcasted_iota(I32, (N_EXPERTS, tm), 0)
    hits = (xrow == e1).astype(F32) + (xrow == e2).astype(F32)
    cnt_ref[0] = jnp.sum(hits, axis=1, keepdims=True).astype(I32)


def _post_attn(o, x, wo_bf, modr, g_rows, router, layer, ctx_row, tm):
    b, n, d = x.shape
    nt = n // tm
    t_all = b * n
    wrh, wrl, br = router
    tile = lambda bb, t: (bb, t, 0)
    const2 = lambda bb, t: (0, 0)
    return pl.pallas_call(
        _post_attn_kernel,
        grid=(b, nt),
        in_specs=[
            pl.BlockSpec((1, tm, d), tile),
            pl.BlockSpec((1, tm, d), tile),
            pl.BlockSpec((d, d), const2),
            _mod_spec(layer, 2, ctx_row),
            _layer_row_spec(layer),
            _mod_spec(layer, 3, ctx_row),
            _mod_spec(layer, 4, ctx_row),
            pl.BlockSpec((d, ROUTER_COLS), const2),
            pl.BlockSpec((d, ROUTER_COLS), const2),
            pl.BlockSpec((1, ROUTER_COLS), const2),
        ],
        out_specs=[
            pl.BlockSpec((1, tm, d), tile),
            pl.BlockSpec((1, tm, d // 2), tile),
            pl.BlockSpec((2, tm), lambda bb, t: (0, bb * nt + t)),
            pl.BlockSpec((tm, ROUTER_COLS), lambda bb, t: (bb * nt + t, 0)),
            pl.BlockSpec((1, N_EXPERTS, 1), lambda bb, t: (bb * nt + t, 0, 0)),
        ],
        out_shape=[
            jax.ShapeDtypeStruct((b, n, d), F32),
            jax.ShapeDtypeStruct((b, n, d // 2), U32),
            jax.ShapeDtypeStruct((2, t_all), I32),
            jax.ShapeDtypeStruct((t_all, ROUTER_COLS), F32),
            jax.ShapeDtypeStruct((b * nt, N_EXPERTS, 1), I32),
        ],
        compiler_params=_cparams(("parallel", "parallel")),
        name="post_attn_router",
    )(o, x, wo_bf, modr, g_rows, modr, modr, wrh, wrl, br)


def _dest_kernel(eid_ref, base_ref, tri_ref, dest_ref):
    tm = eid_ref.shape[1]
    xrow = lax.broadcasted_iota(I32, (N_EXPERTS, tm), 0)
    oh0 = xrow == eid_ref[0:1, :]
    oh1 = xrow == eid_ref[1:2, :]
    both = oh0.astype(F32) + oh1.astype(F32)
    incl = _dot(both.astype(BF16), tri_ref[...])
    before = incl - both + base_ref[0].astype(F32)
    d0 = jnp.sum(jnp.where(oh0, before, 0.0), axis=0, keepdims=True)
    d1 = jnp.sum(jnp.where(oh1, before, 0.0), axis=0, keepdims=True)
    dest_ref[0, 0:1, :] = d0.astype(I32)
    dest_ref[0, 1:2, :] = d1.astype(I32)


def _dest_rows(eid, base, tri):
    tm = tri.shape[0]
    nt = eid.shape[1] // tm
    return pl.pallas_call(
        _dest_kernel,
        grid=(nt,),
        in_specs=[
            pl.BlockSpec((2, tm), lambda i: (0, i)),
            pl.BlockSpec((1, N_EXPERTS, 1), lambda i: (i, 0, 0)),
            pl.BlockSpec((tm, tm), lambda i: (0, 0)),
        ],
        out_specs=pl.BlockSpec((1, 2, tm), lambda i: (i, 0, 0)),
        out_shape=jax.ShapeDtypeStruct((nt, 2, tm), I32),
        compiler_params=_cparams(("parallel",)),
        name="moe_dest",
    )(eid, base, tri)


def _scatter_kernel(*refs, tiles):
    dest_ref = refs[0]
    hf_refs = refs[1:1 + len(tiles)]
    xs_ref, sem = refs[1 + len(tiles):]
    tm = hf_refs[0].shape[0]
    i = pl.program_id(0)

    def scatter_tile(hf_ref):
        def issue(t8, carry):
            base = pl.multiple_of(t8 * ROW_UNROLL, ROW_UNROLL)
            rows = hf_ref.at[pl.ds(base, ROW_UNROLL)]
            for u in range(ROW_UNROLL):
                for k in range(2):
                    d = dest_ref.at[0, k, pl.ds(base, ROW_UNROLL)][u]
                    pltpu.make_async_copy(rows.at[pl.ds(u, 1)], xs_ref.at[pl.ds(d, 1)], sem).start(priority=k)
            return carry

        lax.fori_loop(0, tm // ROW_UNROLL, issue, 0)
        for k in range(2):
            pltpu.make_async_copy(hf_ref, xs_ref.at[pl.ds(0, tm)], sem).wait()

    first = 0
    for hf_ref, nt in zip(hf_refs, tiles):
        pl.when((i >= first) & (i < first + nt))(functools.partial(scatter_tile, hf_ref))
        first += nt


def _scatter_rows(dest, hf_streams, n_rows):
    nt_all, _, tm = dest.shape
    d = hf_streams[0].shape[1]
    tiles = tuple(h.shape[0] // tm for h in hf_streams)
    assert sum(tiles) == nt_all
    in_specs = [pl.BlockSpec((1, 2, tm), lambda i: (i, 0, 0), memory_space=pltpu.SMEM)]
    first = 0
    for nt in tiles:
        in_specs.append(pl.BlockSpec((tm, d), lambda i, first=first, nt=nt: (jnp.clip(i - first, 0, nt - 1), 0)))
        first += nt
    return pl.pallas_call(
        functools.partial(_scatter_kernel, tiles=tiles),
        grid=(nt_all,),
        in_specs=in_specs,
        out_specs=pl.BlockSpec(memory_space=pl.ANY),
        out_shape=jax.ShapeDtypeStruct((n_rows, d), hf_streams[0].dtype),
        scratch_shapes=[pltpu.SemaphoreType.DMA(())],
        compiler_params=_cparams(("arbitrary",)),
        name="moe_scatter",
    )(dest, *hf_streams)


def _expert_kernel(ve_ref, vblk_ref, lo_ref, hi_ref, x_ref, wg_ref, wu_ref, wd_ref, y_ref, wg_s, wu_s, wd_s):
    v = pl.program_id(0)
    pv = jnp.maximum(v - 1, 0)
    new_expert = (v == 0) | (ve_ref[v] != ve_ref[pv])
    first_visit = (v == 0) | (vblk_ref[v] != vblk_ref[pv])
    lo = lo_ref[v]
    hi = hi_ref[v]

    @pl.when(new_expert)
    def _():
        wg_s[...] = wg_ref[0, 0].astype(BF16)
        wu_s[...] = wu_ref[0, 0].astype(BF16)
        wd_s[...] = wd_ref[0, 0].astype(BF16)

    @pl.when(hi > lo)
    def _():
        x_lo, x_hi = _unpack_rows(x_ref[...])
        x_lo = x_lo.astype(BF16)
        x_hi = x_hi.astype(BF16)
        half = x_lo.shape[1]
        gate = _dot(x_lo, wg_s[:half]) + _dot(x_hi, wg_s[half:])
        up = _dot(x_lo, wu_s[:half]) + _dot(x_hi, wu_s[half:])
        mid = gate * jax.nn.sigmoid(gate) * up
        y = _pack_rows(_dot(mid.astype(BF16), wd_s[...]))
        row = lax.broadcasted_iota(I32, (EXPERT_BLOCK, 1), 0)
        mine = (row >= lo) & (row < hi)

        @pl.when(first_visit)
        def _():
            y_ref[...] = jnp.where(mine, y, jnp.uint32(0))

        @pl.when(jnp.logical_not(first_visit))
        def _():
            y_ref[...] = jnp.where(mine, y, y_ref[...])


def _experts(visits, xs, w_gate, w_up, w_down, layer):
    n_rows, dp = xs.shape
    _, _, d, f = w_gate.shape
    assert d == 2 * dp
    ve, vblk, lo, hi = visits
    x_map = lambda v, ve_r, vb_r, lo_r, hi_r: (vb_r[v], 0)
    w_map = lambda v, ve_r, vb_r, lo_r, hi_r: (layer, ve_r[v], 0, 0)
    grid_spec = pltpu.PrefetchScalarGridSpec(
        num_scalar_prefetch=4,
        grid=(ve.shape[0],),
        in_specs=[
            pl.BlockSpec((EXPERT_BLOCK, dp), x_map),
            pl.BlockSpec((1, 1, d, f), w_map),
            pl.BlockSpec((1, 1, d, f), w_map),
            pl.BlockSpec((1, 1, f, d), w_map),
        ],
        out_specs=pl.BlockSpec((EXPERT_BLOCK, dp), x_map),
        scratch_shapes=[pltpu.VMEM((d, f), BF16), pltpu.VMEM((d, f), BF16), pltpu.VMEM((f, d), BF16)],
    )
    return pl.pallas_call(
        _expert_kernel,
        grid_spec=grid_spec,
        out_shape=jax.ShapeDtypeStruct((n_rows, dp), U32),
        compiler_params=_cparams(("arbitrary",)),
        name="moe_experts",
    )(ve, vblk, lo, hi, xs, w_gate, w_up, w_down)


def _combine_kernel(*refs, final_norm):
    if final_norm:
        dest_ref, dest_next_ref, xn_ref, wcol_ref, gt_ref, gfin_ref, y_ref, o_ref, ybuf, sems = refs
    else:
        dest_ref, dest_next_ref, xn_ref, wcol_ref, gt_ref, y_ref, o_ref, ybuf, sems = refs
    tm = xn_ref.shape[1]
    step = pl.program_id(0) * pl.num_programs(1) + pl.program_id(1)
    n_steps = pl.num_programs(0) * pl.num_programs(1)

    def gather_tile(d_ref, slot):
        def issue(t8, carry):
            base = pl.multiple_of(t8 * ROW_UNROLL, ROW_UNROLL)
            for k in range(2):
                rows = ybuf.at[slot, k, pl.ds(base, ROW_UNROLL)]
                for u in range(ROW_UNROLL):
                    d = d_ref.at[0, k, pl.ds(base, ROW_UNROLL)][u]
                    pltpu.make_async_copy(y_ref.at[pl.ds(d, 1)], rows.at[pl.ds(u, 1)], sems.at[slot]).start(priority=k)
            return carry

        lax.fori_loop(0, tm // ROW_UNROLL, issue, 0)

    def finish_tile(slot):
        for k in range(2):
            pltpu.make_async_copy(y_ref.at[pl.ds(0, tm)], ybuf.at[slot, k], sems.at[slot]).wait()
        w = wcol_ref[...]
        lo0, hi0 = _unpack_rows(ybuf[slot, 0])
        lo1, hi1 = _unpack_rows(ybuf[slot, 1])
        moe = jnp.concatenate([w[:, 0:1] * lo0 + w[:, 1:2] * lo1, w[:, 0:1] * hi0 + w[:, 1:2] * hi1], axis=1)
        out = xn_ref[0] + gt_ref[0] * moe
        if final_norm:
            ms = jnp.mean(out * out, axis=-1, keepdims=True)
            out = out * lax.rsqrt(ms + NORM_EPS) * gfin_ref[...]
        o_ref[0] = out

    pl.when(step == 0)(functools.partial(gather_tile, dest_ref, 0))
    for slot in range(2):
        @pl.when((step % 2 == slot) & (step + 1 < n_steps))
        def _(slot=slot):
            gather_tile(dest_next_ref, 1 - slot)

        pl.when(step % 2 == slot)(functools.partial(finish_tile, slot))


def _combine(dest, xn, wcol, modr, y, layer, ctx_row, g_final=None):
    b, n, d = xn.shape
    nt_all, _, tm = dest.shape
    nt = n // tm
    final_norm = g_final is not None
    args = [dest, dest, xn, wcol, modr]
    in_specs = [
        pl.BlockSpec((1, 2, tm), lambda bb, t: (bb * nt + t, 0, 0), memory_space=pltpu.SMEM),
        pl.BlockSpec((1, 2, tm), lambda bb, t: (jnp.minimum(bb * nt + t + 1, nt_all - 1), 0, 0),
                     memory_space=pltpu.SMEM),
        pl.BlockSpec((1, tm, d), lambda bb, t: (bb, t, 0)),
        pl.BlockSpec((tm, ROUTER_COLS), lambda bb, t: (bb * nt + t, 0)),
        _mod_spec(layer, 5, ctx_row),
    ]
    if final_norm:
        args.append(g_final.reshape(1, d))
        in_specs.append(pl.BlockSpec((1, d), lambda bb, t: (0, 0)))
    args.append(y)
    in_specs.append(pl.BlockSpec(memory_space=pl.ANY))
    return pl.pallas_call(
        functools.partial(_combine_kernel, final_norm=final_norm),
        grid=(b, nt),
        in_specs=in_specs,
        out_specs=pl.BlockSpec((1, tm, d), lambda bb, t: (bb, t, 0)),
        out_shape=jax.ShapeDtypeStruct((b, n, d), F32),
        scratch_shapes=[pltpu.VMEM((2, 2, tm, y.shape[1]), y.dtype), pltpu.SemaphoreType.DMA((2,))],
        compiler_params=_cparams(("arbitrary", "arbitrary")),
        name="moe_combine",
    )(*args)


def _rope_tables(n):
    t = np.arange(n)
    row = (t // GRID_W).astype(np.float32)
    col = (t % GRID_W).astype(np.float32)
    quarter = HEAD_DIM // 4
    inv = jnp.asarray(ROPE_THETA, F32) ** (-jnp.arange(quarter, dtype=F32) / quarter)
    ar = jnp.asarray(row)[:, None] * inv
    ac = jnp.asarray(col)[:, None] * inv
    ang = jnp.concatenate([ar, ar, ac, ac], axis=-1)
    ang = jnp.concatenate([ang, ang], axis=-1)
    sign = np.where((np.arange(LANES) % (HEAD_DIM // 2)) < quarter, -1.0, 1.0).astype(np.float32)
    return jnp.cos(ang), jnp.sin(ang) * sign


def _na_bias_tables(rpb):
    n_dr, n_dc = 2 * NA_WIN_H - 1, 2 * NA_WIN_W - 1
    blocked = jnp.full((N_HEADS, 2, n_dc), NEG_INF, F32)
    rows = jnp.concatenate([blocked, rpb.astype(F32) * LOG2E, blocked], axis=1)
    assert rows.shape[1] == n_dr + 4 == NA_TABLE + 1
    pair = jnp.stack([rows[:, 0:NA_TABLE], rows[:, 1:NA_TABLE + 1]], axis=2)
    c = np.arange(GRID_W)[:, None]
    kc = np.arange(GRID_W)[None, :]
    cs = np.clip(c - NA_WIN_W // 2, 0, GRID_W - NA_WIN_W)
    in_window = (kc >= cs) & (kc < cs + NA_WIN_W)
    out = jnp.full((N_HEADS, NA_TABLE, GRID_W, 2, GRID_W), NEG_INF, F32)
    for dc in range(-(NA_WIN_W - 1), NA_WIN_W):
        hit = ((kc - c) == dc) & in_window
        if hit.any():
            out = jnp.where(hit[None, None, :, None, :], pair[:, :, None, :, dc + NA_WIN_W - 1, None], out)
    return out.reshape(N_HEADS, NA_TABLE, GRID_W, LANES)


def _router_tables(w_group, b_group, w_router, b_router):
    d = w_group.shape[0]
    w = jnp.zeros((d, ROUTER_COLS), F32)
    w = w.at[:, 0:N_GROUPS].set(w_group).at[:, 8:8 + N_EXPERTS].set(w_router)
    bias = jnp.zeros((1, ROUTER_COLS), F32)
    bias = bias.at[0, 0:N_GROUPS].set(b_group).at[0, 8:8 + N_EXPERTS].set(b_router)
    hi = w.astype(BF16)
    lo = (w - hi.astype(F32)).astype(BF16)
    return hi, lo, bias


def _pick(table, onehot):
    return jnp.sum(jnp.where(onehot, table[None, :], 0), axis=1)


def _moe_layout(cnt_tiles, n_rows):
    cnt = cnt_tiles[:, :, 0]
    counts = jnp.sum(cnt, axis=0)
    ends = jnp.cumsum(counts)
    starts = ends - counts
    base = starts[None, :] + jnp.cumsum(cnt, axis=0) - cnt
    first_blk = starts // EXPERT_BLOCK
    n_blk = jnp.where(counts > 0, (ends - 1) // EXPERT_BLOCK - first_blk + 1, 0)
    v_end = jnp.cumsum(n_blk)
    v_off = v_end - n_blk
    n_visits = n_rows // EXPERT_BLOCK + N_EXPERTS
    v = jnp.arange(n_visits, dtype=I32)
    valid = v < v_end[-1]
    vv = jnp.minimum(v, v_end[-1] - 1)
    ve = jnp.sum((v_end[None, :] <= vv[:, None]).astype(I32), axis=1)
    onehot = ve[:, None] == jnp.arange(N_EXPERTS, dtype=I32)[None, :]
    vblk = _pick(first_blk, onehot) + vv - _pick(v_off, onehot)
    lo = jnp.clip(_pick(starts, onehot) - vblk * EXPERT_BLOCK, 0, EXPERT_BLOCK)
    hi = jnp.clip(_pick(ends, onehot) - vblk * EXPERT_BLOCK, 0, EXPERT_BLOCK)
    hi = jnp.where(valid, hi, lo)
    visits = tuple(a.astype(I32) for a in (ve, vblk, lo, hi))
    return base[:, :, None].astype(I32), visits


def kernel(x, c, ctx, c_ctx, w_ada, b_ada, g_attn, w_qkv, w_o, sink_a, gq_b, gk_b, rpb_c, g_ffn,
           w_group, b_group, w_router, b_router, w_gate, w_up, w_down, g_final):
    b, n, d = x.shape
    n_ctx = ctx.shape[1]
    depth = w_ada.shape[0]
    tm_tok = 512
    tm_ctx = 256
    assert b + 1 <= MOD_ROWS and d == D_MODEL and n % tm_tok == 0 and n_ctx % tm_ctx == 0
    assert (b * n_ctx) % tm_tok == 0
    ctx_row = b

    cc = jnp.zeros((MOD_ROWS, d), F32).at[:b].set(c).at[b].set(c_ctx)
    modr = _ada_all(cc, w_ada, b_ada).reshape(depth * MOD_ROWS * N_MOD, 1, d)
    g_attn_rows = g_attn.reshape(depth, 1, d)
    g_ffn_rows = g_ffn.reshape(depth, 1, d)
    rope_tabs = _rope_tables(n)
    head_sum = jnp.asarray(np.kron(np.eye(LANES // HEAD_DIM), np.ones((HEAD_DIM, HEAD_DIM))), BF16)
    tri = jnp.asarray(np.triu(np.ones((tm_tok, tm_tok))), BF16)

    xc = ctx
    for i in range(depth):
        m = i % N_MIXERS
        j = i // N_MIXERS
        last = i == depth - 1
        wqkv_bf = w_qkv[i].astype(BF16)
        wo_bf = w_o[i].astype(BF16)
        qk_gains = None
        if m == 1:
            qk_gains = (jnp.tile(gq_b[j], LANES // HEAD_DIM).reshape(1, LANES),
                        jnp.tile(gk_b[j], LANES // HEAD_DIM).reshape(1, LANES), head_sum)

        if m == 2:
            qp, k, v = _qkv(x, modr, g_attn_rows, wqkv_bf, i, None, tm_tok, None, qk_gains)
            qr = None
        else:
            qp, qr, k, v = _qkv(x, modr, g_attn_rows, wqkv_bf, i, None, tm_tok, rope_tabs, qk_gains)
        qc, kc, vc = _qkv(xc, modr, g_attn_rows, wqkv_bf, i, ctx_row, tm_ctx, None, qk_gains)
        sink = sink_a[j] if m == 0 else None
        extra = sink if m == 0 else (_na_bias_tables(rpb_c[j]) if m == 2 else None)
        o = _attention(m, qp, qr, k, v, kc, vc, extra)

        router = _router_tables(w_group[i], b_group[i], w_router[i], b_router[i])
        streams = [(o, x, None)]
        if not last:
            oc = _ctx_attention(qc, kc, vc, sink)
            streams.append((oc.reshape(1, b * n_ctx, d), xc.reshape(1, b * n_ctx, d), ctx_row))
        routed = [_post_attn(o_s, x_s, wo_bf, modr, g_ffn_rows, router, i, row, tm_tok) for (o_s, x_s, row) in streams]

        n_rows = 2 * sum(r[2].shape[1] for r in routed)
        assert n_rows % EXPERT_BLOCK == 0
        base, visits = _moe_layout(jnp.concatenate([r[4] for r in routed], axis=0), n_rows)
        dest = _dest_rows(jnp.concatenate([r[2] for r in routed], axis=1), base, tri)
        xs = _scatter_rows(dest, [r[1].reshape(-1, d // 2) for r in routed], n_rows)
        y = _experts(visits, xs, w_gate, w_up, w_down, i)
        nt_lat = routed[0][4].shape[0]
        x = _combine(dest[:nt_lat], routed[0][0], routed[0][3], modr, y, i, None, g_final if last else None)
        if not last:
            xc = _combine(dest[nt_lat:], routed[1][0], routed[1][3], modr, y, i, ctx_row).reshape(b, n_ctx, d)
    return x
```

```python
import functools

import numpy as np
import jax
import jax.numpy as jnp
from jax import lax
from jax.experimental import pallas as pl
from jax.experimental.pallas import tpu as pltpu

F32 = jnp.float32
BF16 = jnp.bfloat16
I32 = jnp.int32
U32 = jnp.uint32

D_MODEL = 1024
GRID_W = 64
N_MIXERS = 3
N_HEADS = 16
N_KV_HEADS = 4
HEAD_DIM = D_MODEL // N_HEADS
Q_PER_KV = N_HEADS // N_KV_HEADS
QKV_DIM = (N_HEADS + 2 * N_KV_HEADS) * HEAD_DIM
ATTN_SCALE = HEAD_DIM ** -0.5
ROPE_THETA = 10000.0
BLOCK_Q = 128
WINDOW = 128
NA_WIN_H = 8
NA_WIN_W = 16
N_GROUPS = 4
EXPERTS_PER_GROUP = 8
N_EXPERTS = N_GROUPS * EXPERTS_PER_GROUP
EXPERT_HIDDEN = D_MODEL // 2
NORM_EPS = 1e-6
NEG_INF = -1e30
LOG2E = 1.4426950408889634

LANES = 128
MOD_ROWS = 24
N_MOD = 6
EXPERT_BLOCK = 512
ROW_UNROLL = 8
ROUTER_COLS = LANES
NA_KEY_ROWS = 10
NA_TABLE = 18
VMEM_LIMIT = 56 * 1024 * 1024


def _cparams(sem, vmem=VMEM_LIMIT):
    return pltpu.CompilerParams(dimension_semantics=sem, vmem_limit_bytes=vmem)


def _dot(a, b):
    return jnp.dot(a, b, preferred_element_type=F32)


def _dot_nt(a, b):
    return lax.dot_general(a, b, (((1,), (1,)), ((), ())), preferred_element_type=F32)


def _split_bf16(x):
    hi = x.astype(BF16)
    lo = (x - hi.astype(F32)).astype(BF16)
    return hi, lo


def _pack_rows(x):
    w = x.shape[1] // 2
    bits = pltpu.bitcast(x.astype(BF16).astype(F32), U32)
    return (bits[:, w:] & jnp.uint32(0xFFFF0000)) | (bits[:, :w] >> 16)


def _unpack_rows(p):
    lo = pltpu.bitcast(p << 16, F32)
    hi = pltpu.bitcast(p & jnp.uint32(0xFFFF0000), F32)
    return lo, hi


def _rms_mod(x, g, sh, sc):
    ms = jnp.mean(x * x, axis=-1, keepdims=True)
    y = x * lax.rsqrt(ms + NORM_EPS) * g
    return y * (1.0 + sc) + sh


def _ada_kernel(c_ref, w_ref, b_ref, o_ref):
    c = c_ref[...]
    s = c * jax.nn.sigmoid(c)
    o_ref[0] = _dot(s.astype(BF16), w_ref[0].astype(BF16)) + b_ref[0]


def _ada_all(cc, w_ada, b_ada):
    depth, d, d6 = w_ada.shape
    tn = 1536
    return pl.pallas_call(
        _ada_kernel,
        grid=(depth, d6 // tn),
        in_specs=[
            pl.BlockSpec((MOD_ROWS, d), lambda i, j: (0, 0)),
            pl.BlockSpec((1, d, tn), lambda i, j: (i, 0, j)),
            pl.BlockSpec((1, 1, tn), lambda i, j: (i, 0, j)),
        ],
        out_specs=pl.BlockSpec((1, MOD_ROWS, tn), lambda i, j: (i, 0, j)),
        out_shape=jax.ShapeDtypeStruct((depth, MOD_ROWS, d6), F32),
        compiler_params=_cparams(("parallel", "parallel")),
        name="ada_mod",
    )(cc, w_ada, b_ada.reshape(depth, 1, d6))


def _mod_spec(layer, chunk, ctx_row):
    if ctx_row is None:
        return pl.BlockSpec((1, 1, D_MODEL), lambda b, t: ((layer * MOD_ROWS + b) * N_MOD + chunk, 0, 0))
    return pl.BlockSpec((1, 1, D_MODEL), lambda b, t: ((layer * MOD_ROWS + ctx_row) * N_MOD + chunk, 0, 0))


def _layer_row_spec(layer):
    return pl.BlockSpec((1, 1, D_MODEL), lambda b, t: (layer, 0, 0))


def _qkv_kernel(*refs, qk_norm, rope):
    x_ref, g_ref, sh_ref, sc_ref, w_ref = refs[:5]
    pos = 5
    if rope:
        cos_ref, sin_ref = refs[pos:pos + 2]
        pos += 2
    if qk_norm:
        gq_ref, gk_ref, bm_ref = refs[pos:pos + 3]
        pos += 3
    if rope:
        qp_ref, qr_ref, k_ref, v_ref = refs[pos:]
    else:
        qp_ref, k_ref, v_ref = refs[pos:]

    h = _rms_mod(x_ref[0], g_ref[0], sh_ref[0], sc_ref[0])
    res = _dot(h.astype(BF16), w_ref[...])
    half = LANES // 2
    n_q = N_HEADS * HEAD_DIM // LANES
    n_k = N_KV_HEADS * HEAD_DIM // LANES
    if rope:
        lane = lax.broadcasted_iota(I32, (1, LANES), 1)
        first_half = (lane & (HEAD_DIM // 2 - 1)) < (HEAD_DIM // 4)
    for cb in range(QKV_DIM // LANES):
        xs = res[:, cb * LANES:(cb + 1) * LANES]
        is_q = cb < n_q
        is_k = n_q <= cb < n_q + n_k
        if qk_norm and (is_q or is_k):
            hi, lo = _split_bf16(xs * xs)
            ms = (_dot(hi, bm_ref[...]) + _dot(lo, bm_ref[...])) * (1.0 / HEAD_DIM)
            gain = gq_ref[...] if is_q else gk_ref[...]
            xs = xs * lax.rsqrt(ms + NORM_EPS) * gain
        if is_q:
            xs = xs * (ATTN_SCALE * LOG2E)
        if rope and (is_q or is_k):
            rot = jnp.where(first_half, pltpu.roll(xs, LANES - HEAD_DIM // 4, 1), pltpu.roll(xs, HEAD_DIM // 4, 1))
            xr = xs * cos_ref[...] + rot * sin_ref[...]
        if is_q:
            hd = 2 * cb
            qp_ref[0, hd] = xs[:, :half].astype(BF16)
            qp_ref[0, hd + 1] = xs[:, half:].astype(BF16)
            if rope:
                qr_ref[0, hd] = xr[:, :half].astype(BF16)
                qr_ref[0, hd + 1] = xr[:, half:].astype(BF16)
        elif is_k:
            hd = 2 * (cb - n_q)
            kk = xr if rope else xs
            k_ref[0, hd] = kk[:, :half].astype(BF16)
            k_ref[0, hd + 1] = kk[:, half:].astype(BF16)
        else:
            hd = 2 * (cb - n_q - n_k)
            low = lax.broadcasted_iota(I32, (1, LANES), 1) < HEAD_DIM
            swapped = pltpu.roll(xs, half, 1)
            for par, (a, b) in enumerate(((xs, swapped), (swapped, xs))):
                v_aug = jnp.concatenate([jnp.where(low, a, 1.0), jnp.where(low, 1.0, b)], axis=1)
                v_ref[0, hd + par] = v_aug.astype(BF16)


def _qkv(x, modr, g_rows, w_bf, layer, ctx_row, tm, rope_tabs=None, qk_gains=None):
    b, n, d = x.shape
    rope = rope_tabs is not None
    qk_norm = qk_gains is not None
    args = [x, g_rows, modr, modr, w_bf]
    in_specs = [
        pl.BlockSpec((1, tm, d), lambda bb, t: (bb, t, 0)),
        _layer_row_spec(layer),
        _mod_spec(layer, 0, ctx_row),
        _mod_spec(layer, 1, ctx_row),
        pl.BlockSpec((d, QKV_DIM), lambda bb, t: (0, 0)),
    ]
    if rope:
        args += list(rope_tabs)
        in_specs += [pl.BlockSpec((tm, LANES), lambda bb, t: (t, 0))] * 2
    if qk_norm:
        args += list(qk_gains)
        in_specs += [pl.BlockSpec((1, LANES), lambda bb, t: (0, 0))] * 2
        in_specs += [pl.BlockSpec((LANES, LANES), lambda bb, t: (0, 0))]
    q_shape = jax.ShapeDtypeStruct((b, N_HEADS, n, HEAD_DIM), BF16)
    kv_shape = jax.ShapeDtypeStruct((b, N_KV_HEADS, n, HEAD_DIM), BF16)
    q_spec = pl.BlockSpec((1, N_HEADS, tm, HEAD_DIM), lambda bb, t: (bb, 0, t, 0))
    kv_spec = pl.BlockSpec((1, N_KV_HEADS, tm, HEAD_DIM), lambda bb, t: (bb, 0, t, 0))
    v_shape = jax.ShapeDtypeStruct((b, N_KV_HEADS, n, 2 * LANES), BF16)
    v_spec = pl.BlockSpec((1, N_KV_HEADS, tm, 2 * LANES), lambda bb, t: (bb, 0, t, 0))
    if rope:
        out_shape, out_specs = [q_shape, q_shape, kv_shape, v_shape], [q_spec, q_spec, kv_spec, v_spec]
    else:
        out_shape, out_specs = [q_shape, kv_shape, v_shape], [q_spec, kv_spec, v_spec]
    return pl.pallas_call(
        functools.partial(_qkv_kernel, qk_norm=qk_norm, rope=rope),
        grid=(b, n // tm),
        in_specs=in_specs,
        out_specs=out_specs,
        out_shape=out_shape,
        compiler_params=_cparams(("parallel", "parallel")),
        name="qkv_rope" if rope else "qkv",
    )(*args)


def _attend(pieces, m_prev=None, acc_prev=None):
    m = m_prev
    for s, _ in pieces:
        m_s = jnp.max(s, axis=-1, keepdims=True)
        m = m_s if m is None else jnp.maximum(m, m_s)
    acc = None if acc_prev is None else jnp.exp2(m_prev - m) * acc_prev
    for s, v_aug in pieces:
        pv = _dot(jnp.exp2(s - m).astype(BF16), v_aug)
        acc = pv if acc is None else acc + pv
    return m, acc


def _head_pair_outputs(m, acc, rows_per_head, sink_ref=None, kv_head=None):
    low = lax.broadcasted_iota(I32, (1, LANES), 1) < HEAD_DIM
    blocks = []
    for pair in range(Q_PER_KV // 2):
        forms = []
        for par in range(2):
            g = 2 * pair + par
            rows = slice(g * rows_per_head, (g + 1) * rows_per_head)
            num = acc[rows, par * LANES:(par + 1) * LANES]
            den = acc[rows, (1 - par) * LANES:(2 - par) * LANES]
            if sink_ref is not None:
                sink = sink_ref[kv_head * Q_PER_KV + g] * LOG2E
                m2 = jnp.maximum(m[rows], sink)
                scale = jnp.exp2(m[rows] - m2)
                num = num * scale
                den = den * scale + jnp.exp2(sink - m2)
            forms.append(num / den)
        blocks.append(jnp.where(low, forms[0], forms[1]))
    return blocks


def _attn_kernel(*refs, mode, key_chunk):
    if mode == 0:
        qp_ref, qr_ref, k_ref, v_ref, kc_ref, vc_ref, sink_ref, o_ref = refs
    elif mode == 1:
        qp_ref, qr_ref, k_ref, v_ref, kc_ref, vc_ref, o_ref = refs
    else:
        qp_ref, k_ref, v_ref, kc_ref, vc_ref, t2_ref, o_ref = refs
    qb = pl.program_id(1)
    n = k_ref.shape[2]
    rows = Q_PER_KV * BLOCK_Q

    if mode == 0:
        span = BLOCK_Q + 2 * WINDOW
        kstart = pl.multiple_of(jnp.clip(qb * BLOCK_Q - WINDOW, 0, n - span), BLOCK_Q)
        qpos = qb * BLOCK_Q + (lax.broadcasted_iota(I32, (rows, span), 0) & (BLOCK_Q - 1))
        kpos = kstart + lax.broadcasted_iota(I32, (rows, span), 1)
        band = jnp.abs(kpos - qpos) <= WINDOW
    elif mode == 2:
        rows_grid = n // GRID_W
        r0 = qb * (BLOCK_Q // GRID_W)
        wr = jnp.clip(r0 - NA_WIN_H // 2, 0, rows_grid - NA_KEY_ROWS)
        kstart = pl.multiple_of(wr * GRID_W, GRID_W)
        lane = lax.broadcasted_iota(I32, (GRID_W, LANES), 1)

    pieces = []
    for h in range(N_KV_HEADS):
        qp = qp_ref[0, h * Q_PER_KV:(h + 1) * Q_PER_KV].reshape(rows, HEAD_DIM)
        ctx_piece = (_dot_nt(qp, kc_ref[0, h]), vc_ref[0, h])
        if mode == 0:
            qr = qr_ref[0, h * Q_PER_KV:(h + 1) * Q_PER_KV].reshape(rows, HEAD_DIM)
            s = jnp.where(band, _dot_nt(qr, k_ref[0, h, pl.ds(kstart, span), :]), NEG_INF)
            m, acc = _attend([(s, v_ref[0, h, pl.ds(kstart, span), :]), ctx_piece])
            pieces += _head_pair_outputs(m, acc, BLOCK_Q, sink_ref, h)
        elif mode == 1:
            qr = qr_ref[0, h * Q_PER_KV:(h + 1) * Q_PER_KV].reshape(rows, HEAD_DIM)
            n_chunks = n // key_chunk
            m, acc = None, None
            for c in range(n_chunks):
                keys = slice(c * key_chunk, (c + 1) * key_chunk)
                step = [(_dot_nt(qr, k_ref[0, h, keys, :]), v_ref[0, h, keys, :])]
                if c == n_chunks - 1:
                    step.append(ctx_piece)
                m, acc = _attend(step, m, acc)
            pieces += _head_pair_outputs(None, acc, BLOCK_Q)
        else:
            n_keys = NA_KEY_ROWS * GRID_W
            s = _dot_nt(qp, k_ref[0, h, pl.ds(kstart, n_keys), :])
            s_ctx, v_ctx = ctx_piece
            p_lat, p_ctx = [], []
            for g in range(Q_PER_KV):
                for a in range(BLOCK_Q // GRID_W):
                    r = r0 + a
                    rs = jnp.clip(r - NA_WIN_H // 2, 0, rows_grid - NA_WIN_H)
                    tiles = []
                    for j in range(n_keys // LANES):
                        kr0 = wr + 2 * j
                        tab = t2_ref[h * Q_PER_KV + g, kr0 - r + NA_TABLE // 2]
                        pen0 = jnp.where((kr0 >= rs) & (kr0 < rs + NA_WIN_H), 0.0, NEG_INF)
                        pen1 = jnp.where((kr0 + 1 >= rs) & (kr0 + 1 < rs + NA_WIN_H), 0.0, NEG_INF)
                        tiles.append(tab + jnp.where(lane < GRID_W, pen0, pen1))
                    qrows = slice(g * BLOCK_Q + a * GRID_W, g * BLOCK_Q + (a + 1) * GRID_W)
                    sb = s[qrows] + jnp.concatenate(tiles, axis=1)
                    sc = s_ctx[qrows]
                    m = jnp.maximum(jnp.max(sb, axis=-1, keepdims=True), jnp.max(sc, axis=-1, keepdims=True))
                    p_lat.append(jnp.exp2(sb - m).astype(BF16))
                    p_ctx.append(jnp.exp2(sc - m).astype(BF16))
            acc = (_dot(jnp.concatenate(p_lat, axis=0), v_ref[0, h, pl.ds(kstart, n_keys), :])
                   + _dot(jnp.concatenate(p_ctx, axis=0), v_ctx))
            pieces += _head_pair_outputs(None, acc, BLOCK_Q)
    o_ref[0] = jnp.concatenate(pieces, axis=-1).astype(o_ref.dtype)


def _attention(mode, qp, qr, k, v, kc, vc, extra):
    b, _, n, _ = qp.shape
    c = kc.shape[2]
    q_spec = pl.BlockSpec((1, N_HEADS, BLOCK_Q, HEAD_DIM), lambda bb, i: (bb, 0, i, 0))
    k_spec = pl.BlockSpec((1, N_KV_HEADS, n, HEAD_DIM), lambda bb, i: (bb, 0, 0, 0))
    v_spec = pl.BlockSpec((1, N_KV_HEADS, n, 2 * LANES), lambda bb, i: (bb, 0, 0, 0))
    kc_spec = pl.BlockSpec((1, N_KV_HEADS, c, HEAD_DIM), lambda bb, i: (bb, 0, 0, 0))
    vc_spec = pl.BlockSpec((1, N_KV_HEADS, c, 2 * LANES), lambda bb, i: (bb, 0, 0, 0))
    if mode == 2:
        args = [qp, k, v, kc, vc, extra]
        in_specs = [q_spec, k_spec, v_spec, kc_spec, vc_spec,
                    pl.BlockSpec(extra.shape, lambda bb, i: (0, 0, 0, 0))]
    else:
        args = [qp, qr, k, v, kc, vc]
        in_specs = [q_spec, q_spec, k_spec, v_spec, kc_spec, vc_spec]
        if mode == 0:
            args.append(extra)
            in_specs.append(pl.BlockSpec(memory_space=pltpu.SMEM))
    return pl.pallas_call(
        functools.partial(_attn_kernel, mode=mode, key_chunk=512),
        grid=(b, n // BLOCK_Q),
        in_specs=in_specs,
        out_specs=pl.BlockSpec((1, BLOCK_Q, D_MODEL), lambda bb, i: (bb, i, 0)),
        out_shape=jax.ShapeDtypeStruct((b, n, D_MODEL), BF16),
        compiler_params=_cparams(("parallel", "arbitrary")),
        name=("attn_window", "attn_global", "attn_na")[mode],
    )(*args)


def _ctx_attn_kernel(*refs, has_sink):
    if has_sink:
        q_ref, k_ref, v_ref, sink_ref, o_ref = refs
    else:
        q_ref, k_ref, v_ref, o_ref = refs
    c = k_ref.shape[2]
    rows = Q_PER_KV * c
    pieces = []
    for h in range(N_KV_HEADS):
        q = q_ref[0, h * Q_PER_KV:(h + 1) * Q_PER_KV].reshape(rows, HEAD_DIM)
        m, acc = _attend([(_dot_nt(q, k_ref[0, h]), v_ref[0, h])])
        if has_sink:
            pieces += _head_pair_outputs(m, acc, c, sink_ref, h)
        else:
            pieces += _head_pair_outputs(None, acc, c)
    o_ref[0] = jnp.concatenate(pieces, axis=-1).astype(o_ref.dtype)


def _ctx_attention(qc, kc, vc, sink):
    b, _, c, _ = qc.shape
    args = [qc, kc, vc]
    in_specs = [pl.BlockSpec((1, N_HEADS, c, HEAD_DIM), lambda bb: (bb, 0, 0, 0)),
                pl.BlockSpec((1, N_KV_HEADS, c, HEAD_DIM), lambda bb: (bb, 0, 0, 0)),
                pl.BlockSpec((1, N_KV_HEADS, c, 2 * LANES), lambda bb: (bb, 0, 0, 0))]
    if sink is not None:
        args.append(sink)
        in_specs.append(pl.BlockSpec(memory_space=pltpu.SMEM))
    return pl.pallas_call(
        functools.partial(_ctx_attn_kernel, has_sink=sink is not None),
        grid=(b,),
        in_specs=in_specs,
        out_specs=pl.BlockSpec((1, c, D_MODEL), lambda bb: (bb, 0, 0)),
        out_shape=jax.ShapeDtypeStruct((b, c, D_MODEL), BF16),
        compiler_params=_cparams(("parallel",)),
        name="attn_ctx",
    )(*args)


def _post_attn_kernel(o_ref, x_ref, wo_ref, gt_ref, g_ref, sh_ref, sc_ref, wrh_ref, wrl_ref, br_ref,
                      xn_ref, hf_ref, eid_ref, wcol_ref, cnt_ref):
    xn = x_ref[0] + gt_ref[0] * _dot(o_ref[0], wo_ref[...])
    xn_ref[0] = xn
    hf = _rms_mod(xn, g_ref[0], sh_ref[0], sc_ref[0])
    hf_ref[0] = _pack_rows(hf)
    tm = hf.shape[0]

    hh, hl = _split_bf16(hf)
    lg = _dot(hh, wrh_ref[...]) + _dot(hl, wrh_ref[...]) + _dot(hh, wrl_ref[...]) + br_ref[...]
    lgt = lg.T

    gl = lgt[0:N_GROUPS]
    ge = jnp.exp(gl - jnp.max(gl, axis=0, keepdims=True))
    pg = ge / jnp.sum(ge, axis=0, keepdims=True)
    p_top = jnp.max(pg, axis=0, keepdims=True)
    grow = lax.broadcasted_iota(I32, (N_GROUPS, tm), 0)
    grp = jnp.min(jnp.where(pg == p_top, grow, N_GROUPS), axis=0, keepdims=True)

    el = lgt[8:8 + EXPERTS_PER_GROUP]
    for gi in range(1, N_GROUPS):
        el = jnp.where(grp == gi, lgt[8 + gi * EXPERTS_PER_GROUP:8 + (gi + 1) * EXPERTS_PER_GROUP], el)
    ee = jnp.exp(el - jnp.max(el, axis=0, keepdims=True))
    pe = ee / jnp.sum(ee, axis=0, keepdims=True)
    erow = lax.broadcasted_iota(I32, (EXPERTS_PER_GROUP, tm), 0)
    p1 = jnp.max(pe, axis=0, keepdims=True)
    i1 = jnp.min(jnp.where(pe == p1, erow, EXPERTS_PER_GROUP), axis=0, keepdims=True)
    pe2 = jnp.where(erow == i1, -1.0, pe)
    p2 = jnp.max(pe2, axis=0, keepdims=True)
    i2 = jnp.min(jnp.where(pe2 == p2, erow, EXPERTS_PER_GROUP), axis=0, keepdims=True)
    den = p1 + p2
    w1 = p_top * p1 / den
    w2 = p_top * p2 / den
    e1 = grp * EXPERTS_PER_GROUP + i1
    e2 = grp * EXPERTS_PER_GROUP + i2

    eid_ref[0:1, :] = e1
    eid_ref[1:2, :] = e2
    wrows = jnp.concatenate([w1, w2, jnp.zeros((ROUTER_COLS - 2, tm), F32)], axis=0)
    wcol_ref[...] = wrows.T
    xrow = lax.broadcasted_iota(I32, (N_EXPERTS, tm), 0)
    hits = (xrow == e1).astype(F32) + (xrow == e2).astype(F32)
    cnt_ref[0] = jnp.sum(hits, axis=1, keepdims=True).astype(I32)


def _post_attn(o, x, wo_bf, modr, g_rows, router, layer, ctx_row, tm):
    b, n, d = x.shape
    nt = n // tm
    t_all = b * n
    wrh, wrl, br = router
    tile = lambda bb, t: (bb, t, 0)
    const2 = lambda bb, t: (0, 0)
    return pl.pallas_call(
        _post_attn_kernel,
        grid=(b, nt),
        in_specs=[
            pl.BlockSpec((1, tm, d), tile),
            pl.BlockSpec((1, tm, d), tile),
            pl.BlockSpec((d, d), const2),
            _mod_spec(layer, 2, ctx_row),
            _layer_row_spec(layer),
            _mod_spec(layer, 3, ctx_row),
            _mod_spec(layer, 4, ctx_row),
            pl.BlockSpec((d, ROUTER_COLS), const2),
            pl.BlockSpec((d, ROUTER_COLS), const2),
            pl.BlockSpec((1, ROUTER_COLS), const2),
        ],
        out_specs=[
            pl.BlockSpec((1, tm, d), tile),
            pl.BlockSpec((1, tm, d // 2), tile),
            pl.BlockSpec((2, tm), lambda bb, t: (0, bb * nt + t)),
            pl.BlockSpec((tm, ROUTER_COLS), lambda bb, t: (bb * nt + t, 0)),
            pl.BlockSpec((1, N_EXPERTS, 1), lambda bb, t: (bb * nt + t, 0, 0)),
        ],
        out_shape=[
            jax.ShapeDtypeStruct((b, n, d), F32),
            jax.ShapeDtypeStruct((b, n, d // 2), U32),
            jax.ShapeDtypeStruct((2, t_all), I32),
            jax.ShapeDtypeStruct((t_all, ROUTER_COLS), F32),
            jax.ShapeDtypeStruct((b * nt, N_EXPERTS, 1), I32),
        ],
        compiler_params=_cparams(("parallel", "parallel")),
        name="post_attn_router",
    )(o, x, wo_bf, modr, g_rows, modr, modr, wrh, wrl, br)


def _dest_kernel(eid_ref, base_ref, tri_ref, dest_ref):
    tm = eid_ref.shape[1]
    xrow = lax.broadcasted_iota(I32, (N_EXPERTS, tm), 0)
    oh0 = xrow == eid_ref[0:1, :]
    oh1 = xrow == eid_ref[1:2, :]
    both = oh0.astype(F32) + oh1.astype(F32)
    incl = _dot(both.astype(BF16), tri_ref[...])
    before = incl - both + base_ref[0].astype(F32)
    d0 = jnp.sum(jnp.where(oh0, before, 0.0), axis=0, keepdims=True)
    d1 = jnp.sum(jnp.where(oh1, before, 0.0), axis=0, keepdims=True)
    dest_ref[0, 0:1, :] = d0.astype(I32)
    dest_ref[0, 1:2, :] = d1.astype(I32)


def _dest_rows(eid, base, tri):
    tm = tri.shape[0]
    nt = eid.shape[1] // tm
    return pl.pallas_call(
        _dest_kernel,
        grid=(nt,),
        in_specs=[
            pl.BlockSpec((2, tm), lambda i: (0, i)),
            pl.BlockSpec((1, N_EXPERTS, 1), lambda i: (i, 0, 0)),
            pl.BlockSpec((tm, tm), lambda i: (0, 0)),
        ],
        out_specs=pl.BlockSpec((1, 2, tm), lambda i: (i, 0, 0)),
        out_shape=jax.ShapeDtypeStruct((nt, 2, tm), I32),
        compiler_params=_cparams(("parallel",)),
        name="moe_dest",
    )(eid, base, tri)


def _scatter_kernel(*refs, tiles):
    dest_ref = refs[0]
    hf_refs = refs[1:1 + len(tiles)]
    xs_ref, sem = refs[1 + len(tiles):]
    tm = hf_refs[0].shape[0]
    i = pl.program_id(0)

    def scatter_tile(hf_ref):
        def issue(t8, carry):
            base = pl.multiple_of(t8 * ROW_UNROLL, ROW_UNROLL)
            rows = hf_ref.at[pl.ds(base, ROW_UNROLL)]
            for u in range(ROW_UNROLL):
                for k in range(2):
                    d = dest_ref.at[0, k, pl.ds(base, ROW_UNROLL)][u]
                    pltpu.make_async_copy(rows.at[pl.ds(u, 1)], xs_ref.at[d], sem).start(priority=k)
            return carry

        lax.fori_loop(0, tm // ROW_UNROLL, issue, 0)
        for k in range(2):
            pltpu.make_async_copy(hf_ref, xs_ref.at[pl.ds(0, tm), 0], sem).wait()

    first = 0
    for hf_ref, nt in zip(hf_refs, tiles):
        pl.when((i >= first) & (i < first + nt))(functools.partial(scatter_tile, hf_ref))
        first += nt


def _scatter_rows(dest, hf_streams, n_rows):
    nt_all, _, tm = dest.shape
    d = hf_streams[0].shape[1]
    tiles = tuple(h.shape[0] // tm for h in hf_streams)
    assert sum(tiles) == nt_all
    in_specs = [pl.BlockSpec((1, 2, tm), lambda i: (i, 0, 0), memory_space=pltpu.SMEM)]
    first = 0
    for nt in tiles:
        in_specs.append(pl.BlockSpec((tm, d), lambda i, first=first, nt=nt: (jnp.clip(i - first, 0, nt - 1), 0)))
        first += nt
    return pl.pallas_call(
        functools.partial(_scatter_kernel, tiles=tiles),
        grid=(nt_all,),
        in_specs=in_specs,
        out_specs=pl.BlockSpec(memory_space=pl.ANY),
        out_shape=jax.ShapeDtypeStruct((n_rows, 1, d), hf_streams[0].dtype),
        scratch_shapes=[pltpu.SemaphoreType.DMA(())],
        compiler_params=_cparams(("arbitrary",)),
        name="moe_scatter",
    )(dest, *hf_streams)


def _expert_kernel(ve_ref, vblk_ref, lo_ref, hi_ref, x_hbm, wg_ref, wu_ref, wd_ref, y_hbm, wg_s, wu_s, wd_s,
                   xbuf, xsems, ybuf, ysems, *, n_blocks):
    v = pl.program_id(0)
    n_visits = pl.num_programs(0)
    pv = jnp.maximum(v - 1, 0)
    nv = jnp.minimum(v + 1, n_visits - 1)
    blk = vblk_ref[v]
    new_expert = (v == 0) | (ve_ref[v] != ve_ref[pv])
    first_visit = (v == 0) | (blk != vblk_ref[pv])
    last_visit = (v == n_visits - 1) | (blk != vblk_ref[nv])
    lo = lo_ref[v]
    hi = hi_ref[v]
    xslot = v % 2
    yslot = blk % 2

    def x_copy(visit, slot):
        start = pl.multiple_of(vblk_ref[visit] * EXPERT_BLOCK, EXPERT_BLOCK)
        return pltpu.make_async_copy(x_hbm.at[pl.ds(start, EXPERT_BLOCK), 0], xbuf.at[slot], xsems.at[slot])

    def y_copy(slot):
        start = pl.multiple_of(blk * EXPERT_BLOCK, EXPERT_BLOCK)
        return pltpu.make_async_copy(ybuf.at[slot], y_hbm.at[pl.ds(start, EXPERT_BLOCK), 0], ysems.at[slot])

    @pl.when(v == 0)
    def _():
        x_copy(0, 0).start()

    @pl.when(v + 1 < n_visits)
    def _():
        x_copy(v + 1, 1 - xslot).start()

    @pl.when(new_expert)
    def _():
        wg_s[...] = wg_ref[0, 0].astype(BF16)
        wu_s[...] = wu_ref[0, 0].astype(BF16)
        wd_s[...] = wd_ref[0, 0].astype(BF16)

    @pl.when(first_visit & (blk >= 2))
    def _():
        y_copy(yslot).wait()

    x_copy(v, xslot).wait()

    @pl.when(hi > lo)
    def _():
        x_lo, x_hi = _unpack_rows(xbuf[xslot])
        x_lo = x_lo.astype(BF16)
        x_hi = x_hi.astype(BF16)
        half = x_lo.shape[1]
        gate = _dot(x_lo, wg_s[:half]) + _dot(x_hi, wg_s[half:])
        up = _dot(x_lo, wu_s[:half]) + _dot(x_hi, wu_s[half:])
        mid = gate * jax.nn.sigmoid(gate) * up
        y = _pack_rows(_dot(mid.astype(BF16), wd_s[...]))
        row = lax.broadcasted_iota(I32, (EXPERT_BLOCK, 1), 0)
        mine = (row >= lo) & (row < hi)

        @pl.when(first_visit)
        def _():
            ybuf[yslot] = jnp.where(mine, y, jnp.uint32(0))

        @pl.when(jnp.logical_not(first_visit))
        def _():
            ybuf[yslot] = jnp.where(mine, y, ybuf[yslot])

    @pl.when(last_visit)
    def _():
        y_copy(yslot).start()

    @pl.when(v == n_visits - 1)
    def _():
        y_copy(yslot).wait()
        if n_blocks >= 2:
            y_copy(1 - yslot).wait()


def _experts(visits, xs, w_gate, w_up, w_down, layer):
    n_rows, _, dp = xs.shape
    _, _, d, f = w_gate.shape
    assert d == 2 * dp and n_rows % EXPERT_BLOCK == 0
    ve, vblk, lo, hi = visits
    w_map = lambda v, ve_r, vb_r, lo_r, hi_r: (layer, ve_r[v], 0, 0)
    grid_spec = pltpu.PrefetchScalarGridSpec(
        num_scalar_prefetch=4,
        grid=(ve.shape[0],),
        in_specs=[
            pl.BlockSpec(memory_space=pl.ANY),
            pl.BlockSpec((1, 1, d, f), w_map),
            pl.BlockSpec((1, 1, d, f), w_map),
            pl.BlockSpec((1, 1, f, d), w_map),
        ],
        out_specs=pl.BlockSpec(memory_space=pl.ANY),
        scratch_shapes=[pltpu.VMEM((d, f), BF16), pltpu.VMEM((d, f), BF16), pltpu.VMEM((f, d), BF16),
                        pltpu.VMEM((2, EXPERT_BLOCK, dp), U32), pltpu.SemaphoreType.DMA((2,)),
                        pltpu.VMEM((2, EXPERT_BLOCK, dp), U32), pltpu.SemaphoreType.DMA((2,))],
    )
    return pl.pallas_call(
        functools.partial(_expert_kernel, n_blocks=n_rows // EXPERT_BLOCK),
        grid_spec=grid_spec,
        out_shape=jax.ShapeDtypeStruct((n_rows, 1, dp), U32),
        compiler_params=_cparams(("arbitrary",)),
        name="moe_experts",
    )(ve, vblk, lo, hi, xs, w_gate, w_up, w_down)


def _combine_kernel(*refs, final_norm):
    if final_norm:
        dest_ref, dest_next_ref, xn_ref, wcol_ref, gt_ref, gfin_ref, y_ref, o_ref, ybuf, sems = refs
    else:
        dest_ref, dest_next_ref, xn_ref, wcol_ref, gt_ref, y_ref, o_ref, ybuf, sems = refs
    tm = xn_ref.shape[1]
    step = pl.program_id(0) * pl.num_programs(1) + pl.program_id(1)
    n_steps = pl.num_programs(0) * pl.num_programs(1)

    def gather_tile(d_ref, slot):
        def issue(t8, carry):
            base = pl.multiple_of(t8 * ROW_UNROLL, ROW_UNROLL)
            for k in range(2):
                rows = ybuf.at[slot, k, pl.ds(base, ROW_UNROLL)]
                for u in range(ROW_UNROLL):
                    d = d_ref.at[0, k, pl.ds(base, ROW_UNROLL)][u]
                    pltpu.make_async_copy(y_ref.at[d], rows.at[pl.ds(u, 1)], sems.at[slot]).start(priority=k)
            return carry

        lax.fori_loop(0, tm // ROW_UNROLL, issue, 0)

    def finish_tile(slot):
        for k in range(2):
            pltpu.make_async_copy(y_ref.at[pl.ds(0, tm), 0], ybuf.at[slot, k], sems.at[slot]).wait()
        w = wcol_ref[...]
        lo0, hi0 = _unpack_rows(ybuf[slot, 0])
        lo1, hi1 = _unpack_rows(ybuf[slot, 1])
        moe = jnp.concatenate([w[:, 0:1] * lo0 + w[:, 1:2] * lo1, w[:, 0:1] * hi0 + w[:, 1:2] * hi1], axis=1)
        out = xn_ref[0] + gt_ref[0] * moe
        if final_norm:
            ms = jnp.mean(out * out, axis=-1, keepdims=True)
            out = out * lax.rsqrt(ms + NORM_EPS) * gfin_ref[...]
        o_ref[0] = out

    pl.when(step == 0)(functools.partial(gather_tile, dest_ref, 0))
    for slot in range(2):
        @pl.when((step % 2 == slot) & (step + 1 < n_steps))
        def _(slot=slot):
            gather_tile(dest_next_ref, 1 - slot)

        pl.when(step % 2 == slot)(functools.partial(finish_tile, slot))


def _combine(dest, xn, wcol, modr, y, layer, ctx_row, g_final=None):
    b, n, d = xn.shape
    nt_all, _, tm = dest.shape
    nt = n // tm
    final_norm = g_final is not None
    args = [dest, dest, xn, wcol, modr]
    in_specs = [
        pl.BlockSpec((1, 2, tm), lambda bb, t: (bb * nt + t, 0, 0), memory_space=pltpu.SMEM),
        pl.BlockSpec((1, 2, tm), lambda bb, t: (jnp.minimum(bb * nt + t + 1, nt_all - 1), 0, 0),
                     memory_space=pltpu.SMEM),
        pl.BlockSpec((1, tm, d), lambda bb, t: (bb, t, 0)),
        pl.BlockSpec((tm, ROUTER_COLS), lambda bb, t: (bb * nt + t, 0)),
        _mod_spec(layer, 5, ctx_row),
    ]
    if final_norm:
        args.append(g_final.reshape(1, d))
        in_specs.append(pl.BlockSpec((1, d), lambda bb, t: (0, 0)))
    args.append(y)
    in_specs.append(pl.BlockSpec(memory_space=pl.ANY))
    return pl.pallas_call(
        functools.partial(_combine_kernel, final_norm=final_norm),
        grid=(b, nt),
        in_specs=in_specs,
        out_specs=pl.BlockSpec((1, tm, d), lambda bb, t: (bb, t, 0)),
        out_shape=jax.ShapeDtypeStruct((b, n, d), F32),
        scratch_shapes=[pltpu.VMEM((2, 2, tm, y.shape[2]), y.dtype), pltpu.SemaphoreType.DMA((2,))],
        compiler_params=_cparams(("arbitrary", "arbitrary")),
        name="moe_combine",
    )(*args)


def _rope_tables(n):
    t = np.arange(n)
    row = (t // GRID_W).astype(np.float32)
    col = (t % GRID_W).astype(np.float32)
    quarter = HEAD_DIM // 4
    inv = jnp.asarray(ROPE_THETA, F32) ** (-jnp.arange(quarter, dtype=F32) / quarter)
    ar = jnp.asarray(row)[:, None] * inv
    ac = jnp.asarray(col)[:, None] * inv
    ang = jnp.concatenate([ar, ar, ac, ac], axis=-1)
    ang = jnp.concatenate([ang, ang], axis=-1)
    sign = np.where((np.arange(LANES) % (HEAD_DIM // 2)) < quarter, -1.0, 1.0).astype(np.float32)
    return jnp.cos(ang), jnp.sin(ang) * sign


def _na_bias_tables(rpb):
    n_dr, n_dc = 2 * NA_WIN_H - 1, 2 * NA_WIN_W - 1
    blocked = jnp.full((N_HEADS, 2, n_dc), NEG_INF, F32)
    rows = jnp.concatenate([blocked, rpb.astype(F32) * LOG2E, blocked], axis=1)
    assert rows.shape[1] == n_dr + 4 == NA_TABLE + 1
    pair = jnp.stack([rows[:, 0:NA_TABLE], rows[:, 1:NA_TABLE + 1]], axis=2)
    c = np.arange(GRID_W)[:, None]
    kc = np.arange(GRID_W)[None, :]
    cs = np.clip(c - NA_WIN_W // 2, 0, GRID_W - NA_WIN_W)
    in_window = (kc >= cs) & (kc < cs + NA_WIN_W)
    out = jnp.full((N_HEADS, NA_TABLE, GRID_W, 2, GRID_W), NEG_INF, F32)
    for dc in range(-(NA_WIN_W - 1), NA_WIN_W):
        hit = ((kc - c) == dc) & in_window
        if hit.any():
            out = jnp.where(hit[None, None, :, None, :], pair[:, :, None, :, dc + NA_WIN_W - 1, None], out)
    return out.reshape(N_HEADS, NA_TABLE, GRID_W, LANES)


def _router_tables(w_group, b_group, w_router, b_router):
    d = w_group.shape[0]
    w = jnp.zeros((d, ROUTER_COLS), F32)
    w = w.at[:, 0:N_GROUPS].set(w_group).at[:, 8:8 + N_EXPERTS].set(w_router)
    bias = jnp.zeros((1, ROUTER_COLS), F32)
    bias = bias.at[0, 0:N_GROUPS].set(b_group).at[0, 8:8 + N_EXPERTS].set(b_router)
    hi = w.astype(BF16)
    lo = (w - hi.astype(F32)).astype(BF16)
    return hi, lo, bias


def _pick(table, onehot):
    return jnp.sum(jnp.where(onehot, table[None, :], 0), axis=1)


def _moe_layout(cnt_tiles, n_rows):
    cnt = cnt_tiles[:, :, 0]
    counts = jnp.sum(cnt, axis=0)
    ends = jnp.cumsum(counts)
    starts = ends - counts
    base = starts[None, :] + jnp.cumsum(cnt, axis=0) - cnt
    first_blk = starts // EXPERT_BLOCK
    n_blk = jnp.where(counts > 0, (ends - 1) // EXPERT_BLOCK - first_blk + 1, 0)
    v_end = jnp.cumsum(n_blk)
    v_off = v_end - n_blk
    n_visits = n_rows // EXPERT_BLOCK + N_EXPERTS
    v = jnp.arange(n_visits, dtype=I32)
    valid = v < v_end[-1]
    vv = jnp.minimum(v, v_end[-1] - 1)
    ve = jnp.sum((v_end[None, :] <= vv[:, None]).astype(I32), axis=1)
    onehot = ve[:, None] == jnp.arange(N_EXPERTS, dtype=I32)[None, :]
    vblk = _pick(first_blk, onehot) + vv - _pick(v_off, onehot)
    lo = jnp.clip(_pick(starts, onehot) - vblk * EXPERT_BLOCK, 0, EXPERT_BLOCK)
    hi = jnp.clip(_pick(ends, onehot) - vblk * EXPERT_BLOCK, 0, EXPERT_BLOCK)
    hi = jnp.where(valid, hi, lo)
    visits = tuple(a.astype(I32) for a in (ve, vblk, lo, hi))
    return base[:, :, None].astype(I32), visits


def kernel(x, c, ctx, c_ctx, w_ada, b_ada, g_attn, w_qkv, w_o, sink_a, gq_b, gk_b, rpb_c, g_ffn,
           w_group, b_group, w_router, b_router, w_gate, w_up, w_down, g_final):
    b, n, d = x.shape
    n_ctx = ctx.shape[1]
    depth = w_ada.shape[0]
    tm_tok = 512
    tm_ctx = 256
    assert b + 1 <= MOD_ROWS and d == D_MODEL and n % tm_tok == 0 and n_ctx % tm_ctx == 0
    assert (b * n_ctx) % tm_tok == 0
    ctx_row = b

    cc = jnp.zeros((MOD_ROWS, d), F32).at[:b].set(c).at[b].set(c_ctx)
    modr = _ada_all(cc, w_ada, b_ada).reshape(depth * MOD_ROWS * N_MOD, 1, d)
    g_attn_rows = g_attn.reshape(depth, 1, d)
    g_ffn_rows = g_ffn.reshape(depth, 1, d)
    rope_tabs = _rope_tables(n)
    head_sum = jnp.asarray(np.kron(np.eye(LANES // HEAD_DIM), np.ones((HEAD_DIM, HEAD_DIM))), BF16)
    tri = jnp.asarray(np.triu(np.ones((tm_tok, tm_tok))), BF16)

    xc = ctx
    for i in range(depth):
        m = i % N_MIXERS
        j = i // N_MIXERS
        last = i == depth - 1
        wqkv_bf = w_qkv[i].astype(BF16)
        wo_bf = w_o[i].astype(BF16)
        qk_gains = None
        if m == 1:
            qk_gains = (jnp.tile(gq_b[j], LANES // HEAD_DIM).reshape(1, LANES),
                        jnp.tile(gk_b[j], LANES // HEAD_DIM).reshape(1, LANES), head_sum)

        if m == 2:
            qp, k, v = _qkv(x, modr, g_attn_rows, wqkv_bf, i, None, tm_tok, None, qk_gains)
            qr = None
        else:
            qp, qr, k, v = _qkv(x, modr, g_attn_rows, wqkv_bf, i, None, tm_tok, rope_tabs, qk_gains)
        qc, kc, vc = _qkv(xc, modr, g_attn_rows, wqkv_bf, i, ctx_row, tm_ctx, None, qk_gains)
        sink = sink_a[j] if m == 0 else None
        extra = sink if m == 0 else (_na_bias_tables(rpb_c[j]) if m == 2 else None)
        o = _attention(m, qp, qr, k, v, kc, vc, extra)

        router = _router_tables(w_group[i], b_group[i], w_router[i], b_router[i])
        streams = [(o, x, None)]
        if not last:
            oc = _ctx_attention(qc, kc, vc, sink)
            streams.append((oc.reshape(1, b * n_ctx, d), xc.reshape(1, b * n_ctx, d), ctx_row))
        routed = [_post_attn(o_s, x_s, wo_bf, modr, g_ffn_rows, router, i, row, tm_tok) for (o_s, x_s, row) in streams]

        n_rows = 2 * sum(r[2].shape[1] for r in routed)
        assert n_rows % EXPERT_BLOCK == 0
        base, visits = _moe_layout(jnp.concatenate([r[4] for r in routed], axis=0), n_rows)
        dest = _dest_rows(jnp.concatenate([r[2] for r in routed], axis=1), base, tri)
        xs = _scatter_rows(dest, [r[1].reshape(-1, d // 2) for r in routed], n_rows)
        y = _experts(visits, xs, w_gate, w_up, w_down, i)
        nt_lat = routed[0][4].shape[0]
        x = _combine(dest[:nt_lat], routed[0][0], routed[0][3], modr, y, i, None, g_final if last else None)
        if not last:
            xc = _combine(dest[nt_lat:], routed[1][0], routed[1][3], modr, y, i, ctx_row).reshape(b, n_ctx, d)
    return x
```

```python
import functools

import numpy as np
import jax
import jax.numpy as jnp
from jax import lax
from jax.experimental import pallas as pl
from jax.experimental.pallas import tpu as pltpu

F32 = jnp.float32
BF16 = jnp.bfloat16
I32 = jnp.int32
U32 = jnp.uint32

D_MODEL = 1024
GRID_W = 64
N_MIXERS = 3
N_HEADS = 16
N_KV_HEADS = 4
HEAD_DIM = D_MODEL // N_HEADS
Q_PER_KV = N_HEADS // N_KV_HEADS
QKV_DIM = (N_HEADS + 2 * N_KV_HEADS) * HEAD_DIM
ATTN_SCALE = HEAD_DIM ** -0.5
ROPE_THETA = 10000.0
BLOCK_Q = 128
WINDOW = 128
NA_WIN_H = 8
NA_WIN_W = 16
N_GROUPS = 4
EXPERTS_PER_GROUP = 8
N_EXPERTS = N_GROUPS * EXPERTS_PER_GROUP
EXPERT_HIDDEN = D_MODEL // 2
NORM_EPS = 1e-6
NEG_INF = -1e30
LOG2E = 1.4426950408889634

LANES = 128
MOD_ROWS = 24
N_MOD = 6
EXPERT_BLOCK = 512
ROW_UNROLL = 8
ROUTER_COLS = LANES
NA_KEY_ROWS = 10
NA_TABLE = 18
VMEM_LIMIT = 56 * 1024 * 1024


def _cparams(sem, vmem=VMEM_LIMIT):
    return pltpu.CompilerParams(dimension_semantics=sem, vmem_limit_bytes=vmem)


def _dot(a, b):
    return jnp.dot(a, b, preferred_element_type=F32)


def _dot_nt(a, b):
    return lax.dot_general(a, b, (((1,), (1,)), ((), ())), preferred_element_type=F32)


def _split_bf16(x):
    hi = x.astype(BF16)
    lo = (x - hi.astype(F32)).astype(BF16)
    return hi, lo


def _pack_rows(x):
    w = x.shape[1] // 2
    bits = pltpu.bitcast(x.astype(BF16).astype(F32), U32)
    return (bits[:, w:] & jnp.uint32(0xFFFF0000)) | (bits[:, :w] >> 16)


def _unpack_rows(p):
    lo = pltpu.bitcast(p << 16, F32)
    hi = pltpu.bitcast(p & jnp.uint32(0xFFFF0000), F32)
    return lo, hi


def _rms_mod(x, g, sh, sc):
    ms = jnp.mean(x * x, axis=-1, keepdims=True)
    y = x * lax.rsqrt(ms + NORM_EPS) * g
    return y * (1.0 + sc) + sh


def _ada_kernel(c_ref, w_ref, b_ref, o_ref):
    c = c_ref[...]
    s = c * jax.nn.sigmoid(c)
    o_ref[0] = _dot(s.astype(BF16), w_ref[0].astype(BF16)) + b_ref[0]


def _ada_all(cc, w_ada, b_ada):
    depth, d, d6 = w_ada.shape
    tn = 1536
    return pl.pallas_call(
        _ada_kernel,
        grid=(depth, d6 // tn),
        in_specs=[
            pl.BlockSpec((MOD_ROWS, d), lambda i, j: (0, 0)),
            pl.BlockSpec((1, d, tn), lambda i, j: (i, 0, j)),
            pl.BlockSpec((1, 1, tn), lambda i, j: (i, 0, j)),
        ],
        out_specs=pl.BlockSpec((1, MOD_ROWS, tn), lambda i, j: (i, 0, j)),
        out_shape=jax.ShapeDtypeStruct((depth, MOD_ROWS, d6), F32),
        compiler_params=_cparams(("parallel", "parallel")),
        name="ada_mod",
    )(cc, w_ada, b_ada.reshape(depth, 1, d6))


def _mod_spec(layer, chunk, ctx_row):
    if ctx_row is None:
        return pl.BlockSpec((1, 1, D_MODEL), lambda b, t: ((layer * MOD_ROWS + b) * N_MOD + chunk, 0, 0))
    return pl.BlockSpec((1, 1, D_MODEL), lambda b, t: ((layer * MOD_ROWS + ctx_row) * N_MOD + chunk, 0, 0))


def _layer_row_spec(layer):
    return pl.BlockSpec((1, 1, D_MODEL), lambda b, t: (layer, 0, 0))


def _qkv_kernel(*refs, qk_norm, rope):
    _qkv_project(refs[0][0], refs[1:], qk_norm=qk_norm, rope=rope)


def _qkv_project(x, refs, *, qk_norm, rope):
    g_ref, sh_ref, sc_ref, w_ref = refs[:4]
    pos = 4
    if rope:
        cos_ref, sin_ref = refs[pos:pos + 2]
        pos += 2
    if qk_norm:
        gq_ref, gk_ref, bm_ref = refs[pos:pos + 3]
        pos += 3
    if rope:
        qp_ref, qr_ref, k_ref, v_ref = refs[pos:]
    else:
        qp_ref, k_ref, v_ref = refs[pos:]

    h = _rms_mod(x, g_ref[0], sh_ref[0], sc_ref[0])
    res = _dot(h.astype(BF16), w_ref[...])
    half = LANES // 2
    n_q = N_HEADS * HEAD_DIM // LANES
    n_k = N_KV_HEADS * HEAD_DIM // LANES
    if rope:
        lane = lax.broadcasted_iota(I32, (1, LANES), 1)
        first_half = (lane & (HEAD_DIM // 2 - 1)) < (HEAD_DIM // 4)
    for cb in range(QKV_DIM // LANES):
        xs = res[:, cb * LANES:(cb + 1) * LANES]
        is_q = cb < n_q
        is_k = n_q <= cb < n_q + n_k
        if qk_norm and (is_q or is_k):
            hi, lo = _split_bf16(xs * xs)
            ms = (_dot(hi, bm_ref[...]) + _dot(lo, bm_ref[...])) * (1.0 / HEAD_DIM)
            gain = gq_ref[...] if is_q else gk_ref[...]
            xs = xs * lax.rsqrt(ms + NORM_EPS) * gain
        if is_q:
            xs = xs * (ATTN_SCALE * LOG2E)
        if rope and (is_q or is_k):
            rot = jnp.where(first_half, pltpu.roll(xs, LANES - HEAD_DIM // 4, 1), pltpu.roll(xs, HEAD_DIM // 4, 1))
            xr = xs * cos_ref[...] + rot * sin_ref[...]
        if is_q:
            hd = 2 * cb
            qp_ref[0, hd] = xs[:, :half].astype(BF16)
            qp_ref[0, hd + 1] = xs[:, half:].astype(BF16)
            if rope:
                qr_ref[0, hd] = xr[:, :half].astype(BF16)
                qr_ref[0, hd + 1] = xr[:, half:].astype(BF16)
        elif is_k:
            hd = 2 * (cb - n_q)
            kk = xr if rope else xs
            k_ref[0, hd] = kk[:, :half].astype(BF16)
            k_ref[0, hd + 1] = kk[:, half:].astype(BF16)
        else:
            hd = 2 * (cb - n_q - n_k)
            low = lax.broadcasted_iota(I32, (1, LANES), 1) < HEAD_DIM
            swapped = pltpu.roll(xs, half, 1)
            for par, (a, b) in enumerate(((xs, swapped), (swapped, xs))):
                v_aug = jnp.concatenate([jnp.where(low, a, 1.0), jnp.where(low, 1.0, b)], axis=1)
                v_ref[0, hd + par] = v_aug.astype(BF16)


def _qkv_operands(b, n, modr, g_rows, w_bf, layer, ctx_row, tm, rope_tabs, qk_gains):
    d = D_MODEL
    rope = rope_tabs is not None
    args = [g_rows, modr, modr, w_bf]
    in_specs = [
        _layer_row_spec(layer),
        _mod_spec(layer, 0, ctx_row),
        _mod_spec(layer, 1, ctx_row),
        pl.BlockSpec((d, QKV_DIM), lambda bb, t: (0, 0)),
    ]
    if rope:
        args += list(rope_tabs)
        in_specs += [pl.BlockSpec((tm, LANES), lambda bb, t: (t, 0))] * 2
    if qk_gains is not None:
        args += list(qk_gains)
        in_specs += [pl.BlockSpec((1, LANES), lambda bb, t: (0, 0))] * 2
        in_specs += [pl.BlockSpec((LANES, LANES), lambda bb, t: (0, 0))]
    q_shape = jax.ShapeDtypeStruct((b, N_HEADS, n, HEAD_DIM), BF16)
    kv_shape = jax.ShapeDtypeStruct((b, N_KV_HEADS, n, HEAD_DIM), BF16)
    q_spec = pl.BlockSpec((1, N_HEADS, tm, HEAD_DIM), lambda bb, t: (bb, 0, t, 0))
    kv_spec = pl.BlockSpec((1, N_KV_HEADS, tm, HEAD_DIM), lambda bb, t: (bb, 0, t, 0))
    v_shape = jax.ShapeDtypeStruct((b, N_KV_HEADS, n, 2 * LANES), BF16)
    v_spec = pl.BlockSpec((1, N_KV_HEADS, tm, 2 * LANES), lambda bb, t: (bb, 0, t, 0))
    if rope:
        out_shape, out_specs = [q_shape, q_shape, kv_shape, v_shape], [q_spec, q_spec, kv_spec, v_spec]
    else:
        out_shape, out_specs = [q_shape, kv_shape, v_shape], [q_spec, kv_spec, v_spec]
    return args, in_specs, out_shape, out_specs


def _qkv(x, modr, g_rows, w_bf, layer, ctx_row, tm, rope_tabs=None, qk_gains=None):
    b, n, d = x.shape
    args, in_specs, out_shape, out_specs = _qkv_operands(b, n, modr, g_rows, w_bf, layer, ctx_row, tm,
                                                         rope_tabs, qk_gains)
    return pl.pallas_call(
        functools.partial(_qkv_kernel, qk_norm=qk_gains is not None, rope=rope_tabs is not None),
        grid=(b, n // tm),
        in_specs=[pl.BlockSpec((1, tm, d), lambda bb, t: (bb, t, 0))] + in_specs,
        out_specs=out_specs,
        out_shape=out_shape,
        compiler_params=_cparams(("parallel", "parallel")),
        name="qkv_rope" if rope_tabs is not None else "qkv",
    )(x, *args)


def _attend(pieces, m_prev=None, acc_prev=None):
    m = m_prev
    for s, _ in pieces:
        m_s = jnp.max(s, axis=-1, keepdims=True)
        m = m_s if m is None else jnp.maximum(m, m_s)
    acc = None if acc_prev is None else jnp.exp2(m_prev - m) * acc_prev
    for s, v_aug in pieces:
        pv = _dot(jnp.exp2(s - m).astype(BF16), v_aug)
        acc = pv if acc is None else acc + pv
    return m, acc


def _head_pair_outputs(m, acc, rows_per_head, sink_ref=None, kv_head=None):
    low = lax.broadcasted_iota(I32, (1, LANES), 1) < HEAD_DIM
    blocks = []
    for pair in range(Q_PER_KV // 2):
        forms = []
        for par in range(2):
            g = 2 * pair + par
            rows = slice(g * rows_per_head, (g + 1) * rows_per_head)
            num = acc[rows, par * LANES:(par + 1) * LANES]
            den = acc[rows, (1 - par) * LANES:(2 - par) * LANES]
            if sink_ref is not None:
                sink = sink_ref[kv_head * Q_PER_KV + g] * LOG2E
                m2 = jnp.maximum(m[rows], sink)
                scale = jnp.exp2(m[rows] - m2)
                num = num * scale
                den = den * scale + jnp.exp2(sink - m2)
            forms.append(num / den)
        blocks.append(jnp.where(low, forms[0], forms[1]))
    return blocks


def _attn_kernel(*refs, mode, key_chunk):
    if mode == 0:
        qp_ref, qr_ref, k_ref, v_ref, kc_ref, vc_ref, sink_ref, o_ref = refs
    elif mode == 1:
        qp_ref, qr_ref, k_ref, v_ref, kc_ref, vc_ref, o_ref = refs
    else:
        qp_ref, k_ref, v_ref, kc_ref, vc_ref, t2_ref, o_ref = refs
    qb = pl.program_id(1)
    n = k_ref.shape[2]
    rows = Q_PER_KV * BLOCK_Q

    if mode == 0:
        span = BLOCK_Q + 2 * WINDOW
        kstart = pl.multiple_of(jnp.clip(qb * BLOCK_Q - WINDOW, 0, n - span), BLOCK_Q)
        qpos = qb * BLOCK_Q + (lax.broadcasted_iota(I32, (rows, span), 0) & (BLOCK_Q - 1))
        kpos = kstart + lax.broadcasted_iota(I32, (rows, span), 1)
        band = jnp.abs(kpos - qpos) <= WINDOW
    elif mode == 2:
        rows_grid = n // GRID_W
        r0 = qb * (BLOCK_Q // GRID_W)
        wr = jnp.clip(r0 - NA_WIN_H // 2, 0, rows_grid - NA_KEY_ROWS)
        kstart = pl.multiple_of(wr * GRID_W, GRID_W)
        lane = lax.broadcasted_iota(I32, (GRID_W, LANES), 1)

    pieces = []
    for h in range(N_KV_HEADS):
        qp = qp_ref[0, h * Q_PER_KV:(h + 1) * Q_PER_KV].reshape(rows, HEAD_DIM)
        ctx_piece = (_dot_nt(qp, kc_ref[0, h]), vc_ref[0, h])
        if mode == 0:
            qr = qr_ref[0, h * Q_PER_KV:(h + 1) * Q_PER_KV].reshape(rows, HEAD_DIM)
            s = jnp.where(band, _dot_nt(qr, k_ref[0, h, pl.ds(kstart, span), :]), NEG_INF)
            m, acc = _attend([(s, v_ref[0, h, pl.ds(kstart, span), :]), ctx_piece])
            pieces += _head_pair_outputs(m, acc, BLOCK_Q, sink_ref, h)
        elif mode == 1:
            qr = qr_ref[0, h * Q_PER_KV:(h + 1) * Q_PER_KV].reshape(rows, HEAD_DIM)
            n_chunks = n // key_chunk
            m, acc = None, None
            for c in range(n_chunks):
                keys = slice(c * key_chunk, (c + 1) * key_chunk)
                step = [(_dot_nt(qr, k_ref[0, h, keys, :]), v_ref[0, h, keys, :])]
                if c == n_chunks - 1:
                    step.append(ctx_piece)
                m, acc = _attend(step, m, acc)
            pieces += _head_pair_outputs(None, acc, BLOCK_Q)
        else:
            n_keys = NA_KEY_ROWS * GRID_W
            s = _dot_nt(qp, k_ref[0, h, pl.ds(kstart, n_keys), :])
            s_ctx, v_ctx = ctx_piece
            p_lat, p_ctx = [], []
            for g in range(Q_PER_KV):
                for a in range(BLOCK_Q // GRID_W):
                    r = r0 + a
                    rs = jnp.clip(r - NA_WIN_H // 2, 0, rows_grid - NA_WIN_H)
                    tiles = []
                    for j in range(n_keys // LANES):
                        kr0 = wr + 2 * j
                        tab = t2_ref[h * Q_PER_KV + g, kr0 - r + NA_TABLE // 2]
                        pen0 = jnp.where((kr0 >= rs) & (kr0 < rs + NA_WIN_H), 0.0, NEG_INF)
                        pen1 = jnp.where((kr0 + 1 >= rs) & (kr0 + 1 < rs + NA_WIN_H), 0.0, NEG_INF)
                        tiles.append(tab + jnp.where(lane < GRID_W, pen0, pen1))
                    qrows = slice(g * BLOCK_Q + a * GRID_W, g * BLOCK_Q + (a + 1) * GRID_W)
                    sb = s[qrows] + jnp.concatenate(tiles, axis=1)
                    sc = s_ctx[qrows]
                    m = jnp.maximum(jnp.max(sb, axis=-1, keepdims=True), jnp.max(sc, axis=-1, keepdims=True))
                    p_lat.append(jnp.exp2(sb - m).astype(BF16))
                    p_ctx.append(jnp.exp2(sc - m).astype(BF16))
            acc = (_dot(jnp.concatenate(p_lat, axis=0), v_ref[0, h, pl.ds(kstart, n_keys), :])
                   + _dot(jnp.concatenate(p_ctx, axis=0), v_ctx))
            pieces += _head_pair_outputs(None, acc, BLOCK_Q)
    o_ref[0] = jnp.concatenate(pieces, axis=-1).astype(o_ref.dtype)


def _attention(mode, qp, qr, k, v, kc, vc, extra):
    b, _, n, _ = qp.shape
    c = kc.shape[2]
    q_spec = pl.BlockSpec((1, N_HEADS, BLOCK_Q, HEAD_DIM), lambda bb, i: (bb, 0, i, 0))
    k_spec = pl.BlockSpec((1, N_KV_HEADS, n, HEAD_DIM), lambda bb, i: (bb, 0, 0, 0))
    v_spec = pl.BlockSpec((1, N_KV_HEADS, n, 2 * LANES), lambda bb, i: (bb, 0, 0, 0))
    kc_spec = pl.BlockSpec((1, N_KV_HEADS, c, HEAD_DIM), lambda bb, i: (bb, 0, 0, 0))
    vc_spec = pl.BlockSpec((1, N_KV_HEADS, c, 2 * LANES), lambda bb, i: (bb, 0, 0, 0))
    if mode == 2:
        args = [qp, k, v, kc, vc, extra]
        in_specs = [q_spec, k_spec, v_spec, kc_spec, vc_spec,
                    pl.BlockSpec(extra.shape, lambda bb, i: (0, 0, 0, 0))]
    else:
        args = [qp, qr, k, v, kc, vc]
        in_specs = [q_spec, q_spec, k_spec, v_spec, kc_spec, vc_spec]
        if mode == 0:
            args.append(extra)
            in_specs.append(pl.BlockSpec(memory_space=pltpu.SMEM))
    return pl.pallas_call(
        functools.partial(_attn_kernel, mode=mode, key_chunk=512),
        grid=(b, n // BLOCK_Q),
        in_specs=in_specs,
        out_specs=pl.BlockSpec((1, BLOCK_Q, D_MODEL), lambda bb, i: (bb, i, 0)),
        out_shape=jax.ShapeDtypeStruct((b, n, D_MODEL), BF16),
        compiler_params=_cparams(("parallel", "arbitrary")),
        name=("attn_window", "attn_global", "attn_na")[mode],
    )(*args)


def _ctx_attn_kernel(*refs, has_sink):
    if has_sink:
        q_ref, k_ref, v_ref, sink_ref, o_ref = refs
    else:
        q_ref, k_ref, v_ref, o_ref = refs
    c = k_ref.shape[2]
    rows = Q_PER_KV * c
    pieces = []
    for h in range(N_KV_HEADS):
        q = q_ref[0, h * Q_PER_KV:(h + 1) * Q_PER_KV].reshape(rows, HEAD_DIM)
        m, acc = _attend([(_dot_nt(q, k_ref[0, h]), v_ref[0, h])])
        if has_sink:
            pieces += _head_pair_outputs(m, acc, c, sink_ref, h)
        else:
            pieces += _head_pair_outputs(None, acc, c)
    o_ref[0] = jnp.concatenate(pieces, axis=-1).astype(o_ref.dtype)


def _ctx_attention(qc, kc, vc, sink):
    b, _, c, _ = qc.shape
    args = [qc, kc, vc]
    in_specs = [pl.BlockSpec((1, N_HEADS, c, HEAD_DIM), lambda bb: (bb, 0, 0, 0)),
                pl.BlockSpec((1, N_KV_HEADS, c, HEAD_DIM), lambda bb: (bb, 0, 0, 0)),
                pl.BlockSpec((1, N_KV_HEADS, c, 2 * LANES), lambda bb: (bb, 0, 0, 0))]
    if sink is not None:
        args.append(sink)
        in_specs.append(pl.BlockSpec(memory_space=pltpu.SMEM))
    return pl.pallas_call(
        functools.partial(_ctx_attn_kernel, has_sink=sink is not None),
        grid=(b,),
        in_specs=in_specs,
        out_specs=pl.BlockSpec((1, c, D_MODEL), lambda bb: (bb, 0, 0)),
        out_shape=jax.ShapeDtypeStruct((b, c, D_MODEL), BF16),
        compiler_params=_cparams(("parallel",)),
        name="attn_ctx",
    )(*args)


def _post_attn_kernel(o_ref, x_ref, wo_ref, gt_ref, g_ref, sh_ref, sc_ref, wrh_ref, wrl_ref, br_ref,
                      xn_ref, hf_ref, eid_ref, wcol_ref, cnt_ref):
    xn = x_ref[0] + gt_ref[0] * _dot(o_ref[0], wo_ref[...])
    xn_ref[0] = xn
    hf = _rms_mod(xn, g_ref[0], sh_ref[0], sc_ref[0])
    hf_ref[0] = _pack_rows(hf)
    tm = hf.shape[0]

    hh, hl = _split_bf16(hf)
    lg = _dot(hh, wrh_ref[...]) + _dot(hl, wrh_ref[...]) + _dot(hh, wrl_ref[...]) + br_ref[...]
    lgt = lg.T

    gl = lgt[0:N_GROUPS]
    ge = jnp.exp(gl - jnp.max(gl, axis=0, keepdims=True))
    pg = ge / jnp.sum(ge, axis=0, keepdims=True)
    p_top = jnp.max(pg, axis=0, keepdims=True)
    grow = lax.broadcasted_iota(I32, (N_GROUPS, tm), 0)
    grp = jnp.min(jnp.where(pg == p_top, grow, N_GROUPS), axis=0, keepdims=True)

    el = lgt[8:8 + EXPERTS_PER_GROUP]
    for gi in range(1, N_GROUPS):
        el = jnp.where(grp == gi, lgt[8 + gi * EXPERTS_PER_GROUP:8 + (gi + 1) * EXPERTS_PER_GROUP], el)
    ee = jnp.exp(el - jnp.max(el, axis=0, keepdims=True))
    pe = ee / jnp.sum(ee, axis=0, keepdims=True)
    erow = lax.broadcasted_iota(I32, (EXPERTS_PER_GROUP, tm), 0)
    p1 = jnp.max(pe, axis=0, keepdims=True)
    i1 = jnp.min(jnp.where(pe == p1, erow, EXPERTS_PER_GROUP), axis=0, keepdims=True)
    pe2 = jnp.where(erow == i1, -1.0, pe)
    p2 = jnp.max(pe2, axis=0, keepdims=True)
    i2 = jnp.min(jnp.where(pe2 == p2, erow, EXPERTS_PER_GROUP), axis=0, keepdims=True)
    den = p1 + p2
    w1 = p_top * p1 / den
    w2 = p_top * p2 / den
    e1 = grp * EXPERTS_PER_GROUP + i1
    e2 = grp * EXPERTS_PER_GROUP + i2

    eid_ref[0:1, :] = e1
    eid_ref[1:2, :] = e2
    wrows = jnp.concatenate([w1, w2, jnp.zeros((ROUTER_COLS - 2, tm), F32)], axis=0)
    wcol_ref[...] = wrows.T
    xrow = lax.broadcasted_iota(I32, (N_EXPERTS, tm), 0)
    hits = (xrow == e1).astype(F32) + (xrow == e2).astype(F32)
    cnt_ref[0] = jnp.sum(hits, axis=1, keepdims=True).astype(I32)


def _post_attn(o, x, wo_bf, modr, g_rows, router, layer, ctx_row, tm):
    b, n, d = x.shape
    nt = n // tm
    t_all = b * n
    wrh, wrl, br = router
    tile = lambda bb, t: (bb, t, 0)
    const2 = lambda bb, t: (0, 0)
    return pl.pallas_call(
        _post_attn_kernel,
        grid=(b, nt),
        in_specs=[
            pl.BlockSpec((1, tm, d), tile),
            pl.BlockSpec((1, tm, d), tile),
            pl.BlockSpec((d, d), const2),
            _mod_spec(layer, 2, ctx_row),
            _layer_row_spec(layer),
            _mod_spec(layer, 3, ctx_row),
            _mod_spec(layer, 4, ctx_row),
            pl.BlockSpec((d, ROUTER_COLS), const2),
            pl.BlockSpec((d, ROUTER_COLS), const2),
            pl.BlockSpec((1, ROUTER_COLS), const2),
        ],
        out_specs=[
            pl.BlockSpec((1, tm, d), tile),
            pl.BlockSpec((1, tm, d // 2), tile),
            pl.BlockSpec((2, tm), lambda bb, t: (0, bb * nt + t)),
            pl.BlockSpec((tm, ROUTER_COLS), lambda bb, t: (bb * nt + t, 0)),
            pl.BlockSpec((1, N_EXPERTS, 1), lambda bb, t: (bb * nt + t, 0, 0)),
        ],
        out_shape=[
            jax.ShapeDtypeStruct((b, n, d), F32),
            jax.ShapeDtypeStruct((b, n, d // 2), U32),
            jax.ShapeDtypeStruct((2, t_all), I32),
            jax.ShapeDtypeStruct((t_all, ROUTER_COLS), F32),
            jax.ShapeDtypeStruct((b * nt, N_EXPERTS, 1), I32),
        ],
        compiler_params=_cparams(("parallel", "parallel")),
        name="post_attn_router",
    )(o, x, wo_bf, modr, g_rows, modr, modr, wrh, wrl, br)


def _dest_kernel(eid_ref, base_ref, tri_ref, dest_ref):
    tm = eid_ref.shape[1]
    xrow = lax.broadcasted_iota(I32, (N_EXPERTS, tm), 0)
    oh0 = xrow == eid_ref[0:1, :]
    oh1 = xrow == eid_ref[1:2, :]
    both = oh0.astype(F32) + oh1.astype(F32)
    incl = _dot(both.astype(BF16), tri_ref[...])
    before = incl - both + base_ref[0].astype(F32)
    d0 = jnp.sum(jnp.where(oh0, before, 0.0), axis=0, keepdims=True)
    d1 = jnp.sum(jnp.where(oh1, before, 0.0), axis=0, keepdims=True)
    dest_ref[0, 0:1, :] = d0.astype(I32)
    dest_ref[0, 1:2, :] = d1.astype(I32)


def _dest_rows(eid, base, tri):
    tm = tri.shape[0]
    nt = eid.shape[1] // tm
    return pl.pallas_call(
        _dest_kernel,
        grid=(nt,),
        in_specs=[
            pl.BlockSpec((2, tm), lambda i: (0, i)),
            pl.BlockSpec((1, N_EXPERTS, 1), lambda i: (i, 0, 0)),
            pl.BlockSpec((tm, tm), lambda i: (0, 0)),
        ],
        out_specs=pl.BlockSpec((1, 2, tm), lambda i: (i, 0, 0)),
        out_shape=jax.ShapeDtypeStruct((nt, 2, tm), I32),
        compiler_params=_cparams(("parallel",)),
        name="moe_dest",
    )(eid, base, tri)


def _scatter_kernel(*refs, tiles):
    dest_ref = refs[0]
    hf_refs = refs[1:1 + len(tiles)]
    xs_ref, sem = refs[1 + len(tiles):]
    tm = hf_refs[0].shape[0]
    i = pl.program_id(0)

    def scatter_tile(hf_ref):
        def issue(t8, carry):
            base = pl.multiple_of(t8 * ROW_UNROLL, ROW_UNROLL)
            rows = hf_ref.at[pl.ds(base, ROW_UNROLL)]
            for u in range(ROW_UNROLL):
                for k in range(2):
                    d = dest_ref.at[0, k, pl.ds(base, ROW_UNROLL)][u]
                    pltpu.make_async_copy(rows.at[pl.ds(u, 1)], xs_ref.at[d], sem).start(priority=k)
            return carry

        lax.fori_loop(0, tm // ROW_UNROLL, issue, 0)
        for k in range(2):
            pltpu.make_async_copy(hf_ref, xs_ref.at[pl.ds(0, tm), 0], sem).wait()

    first = 0
    for hf_ref, nt in zip(hf_refs, tiles):
        pl.when((i >= first) & (i < first + nt))(functools.partial(scatter_tile, hf_ref))
        first += nt


def _scatter_rows(dest, hf_streams, n_rows):
    nt_all, _, tm = dest.shape
    d = hf_streams[0].shape[1]
    tiles = tuple(h.shape[0] // tm for h in hf_streams)
    assert sum(tiles) == nt_all
    in_specs = [pl.BlockSpec((1, 2, tm), lambda i: (i, 0, 0), memory_space=pltpu.SMEM)]
    first = 0
    for nt in tiles:
        in_specs.append(pl.BlockSpec((tm, d), lambda i, first=first, nt=nt: (jnp.clip(i - first, 0, nt - 1), 0)))
        first += nt
    return pl.pallas_call(
        functools.partial(_scatter_kernel, tiles=tiles),
        grid=(nt_all,),
        in_specs=in_specs,
        out_specs=pl.BlockSpec(memory_space=pl.ANY),
        out_shape=jax.ShapeDtypeStruct((n_rows, 1, d), hf_streams[0].dtype),
        scratch_shapes=[pltpu.SemaphoreType.DMA(())],
        compiler_params=_cparams(("arbitrary",)),
        name="moe_scatter",
    )(dest, *hf_streams)


def _expert_kernel(ve_ref, vblk_ref, lo_ref, hi_ref, x_hbm, wg_ref, wu_ref, wd_ref, y_hbm, wg_s, wu_s, wd_s,
                   xbuf, xsems, ybuf, ysems, *, n_blocks):
    v = pl.program_id(0)
    n_visits = pl.num_programs(0)
    pv = jnp.maximum(v - 1, 0)
    nv = jnp.minimum(v + 1, n_visits - 1)
    blk = vblk_ref[v]
    new_expert = (v == 0) | (ve_ref[v] != ve_ref[pv])
    first_visit = (v == 0) | (blk != vblk_ref[pv])
    last_visit = (v == n_visits - 1) | (blk != vblk_ref[nv])
    lo = lo_ref[v]
    hi = hi_ref[v]
    xslot = v % 2
    yslot = blk % 2

    def x_copy(visit, slot):
        start = pl.multiple_of(vblk_ref[visit] * EXPERT_BLOCK, EXPERT_BLOCK)
        return pltpu.make_async_copy(x_hbm.at[pl.ds(start, EXPERT_BLOCK), 0], xbuf.at[slot], xsems.at[slot])

    def y_copy(slot):
        start = pl.multiple_of(blk * EXPERT_BLOCK, EXPERT_BLOCK)
        return pltpu.make_async_copy(ybuf.at[slot], y_hbm.at[pl.ds(start, EXPERT_BLOCK), 0], ysems.at[slot])

    @pl.when(v == 0)
    def _():
        x_copy(0, 0).start()

    @pl.when(v + 1 < n_visits)
    def _():
        x_copy(v + 1, 1 - xslot).start()

    @pl.when(new_expert)
    def _():
        wg_s[...] = wg_ref[0, 0].astype(BF16)
        wu_s[...] = wu_ref[0, 0].astype(BF16)
        wd_s[...] = wd_ref[0, 0].astype(BF16)

    @pl.when(first_visit & (blk >= 2))
    def _():
        y_copy(yslot).wait()

    x_copy(v, xslot).wait()

    @pl.when(hi > lo)
    def _():
        x_lo, x_hi = _unpack_rows(xbuf[xslot])
        x_lo = x_lo.astype(BF16)
        x_hi = x_hi.astype(BF16)
        half = x_lo.shape[1]
        gate = _dot(x_lo, wg_s[:half]) + _dot(x_hi, wg_s[half:])
        up = _dot(x_lo, wu_s[:half]) + _dot(x_hi, wu_s[half:])
        mid = gate * jax.nn.sigmoid(gate) * up
        y = _pack_rows(_dot(mid.astype(BF16), wd_s[...]))
        row = lax.broadcasted_iota(I32, (EXPERT_BLOCK, 1), 0)
        mine = (row >= lo) & (row < hi)

        @pl.when(first_visit)
        def _():
            ybuf[yslot] = jnp.where(mine, y, jnp.uint32(0))

        @pl.when(jnp.logical_not(first_visit))
        def _():
            ybuf[yslot] = jnp.where(mine, y, ybuf[yslot])

    @pl.when(last_visit)
    def _():
        y_copy(yslot).start()

    @pl.when(v == n_visits - 1)
    def _():
        y_copy(yslot).wait()
        if n_blocks >= 2:
            y_copy(1 - yslot).wait()


def _experts(visits, xs, w_gate, w_up, w_down, layer):
    n_rows, _, dp = xs.shape
    _, _, d, f = w_gate.shape
    assert d == 2 * dp and n_rows % EXPERT_BLOCK == 0
    ve, vblk, lo, hi = visits
    w_map = lambda v, ve_r, vb_r, lo_r, hi_r: (layer, ve_r[v], 0, 0)
    grid_spec = pltpu.PrefetchScalarGridSpec(
        num_scalar_prefetch=4,
        grid=(ve.shape[0],),
        in_specs=[
            pl.BlockSpec(memory_space=pl.ANY),
            pl.BlockSpec((1, 1, d, f), w_map),
            pl.BlockSpec((1, 1, d, f), w_map),
            pl.BlockSpec((1, 1, f, d), w_map),
        ],
        out_specs=pl.BlockSpec(memory_space=pl.ANY),
        scratch_shapes=[pltpu.VMEM((d, f), BF16), pltpu.VMEM((d, f), BF16), pltpu.VMEM((f, d), BF16),
                        pltpu.VMEM((2, EXPERT_BLOCK, dp), U32), pltpu.SemaphoreType.DMA((2,)),
                        pltpu.VMEM((2, EXPERT_BLOCK, dp), U32), pltpu.SemaphoreType.DMA((2,))],
    )
    return pl.pallas_call(
        functools.partial(_expert_kernel, n_blocks=n_rows // EXPERT_BLOCK),
        grid_spec=grid_spec,
        out_shape=jax.ShapeDtypeStruct((n_rows, 1, dp), U32),
        compiler_params=_cparams(("arbitrary",)),
        name="moe_experts",
    )(ve, vblk, lo, hi, xs, w_gate, w_up, w_down)


def _combine_kernel(*refs, final_norm, qkv):
    dest_ref, dest_next_ref, xn_ref, wcol_ref, gt_ref = refs[:5]
    pos = 5
    if final_norm:
        gfin_ref = refs[pos]
        pos += 1
    y_ref = refs[pos]
    ybuf, sems = refs[-2:]
    rest = refs[pos + 1:-2]
    n_qkv_in = 0 if qkv is None else 4 + 2 * qkv["rope"] + 3 * qkv["qk_norm"]
    o_ref = rest[n_qkv_in]
    tm = xn_ref.shape[1]
    step = pl.program_id(0) * pl.num_programs(1) + pl.program_id(1)
    n_steps = pl.num_programs(0) * pl.num_programs(1)
    slot = step % 2

    def gather_tile(d_ref, into):
        def issue(t8, carry):
            base = pl.multiple_of(t8 * ROW_UNROLL, ROW_UNROLL)
            for k in range(2):
                rows = ybuf.at[into, k, pl.ds(base, ROW_UNROLL)]
                for u in range(ROW_UNROLL):
                    d = d_ref.at[0, k, pl.ds(base, ROW_UNROLL)][u]
                    pltpu.make_async_copy(y_ref.at[d], rows.at[pl.ds(u, 1)], sems.at[into]).start(priority=k)
            return carry

        lax.fori_loop(0, tm // ROW_UNROLL, issue, 0)

    pl.when(step == 0)(functools.partial(gather_tile, dest_ref, 0))
    pl.when(step + 1 < n_steps)(functools.partial(gather_tile, dest_next_ref, 1 - slot))

    for k in range(2):
        pltpu.make_async_copy(y_ref.at[pl.ds(0, tm), 0], ybuf.at[slot, k], sems.at[slot]).wait()
    w = wcol_ref[...]
    lo0, hi0 = _unpack_rows(ybuf[slot, 0])
    lo1, hi1 = _unpack_rows(ybuf[slot, 1])
    moe = jnp.concatenate([w[:, 0:1] * lo0 + w[:, 1:2] * lo1, w[:, 0:1] * hi0 + w[:, 1:2] * hi1], axis=1)
    out = xn_ref[0] + gt_ref[0] * moe
    if final_norm:
        ms = jnp.mean(out * out, axis=-1, keepdims=True)
        out = out * lax.rsqrt(ms + NORM_EPS) * gfin_ref[...]
    o_ref[0] = out
    if qkv is not None:
        _qkv_project(out, rest[:n_qkv_in] + rest[n_qkv_in + 1:], **qkv)


def _combine(dest, xn, wcol, modr, y, layer, ctx_row, g_final=None, next_qkv=None):
    b, n, d = xn.shape
    nt_all, _, tm = dest.shape
    nt = n // tm
    final_norm = g_final is not None
    args = [dest, dest, xn, wcol, modr]
    in_specs = [
        pl.BlockSpec((1, 2, tm), lambda bb, t: (bb * nt + t, 0, 0), memory_space=pltpu.SMEM),
        pl.BlockSpec((1, 2, tm), lambda bb, t: (jnp.minimum(bb * nt + t + 1, nt_all - 1), 0, 0),
                     memory_space=pltpu.SMEM),
        pl.BlockSpec((1, tm, d), lambda bb, t: (bb, t, 0)),
        pl.BlockSpec((tm, ROUTER_COLS), lambda bb, t: (bb * nt + t, 0)),
        _mod_spec(layer, 5, ctx_row),
    ]
    if final_norm:
        args.append(g_final.reshape(1, d))
        in_specs.append(pl.BlockSpec((1, d), lambda bb, t: (0, 0)))
    args.append(y)
    in_specs.append(pl.BlockSpec(memory_space=pl.ANY))
    out_shape = [jax.ShapeDtypeStruct((b, n, d), F32)]
    out_specs = [pl.BlockSpec((1, tm, d), lambda bb, t: (bb, t, 0))]
    qkv = None
    if next_qkv is not None:
        g_rows, w_bf, next_layer, rope_tabs, qk_gains = next_qkv
        q_args, q_in_specs, q_out_shape, q_out_specs = _qkv_operands(b, n, modr, g_rows, w_bf, next_layer, ctx_row,
                                                                     tm, rope_tabs, qk_gains)
        args += q_args
        in_specs += q_in_specs
        out_shape += q_out_shape
        out_specs += q_out_specs
        qkv = dict(qk_norm=qk_gains is not None, rope=rope_tabs is not None)
    outs = pl.pallas_call(
        functools.partial(_combine_kernel, final_norm=final_norm, qkv=qkv),
        grid=(b, nt),
        in_specs=in_specs,
        out_specs=out_specs,
        out_shape=out_shape,
        scratch_shapes=[pltpu.VMEM((2, 2, tm, y.shape[2]), y.dtype), pltpu.SemaphoreType.DMA((2,))],
        compiler_params=_cparams(("arbitrary", "arbitrary")),
        name="moe_combine" if qkv is None else "moe_combine_qkv",
    )(*args)
    return outs[0] if qkv is None else outs


def _rope_tables(n):
    t = np.arange(n)
    row = (t // GRID_W).astype(np.float32)
    col = (t % GRID_W).astype(np.float32)
    quarter = HEAD_DIM // 4
    inv = jnp.asarray(ROPE_THETA, F32) ** (-jnp.arange(quarter, dtype=F32) / quarter)
    ar = jnp.asarray(row)[:, None] * inv
    ac = jnp.asarray(col)[:, None] * inv
    ang = jnp.concatenate([ar, ar, ac, ac], axis=-1)
    ang = jnp.concatenate([ang, ang], axis=-1)
    sign = np.where((np.arange(LANES) % (HEAD_DIM // 2)) < quarter, -1.0, 1.0).astype(np.float32)
    return jnp.cos(ang), jnp.sin(ang) * sign


def _na_bias_tables(rpb):
    n_dr, n_dc = 2 * NA_WIN_H - 1, 2 * NA_WIN_W - 1
    blocked = jnp.full((N_HEADS, 2, n_dc), NEG_INF, F32)
    rows = jnp.concatenate([blocked, rpb.astype(F32) * LOG2E, blocked], axis=1)
    assert rows.shape[1] == n_dr + 4 == NA_TABLE + 1
    pair = jnp.stack([rows[:, 0:NA_TABLE], rows[:, 1:NA_TABLE + 1]], axis=2)
    c = np.arange(GRID_W)[:, None]
    kc = np.arange(GRID_W)[None, :]
    cs = np.clip(c - NA_WIN_W // 2, 0, GRID_W - NA_WIN_W)
    in_window = (kc >= cs) & (kc < cs + NA_WIN_W)
    out = jnp.full((N_HEADS, NA_TABLE, GRID_W, 2, GRID_W), NEG_INF, F32)
    for dc in range(-(NA_WIN_W - 1), NA_WIN_W):
        hit = ((kc - c) == dc) & in_window
        if hit.any():
            out = jnp.where(hit[None, None, :, None, :], pair[:, :, None, :, dc + NA_WIN_W - 1, None], out)
    return out.reshape(N_HEADS, NA_TABLE, GRID_W, LANES)


def _router_tables(w_group, b_group, w_router, b_router):
    d = w_group.shape[0]
    w = jnp.zeros((d, ROUTER_COLS), F32)
    w = w.at[:, 0:N_GROUPS].set(w_group).at[:, 8:8 + N_EXPERTS].set(w_router)
    bias = jnp.zeros((1, ROUTER_COLS), F32)
    bias = bias.at[0, 0:N_GROUPS].set(b_group).at[0, 8:8 + N_EXPERTS].set(b_router)
    hi = w.astype(BF16)
    lo = (w - hi.astype(F32)).astype(BF16)
    return hi, lo, bias


def _pick(table, onehot):
    return jnp.sum(jnp.where(onehot, table[None, :], 0), axis=1)


def _moe_layout(cnt_tiles, n_rows):
    cnt = cnt_tiles[:, :, 0]
    counts = jnp.sum(cnt, axis=0)
    ends = jnp.cumsum(counts)
    starts = ends - counts
    base = starts[None, :] + jnp.cumsum(cnt, axis=0) - cnt
    first_blk = starts // EXPERT_BLOCK
    n_blk = jnp.where(counts > 0, (ends - 1) // EXPERT_BLOCK - first_blk + 1, 0)
    v_end = jnp.cumsum(n_blk)
    v_off = v_end - n_blk
    n_visits = n_rows // EXPERT_BLOCK + N_EXPERTS
    v = jnp.arange(n_visits, dtype=I32)
    valid = v < v_end[-1]
    vv = jnp.minimum(v, v_end[-1] - 1)
    ve = jnp.sum((v_end[None, :] <= vv[:, None]).astype(I32), axis=1)
    onehot = ve[:, None] == jnp.arange(N_EXPERTS, dtype=I32)[None, :]
    vblk = _pick(first_blk, onehot) + vv - _pick(v_off, onehot)
    lo = jnp.clip(_pick(starts, onehot) - vblk * EXPERT_BLOCK, 0, EXPERT_BLOCK)
    hi = jnp.clip(_pick(ends, onehot) - vblk * EXPERT_BLOCK, 0, EXPERT_BLOCK)
    hi = jnp.where(valid, hi, lo)
    visits = tuple(a.astype(I32) for a in (ve, vblk, lo, hi))
    return base[:, :, None].astype(I32), visits


def kernel(x, c, ctx, c_ctx, w_ada, b_ada, g_attn, w_qkv, w_o, sink_a, gq_b, gk_b, rpb_c, g_ffn,
           w_group, b_group, w_router, b_router, w_gate, w_up, w_down, g_final):
    b, n, d = x.shape
    n_ctx = ctx.shape[1]
    depth = w_ada.shape[0]
    tm_tok = 512
    tm_ctx = 256
    assert b + 1 <= MOD_ROWS and d == D_MODEL and n % tm_tok == 0 and n_ctx % tm_ctx == 0
    assert (b * n_ctx) % tm_tok == 0
    ctx_row = b

    cc = jnp.zeros((MOD_ROWS, d), F32).at[:b].set(c).at[b].set(c_ctx)
    modr = _ada_all(cc, w_ada, b_ada).reshape(depth * MOD_ROWS * N_MOD, 1, d)
    g_attn_rows = g_attn.reshape(depth, 1, d)
    g_ffn_rows = g_ffn.reshape(depth, 1, d)
    rope_tabs = _rope_tables(n)
    head_sum = jnp.asarray(np.kron(np.eye(LANES // HEAD_DIM), np.ones((HEAD_DIM, HEAD_DIM))), BF16)
    tri = jnp.asarray(np.triu(np.ones((tm_tok, tm_tok))), BF16)

    def layer_qkv_config(layer):
        mixer, idx = layer % N_MIXERS, layer // N_MIXERS
        gains = None
        if mixer == 1:
            gains = (jnp.tile(gq_b[idx], LANES // HEAD_DIM).reshape(1, LANES),
                     jnp.tile(gk_b[idx], LANES // HEAD_DIM).reshape(1, LANES), head_sum)
        return w_qkv[layer].astype(BF16), (rope_tabs if mixer != 2 else None), gains

    xc = ctx
    latent_qkv = None
    for i in range(depth):
        m = i % N_MIXERS
        j = i // N_MIXERS
        last = i == depth - 1
        wqkv_bf, rope_i, qk_gains = layer_qkv_config(i)
        wo_bf = w_o[i].astype(BF16)

        if latent_qkv is None:
            latent_qkv = _qkv(x, modr, g_attn_rows, wqkv_bf, i, None, tm_tok, rope_i, qk_gains)
        if m == 2:
            qp, k, v = latent_qkv
            qr = None
        else:
            qp, qr, k, v = latent_qkv
        qc, kc, vc = _qkv(xc, modr, g_attn_rows, wqkv_bf, i, ctx_row, tm_ctx, None, qk_gains)
        sink = sink_a[j] if m == 0 else None
        extra = sink if m == 0 else (_na_bias_tables(rpb_c[j]) if m == 2 else None)
        o = _attention(m, qp, qr, k, v, kc, vc, extra)

        router = _router_tables(w_group[i], b_group[i], w_router[i], b_router[i])
        streams = [(o, x, None)]
        if not last:
            oc = _ctx_attention(qc, kc, vc, sink)
            streams.append((oc.reshape(1, b * n_ctx, d), xc.reshape(1, b * n_ctx, d), ctx_row))
        routed = [_post_attn(o_s, x_s, wo_bf, modr, g_ffn_rows, router, i, row, tm_tok) for (o_s, x_s, row) in streams]

        n_rows = 2 * sum(r[2].shape[1] for r in routed)
        assert n_rows % EXPERT_BLOCK == 0
        base, visits = _moe_layout(jnp.concatenate([r[4] for r in routed], axis=0), n_rows)
        dest = _dest_rows(jnp.concatenate([r[2] for r in routed], axis=1), base, tri)
        xs = _scatter_rows(dest, [r[1].reshape(-1, d // 2) for r in routed], n_rows)
        y = _experts(visits, xs, w_gate, w_up, w_down, i)
        nt_lat = routed[0][4].shape[0]
        if last:
            x = _combine(dest[:nt_lat], routed[0][0], routed[0][3], modr, y, i, None, g_final=g_final)
        else:
            w_next, rope_next, gains_next = layer_qkv_config(i + 1)
            x, *latent_qkv = _combine(dest[:nt_lat], routed[0][0], routed[0][3], modr, y, i, None,
                                      next_qkv=(g_attn_rows, w_next, i + 1, rope_next, gains_next))
            xc = _combine(dest[nt_lat:], routed[1][0], routed[1][3], modr, y, i, ctx_row).reshape(b, n_ctx, d)
    return x
```

```python
import functools

import numpy as np
import jax
import jax.numpy as jnp
from jax import lax
from jax.experimental import pallas as pl
from jax.experimental.pallas import tpu as pltpu

F32 = jnp.float32
BF16 = jnp.bfloat16
I32 = jnp.int32
U32 = jnp.uint32

D_MODEL = 1024
GRID_W = 64
N_MIXERS = 3
N_HEADS = 16
N_KV_HEADS = 4
HEAD_DIM = D_MODEL // N_HEADS
Q_PER_KV = N_HEADS // N_KV_HEADS
QKV_DIM = (N_HEADS + 2 * N_KV_HEADS) * HEAD_DIM
ATTN_SCALE = HEAD_DIM ** -0.5
ROPE_THETA = 10000.0
ATTN_BLOCK_Q = (128, 256, 128)
WINDOW = 128
NA_WIN_H = 8
NA_WIN_W = 16
N_GROUPS = 4
EXPERTS_PER_GROUP = 8
N_EXPERTS = N_GROUPS * EXPERTS_PER_GROUP
EXPERT_HIDDEN = D_MODEL // 2
NORM_EPS = 1e-6
NEG_INF = -1e30
LOG2E = 1.4426950408889634

LANES = 128
MOD_ROWS = 24
N_MOD = 6
EXPERT_BLOCK = 512
ROW_UNROLL = 8
ROUTER_COLS = LANES
NA_KEY_ROWS = 10
NA_TABLE = 18
VMEM_LIMIT = 56 * 1024 * 1024


def _cparams(sem, vmem=VMEM_LIMIT):
    return pltpu.CompilerParams(dimension_semantics=sem, vmem_limit_bytes=vmem)


def _dot(a, b):
    return jnp.dot(a, b, preferred_element_type=F32)


def _dot_nt(a, b):
    return lax.dot_general(a, b, (((1,), (1,)), ((), ())), preferred_element_type=F32)


def _split_bf16(x):
    hi = x.astype(BF16)
    lo = (x - hi.astype(F32)).astype(BF16)
    return hi, lo


def _pack_rows(x):
    w = x.shape[1] // 2
    bits = pltpu.bitcast(x.astype(BF16).astype(F32), U32)
    return (bits[:, w:] & jnp.uint32(0xFFFF0000)) | (bits[:, :w] >> 16)


def _unpack_rows(p):
    lo = pltpu.bitcast(p << 16, F32)
    hi = pltpu.bitcast(p & jnp.uint32(0xFFFF0000), F32)
    return lo, hi


def _rms_mod(x, g, sh, sc):
    ms = jnp.mean(x * x, axis=-1, keepdims=True)
    y = x * lax.rsqrt(ms + NORM_EPS) * g
    return y * (1.0 + sc) + sh


def _ada_kernel(c_ref, w_ref, b_ref, o_ref):
    c = c_ref[...]
    s = c * jax.nn.sigmoid(c)
    o_ref[0] = _dot(s.astype(BF16), w_ref[0].astype(BF16)) + b_ref[0]


def _ada_all(cc, w_ada, b_ada):
    depth, d, d6 = w_ada.shape
    tn = 1536
    return pl.pallas_call(
        _ada_kernel,
        grid=(depth, d6 // tn),
        in_specs=[
            pl.BlockSpec((MOD_ROWS, d), lambda i, j: (0, 0)),
            pl.BlockSpec((1, d, tn), lambda i, j: (i, 0, j)),
            pl.BlockSpec((1, 1, tn), lambda i, j: (i, 0, j)),
        ],
        out_specs=pl.BlockSpec((1, MOD_ROWS, tn), lambda i, j: (i, 0, j)),
        out_shape=jax.ShapeDtypeStruct((depth, MOD_ROWS, d6), F32),
        compiler_params=_cparams(("parallel", "parallel")),
        name="ada_mod",
    )(cc, w_ada, b_ada.reshape(depth, 1, d6))


def _mod_spec(layer, chunk, ctx_row):
    if ctx_row is None:
        return pl.BlockSpec((1, 1, D_MODEL), lambda b, t: ((layer * MOD_ROWS + b) * N_MOD + chunk, 0, 0))
    return pl.BlockSpec((1, 1, D_MODEL), lambda b, t: ((layer * MOD_ROWS + ctx_row) * N_MOD + chunk, 0, 0))


def _layer_row_spec(layer):
    return pl.BlockSpec((1, 1, D_MODEL), lambda b, t: (layer, 0, 0))


def _qkv_kernel(*refs, qk_norm, rope):
    _qkv_project(refs[0][0], refs[1:], qk_norm=qk_norm, rope=rope)


def _qkv_project(x, refs, *, qk_norm, rope):
    g_ref, sh_ref, sc_ref, w_ref = refs[:4]
    pos = 4
    if rope:
        cos_ref, sin_ref = refs[pos:pos + 2]
        pos += 2
    if qk_norm:
        gq_ref, gk_ref, bm_ref = refs[pos:pos + 3]
        pos += 3
    if rope:
        qp_ref, qr_ref, k_ref, v_ref = refs[pos:]
    else:
        qp_ref, k_ref, v_ref = refs[pos:]

    h = _rms_mod(x, g_ref[0], sh_ref[0], sc_ref[0])
    res = _dot(h.astype(BF16), w_ref[...])
    half = LANES // 2
    n_q = N_HEADS * HEAD_DIM // LANES
    n_k = N_KV_HEADS * HEAD_DIM // LANES
    if rope:
        lane = lax.broadcasted_iota(I32, (1, LANES), 1)
        first_half = (lane & (HEAD_DIM // 2 - 1)) < (HEAD_DIM // 4)
    for cb in range(QKV_DIM // LANES):
        xs = res[:, cb * LANES:(cb + 1) * LANES]
        is_q = cb < n_q
        is_k = n_q <= cb < n_q + n_k
        if qk_norm and (is_q or is_k):
            hi, lo = _split_bf16(xs * xs)
            ms = (_dot(hi, bm_ref[...]) + _dot(lo, bm_ref[...])) * (1.0 / HEAD_DIM)
            gain = gq_ref[...] if is_q else gk_ref[...]
            xs = xs * lax.rsqrt(ms + NORM_EPS) * gain
        if is_q:
            xs = xs * (ATTN_SCALE * LOG2E)
        if rope and (is_q or is_k):
            rot = jnp.where(first_half, pltpu.roll(xs, LANES - HEAD_DIM // 4, 1), pltpu.roll(xs, HEAD_DIM // 4, 1))
            xr = xs * cos_ref[...] + rot * sin_ref[...]
        if is_q:
            hd = 2 * cb
            qp_ref[0, hd] = xs[:, :half].astype(BF16)
            qp_ref[0, hd + 1] = xs[:, half:].astype(BF16)
            if rope:
                qr_ref[0, hd] = xr[:, :half].astype(BF16)
                qr_ref[0, hd + 1] = xr[:, half:].astype(BF16)
        elif is_k:
            hd = 2 * (cb - n_q)
            kk = xr if rope else xs
            k_ref[0, hd] = kk[:, :half].astype(BF16)
            k_ref[0, hd + 1] = kk[:, half:].astype(BF16)
        else:
            hd = 2 * (cb - n_q - n_k)
            low = lax.broadcasted_iota(I32, (1, LANES), 1) < HEAD_DIM
            swapped = pltpu.roll(xs, half, 1)
            for par, (a, b) in enumerate(((xs, swapped), (swapped, xs))):
                v_aug = jnp.concatenate([jnp.where(low, a, 1.0), jnp.where(low, 1.0, b)], axis=1)
                v_ref[0, hd + par] = v_aug.astype(BF16)


def _qkv_operands(b, n, modr, g_rows, w_bf, layer, ctx_row, tm, rope_tabs, qk_gains):
    d = D_MODEL
    rope = rope_tabs is not None
    args = [g_rows, modr, modr, w_bf]
    in_specs = [
        _layer_row_spec(layer),
        _mod_spec(layer, 0, ctx_row),
        _mod_spec(layer, 1, ctx_row),
        pl.BlockSpec((d, QKV_DIM), lambda bb, t: (0, 0)),
    ]
    if rope:
        args += list(rope_tabs)
        in_specs += [pl.BlockSpec((tm, LANES), lambda bb, t: (t, 0))] * 2
    if qk_gains is not None:
        args += list(qk_gains)
        in_specs += [pl.BlockSpec((1, LANES), lambda bb, t: (0, 0))] * 2
        in_specs += [pl.BlockSpec((LANES, LANES), lambda bb, t: (0, 0))]
    q_shape = jax.ShapeDtypeStruct((b, N_HEADS, n, HEAD_DIM), BF16)
    kv_shape = jax.ShapeDtypeStruct((b, N_KV_HEADS, n, HEAD_DIM), BF16)
    q_spec = pl.BlockSpec((1, N_HEADS, tm, HEAD_DIM), lambda bb, t: (bb, 0, t, 0))
    kv_spec = pl.BlockSpec((1, N_KV_HEADS, tm, HEAD_DIM), lambda bb, t: (bb, 0, t, 0))
    v_shape = jax.ShapeDtypeStruct((b, N_KV_HEADS, n, 2 * LANES), BF16)
    v_spec = pl.BlockSpec((1, N_KV_HEADS, tm, 2 * LANES), lambda bb, t: (bb, 0, t, 0))
    if rope:
        out_shape, out_specs = [q_shape, q_shape, kv_shape, v_shape], [q_spec, q_spec, kv_spec, v_spec]
    else:
        out_shape, out_specs = [q_shape, kv_shape, v_shape], [q_spec, kv_spec, v_spec]
    return args, in_specs, out_shape, out_specs


def _qkv(x, modr, g_rows, w_bf, layer, ctx_row, tm, rope_tabs=None, qk_gains=None):
    b, n, d = x.shape
    args, in_specs, out_shape, out_specs = _qkv_operands(b, n, modr, g_rows, w_bf, layer, ctx_row, tm,
                                                         rope_tabs, qk_gains)
    return pl.pallas_call(
        functools.partial(_qkv_kernel, qk_norm=qk_gains is not None, rope=rope_tabs is not None),
        grid=(b, n // tm),
        in_specs=[pl.BlockSpec((1, tm, d), lambda bb, t: (bb, t, 0))] + in_specs,
        out_specs=out_specs,
        out_shape=out_shape,
        compiler_params=_cparams(("parallel", "parallel")),
        name="qkv_rope" if rope_tabs is not None else "qkv",
    )(x, *args)


def _attend(pieces, m_prev=None, acc_prev=None):
    m = m_prev
    for s, _ in pieces:
        m_s = jnp.max(s, axis=-1, keepdims=True)
        m = m_s if m is None else jnp.maximum(m, m_s)
    acc = None if acc_prev is None else jnp.exp2(m_prev - m) * acc_prev
    for s, v_aug in pieces:
        pv = _dot(jnp.exp2(s - m).astype(BF16), v_aug)
        acc = pv if acc is None else acc + pv
    return m, acc


def _head_pair_outputs(m, acc, rows_per_head, sink_ref=None, kv_head=None):
    low = lax.broadcasted_iota(I32, (1, LANES), 1) < HEAD_DIM
    blocks = []
    for pair in range(Q_PER_KV // 2):
        forms = []
        for par in range(2):
            g = 2 * pair + par
            rows = slice(g * rows_per_head, (g + 1) * rows_per_head)
            num = acc[rows, par * LANES:(par + 1) * LANES]
            den = acc[rows, (1 - par) * LANES:(2 - par) * LANES]
            if sink_ref is not None:
                sink = sink_ref[kv_head * Q_PER_KV + g] * LOG2E
                m2 = jnp.maximum(m[rows], sink)
                scale = jnp.exp2(m[rows] - m2)
                num = num * scale
                den = den * scale + jnp.exp2(sink - m2)
            forms.append(num / den)
        blocks.append(jnp.where(low, forms[0], forms[1]))
    return blocks


def _attn_kernel(*refs, mode, bq, key_chunk):
    if mode == 0:
        qp_ref, qr_ref, k_ref, v_ref, kc_ref, vc_ref, sink_ref, o_ref = refs
    elif mode == 1:
        qp_ref, qr_ref, k_ref, v_ref, kc_ref, vc_ref, o_ref = refs
    else:
        qp_ref, k_ref, v_ref, kc_ref, vc_ref, t2_ref, o_ref = refs
    qb = pl.program_id(1)
    n = k_ref.shape[2]
    rows = Q_PER_KV * bq

    if mode == 0:
        span = bq + 2 * WINDOW
        kstart = pl.multiple_of(jnp.clip(qb * bq - WINDOW, 0, n - span), WINDOW)
        qpos = qb * bq + (lax.broadcasted_iota(I32, (rows, span), 0) & (bq - 1))
        kpos = kstart + lax.broadcasted_iota(I32, (rows, span), 1)
        band = jnp.abs(kpos - qpos) <= WINDOW
    elif mode == 2:
        rows_grid = n // GRID_W
        r0 = qb * (bq // GRID_W)
        wr = jnp.clip(r0 - NA_WIN_H // 2, 0, rows_grid - NA_KEY_ROWS)
        kstart = pl.multiple_of(wr * GRID_W, GRID_W)
        lane = lax.broadcasted_iota(I32, (GRID_W, LANES), 1)

    pieces = []
    for h in range(N_KV_HEADS):
        qp = qp_ref[0, h * Q_PER_KV:(h + 1) * Q_PER_KV].reshape(rows, HEAD_DIM)
        ctx_piece = (_dot_nt(qp, kc_ref[0, h]), vc_ref[0, h])
        if mode == 0:
            qr = qr_ref[0, h * Q_PER_KV:(h + 1) * Q_PER_KV].reshape(rows, HEAD_DIM)
            s = jnp.where(band, _dot_nt(qr, k_ref[0, h, pl.ds(kstart, span), :]), NEG_INF)
            m, acc = _attend([(s, v_ref[0, h, pl.ds(kstart, span), :]), ctx_piece])
            pieces += _head_pair_outputs(m, acc, bq, sink_ref, h)
        elif mode == 1:
            qr = qr_ref[0, h * Q_PER_KV:(h + 1) * Q_PER_KV].reshape(rows, HEAD_DIM)
            n_chunks = n // key_chunk
            m, acc = None, None
            for c in range(n_chunks):
                keys = slice(c * key_chunk, (c + 1) * key_chunk)
                step = [(_dot_nt(qr, k_ref[0, h, keys, :]), v_ref[0, h, keys, :])]
                if c == n_chunks - 1:
                    step.append(ctx_piece)
                m, acc = _attend(step, m, acc)
            pieces += _head_pair_outputs(None, acc, bq)
        else:
            n_keys = NA_KEY_ROWS * GRID_W
            s = _dot_nt(qp, k_ref[0, h, pl.ds(kstart, n_keys), :])
            s_ctx, v_ctx = ctx_piece
            p_lat, p_ctx = [], []
            for g in range(Q_PER_KV):
                for a in range(bq // GRID_W):
                    r = r0 + a
                    rs = jnp.clip(r - NA_WIN_H // 2, 0, rows_grid - NA_WIN_H)
                    tiles = []
                    for j in range(n_keys // LANES):
                        kr0 = wr + 2 * j
                        tab = t2_ref[h * Q_PER_KV + g, kr0 - r + NA_TABLE // 2]
                        pen0 = jnp.where((kr0 >= rs) & (kr0 < rs + NA_WIN_H), 0.0, NEG_INF)
                        pen1 = jnp.where((kr0 + 1 >= rs) & (kr0 + 1 < rs + NA_WIN_H), 0.0, NEG_INF)
                        tiles.append(tab + jnp.where(lane < GRID_W, pen0, pen1))
                    qrows = slice(g * bq + a * GRID_W, g * bq + (a + 1) * GRID_W)
                    sb = s[qrows] + jnp.concatenate(tiles, axis=1)
                    sc = s_ctx[qrows]
                    m = jnp.maximum(jnp.max(sb, axis=-1, keepdims=True), jnp.max(sc, axis=-1, keepdims=True))
                    p_lat.append(jnp.exp2(sb - m).astype(BF16))
                    p_ctx.append(jnp.exp2(sc - m).astype(BF16))
            acc = (_dot(jnp.concatenate(p_lat, axis=0), v_ref[0, h, pl.ds(kstart, n_keys), :])
                   + _dot(jnp.concatenate(p_ctx, axis=0), v_ctx))
            pieces += _head_pair_outputs(None, acc, bq)
    o_ref[0] = jnp.concatenate(pieces, axis=-1).astype(o_ref.dtype)


def _attention(mode, qp, qr, k, v, kc, vc, extra):
    b, _, n, _ = qp.shape
    c = kc.shape[2]
    bq = ATTN_BLOCK_Q[mode]
    q_spec = pl.BlockSpec((1, N_HEADS, bq, HEAD_DIM), lambda bb, i: (bb, 0, i, 0))
    k_spec = pl.BlockSpec((1, N_KV_HEADS, n, HEAD_DIM), lambda bb, i: (bb, 0, 0, 0))
    v_spec = pl.BlockSpec((1, N_KV_HEADS, n, 2 * LANES), lambda bb, i: (bb, 0, 0, 0))
    kc_spec = pl.BlockSpec((1, N_KV_HEADS, c, HEAD_DIM), lambda bb, i: (bb, 0, 0, 0))
    vc_spec = pl.BlockSpec((1, N_KV_HEADS, c, 2 * LANES), lambda bb, i: (bb, 0, 0, 0))
    if mode == 2:
        args = [qp, k, v, kc, vc, extra]
        in_specs = [q_spec, k_spec, v_spec, kc_spec, vc_spec,
                    pl.BlockSpec(extra.shape, lambda bb, i: (0, 0, 0, 0))]
    else:
        args = [qp, qr, k, v, kc, vc]
        in_specs = [q_spec, q_spec, k_spec, v_spec, kc_spec, vc_spec]
        if mode == 0:
            args.append(extra)
            in_specs.append(pl.BlockSpec(memory_space=pltpu.SMEM))
    return pl.pallas_call(
        functools.partial(_attn_kernel, mode=mode, bq=bq, key_chunk=512),
        grid=(b, n // bq),
        in_specs=in_specs,
        out_specs=pl.BlockSpec((1, bq, D_MODEL), lambda bb, i: (bb, i, 0)),
        out_shape=jax.ShapeDtypeStruct((b, n, D_MODEL), BF16),
        compiler_params=_cparams(("parallel", "arbitrary")),
        name=("attn_window", "attn_global", "attn_na")[mode],
    )(*args)


def _ctx_attn_kernel(*refs, has_sink):
    if has_sink:
        q_ref, k_ref, v_ref, sink_ref, o_ref = refs
    else:
        q_ref, k_ref, v_ref, o_ref = refs
    c = k_ref.shape[2]
    rows = Q_PER_KV * c
    pieces = []
    for h in range(N_KV_HEADS):
        q = q_ref[0, h * Q_PER_KV:(h + 1) * Q_PER_KV].reshape(rows, HEAD_DIM)
        m, acc = _attend([(_dot_nt(q, k_ref[0, h]), v_ref[0, h])])
        if has_sink:
            pieces += _head_pair_outputs(m, acc, c, sink_ref, h)
        else:
            pieces += _head_pair_outputs(None, acc, c)
    o_ref[0] = jnp.concatenate(pieces, axis=-1).astype(o_ref.dtype)


def _ctx_attention(qc, kc, vc, sink):
    b, _, c, _ = qc.shape
    args = [qc, kc, vc]
    in_specs = [pl.BlockSpec((1, N_HEADS, c, HEAD_DIM), lambda bb: (bb, 0, 0, 0)),
                pl.BlockSpec((1, N_KV_HEADS, c, HEAD_DIM), lambda bb: (bb, 0, 0, 0)),
                pl.BlockSpec((1, N_KV_HEADS, c, 2 * LANES), lambda bb: (bb, 0, 0, 0))]
    if sink is not None:
        args.append(sink)
        in_specs.append(pl.BlockSpec(memory_space=pltpu.SMEM))
    return pl.pallas_call(
        functools.partial(_ctx_attn_kernel, has_sink=sink is not None),
        grid=(b,),
        in_specs=in_specs,
        out_specs=pl.BlockSpec((1, c, D_MODEL), lambda bb: (bb, 0, 0)),
        out_shape=jax.ShapeDtypeStruct((b, c, D_MODEL), BF16),
        compiler_params=_cparams(("parallel",)),
        name="attn_ctx",
    )(*args)


def _post_attn_kernel(o_ref, x_ref, wo_ref, gt_ref, g_ref, sh_ref, sc_ref, wr_hilo_ref, wr_hi_ref, br_ref,
                      xn_ref, hf_ref, eid_ref, wcol_ref, cnt_ref):
    xn = x_ref[0] + gt_ref[0] * _dot(o_ref[0], wo_ref[...])
    xn_ref[0] = xn
    hf = _rms_mod(xn, g_ref[0], sh_ref[0], sc_ref[0])
    hf_ref[0] = _pack_rows(hf)
    tm = hf.shape[0]

    hh, hl = _split_bf16(hf)
    both = _dot(hh, wr_hilo_ref[...])
    lg = both[:, :ROUTER_COLS] + both[:, ROUTER_COLS:] + _dot(hl, wr_hi_ref[...]) + br_ref[...]
    lgt = lg.T

    gl = lgt[0:N_GROUPS]
    ge = jnp.exp(gl - jnp.max(gl, axis=0, keepdims=True))
    pg = ge / jnp.sum(ge, axis=0, keepdims=True)
    p_top = jnp.max(pg, axis=0, keepdims=True)
    grow = lax.broadcasted_iota(I32, (N_GROUPS, tm), 0)
    grp = jnp.min(jnp.where(pg == p_top, grow, N_GROUPS), axis=0, keepdims=True)

    el = lgt[8:8 + EXPERTS_PER_GROUP]
    for gi in range(1, N_GROUPS):
        el = jnp.where(grp == gi, lgt[8 + gi * EXPERTS_PER_GROUP:8 + (gi + 1) * EXPERTS_PER_GROUP], el)
    ee = jnp.exp(el - jnp.max(el, axis=0, keepdims=True))
    pe = ee / jnp.sum(ee, axis=0, keepdims=True)
    erow = lax.broadcasted_iota(I32, (EXPERTS_PER_GROUP, tm), 0)
    p1 = jnp.max(pe, axis=0, keepdims=True)
    i1 = jnp.min(jnp.where(pe == p1, erow, EXPERTS_PER_GROUP), axis=0, keepdims=True)
    pe2 = jnp.where(erow == i1, -1.0, pe)
    p2 = jnp.max(pe2, axis=0, keepdims=True)
    i2 = jnp.min(jnp.where(pe2 == p2, erow, EXPERTS_PER_GROUP), axis=0, keepdims=True)
    den = p1 + p2
    w1 = p_top * p1 / den
    w2 = p_top * p2 / den
    e1 = grp * EXPERTS_PER_GROUP + i1
    e2 = grp * EXPERTS_PER_GROUP + i2

    eid_ref[0:1, :] = e1
    eid_ref[1:2, :] = e2
    wrows = jnp.concatenate([w1, w2, jnp.zeros((ROUTER_COLS - 2, tm), F32)], axis=0)
    wcol_ref[...] = wrows.T
    xrow = lax.broadcasted_iota(I32, (N_EXPERTS, tm), 0)
    hits = (xrow == e1).astype(F32) + (xrow == e2).astype(F32)
    cnt_ref[0] = jnp.sum(hits, axis=1, keepdims=True).astype(I32)


def _post_attn(o, x, wo_bf, modr, g_rows, router, layer, ctx_row, tm):
    b, n, d = x.shape
    nt = n // tm
    t_all = b * n
    wrh, wrl, br = router
    tile = lambda bb, t: (bb, t, 0)
    const2 = lambda bb, t: (0, 0)
    return pl.pallas_call(
        _post_attn_kernel,
        grid=(b, nt),
        in_specs=[
            pl.BlockSpec((1, tm, d), tile),
            pl.BlockSpec((1, tm, d), tile),
            pl.BlockSpec((d, d), const2),
            _mod_spec(layer, 2, ctx_row),
            _layer_row_spec(layer),
            _mod_spec(layer, 3, ctx_row),
            _mod_spec(layer, 4, ctx_row),
            pl.BlockSpec((d, 2 * ROUTER_COLS), const2),
            pl.BlockSpec((d, ROUTER_COLS), const2),
            pl.BlockSpec((1, ROUTER_COLS), const2),
        ],
        out_specs=[
            pl.BlockSpec((1, tm, d), tile),
            pl.BlockSpec((1, tm, d // 2), tile),
            pl.BlockSpec((2, tm), lambda bb, t: (0, bb * nt + t)),
            pl.BlockSpec((tm, ROUTER_COLS), lambda bb, t: (bb * nt + t, 0)),
            pl.BlockSpec((1, N_EXPERTS, 1), lambda bb, t: (bb * nt + t, 0, 0)),
        ],
        out_shape=[
            jax.ShapeDtypeStruct((b, n, d), F32),
            jax.ShapeDtypeStruct((b, n, d // 2), U32),
            jax.ShapeDtypeStruct((2, t_all), I32),
            jax.ShapeDtypeStruct((t_all, ROUTER_COLS), F32),
            jax.ShapeDtypeStruct((b * nt, N_EXPERTS, 1), I32),
        ],
        compiler_params=_cparams(("parallel", "parallel")),
        name="post_attn_router",
    )(o, x, wo_bf, modr, g_rows, modr, modr, wrh, wrl, br)


def _dest_kernel(eid_ref, base_ref, tri_ref, dest_ref):
    tm = eid_ref.shape[1]
    xrow = lax.broadcasted_iota(I32, (N_EXPERTS, tm), 0)
    oh0 = xrow == eid_ref[0:1, :]
    oh1 = xrow == eid_ref[1:2, :]
    both = oh0.astype(F32) + oh1.astype(F32)
    incl = _dot(both.astype(BF16), tri_ref[...])
    before = incl - both + base_ref[0].astype(F32)
    d0 = jnp.sum(jnp.where(oh0, before, 0.0), axis=0, keepdims=True)
    d1 = jnp.sum(jnp.where(oh1, before, 0.0), axis=0, keepdims=True)
    dest_ref[0, 0:1, :] = d0.astype(I32)
    dest_ref[0, 1:2, :] = d1.astype(I32)


def _dest_rows(eid, base, tri):
    tm = tri.shape[0]
    nt = eid.shape[1] // tm
    return pl.pallas_call(
        _dest_kernel,
        grid=(nt,),
        in_specs=[
            pl.BlockSpec((2, tm), lambda i: (0, i)),
            pl.BlockSpec((1, N_EXPERTS, 1), lambda i: (i, 0, 0)),
            pl.BlockSpec((tm, tm), lambda i: (0, 0)),
        ],
        out_specs=pl.BlockSpec((1, 2, tm), lambda i: (i, 0, 0)),
        out_shape=jax.ShapeDtypeStruct((nt, 2, tm), I32),
        compiler_params=_cparams(("parallel",)),
        name="moe_dest",
    )(eid, base, tri)


def _scatter_kernel(*refs, tiles):
    dest_ref = refs[0]
    hf_refs = refs[1:1 + len(tiles)]
    xs_ref, sem = refs[1 + len(tiles):]
    tm = hf_refs[0].shape[0]
    i = pl.program_id(0)

    def scatter_tile(hf_ref):
        def issue(t8, carry):
            base = pl.multiple_of(t8 * ROW_UNROLL, ROW_UNROLL)
            rows = hf_ref.at[pl.ds(base, ROW_UNROLL)]
            for u in range(ROW_UNROLL):
                for k in range(2):
                    d = dest_ref.at[0, k, pl.ds(base, ROW_UNROLL)][u]
                    pltpu.make_async_copy(rows.at[pl.ds(u, 1)], xs_ref.at[d], sem).start(priority=k)
            return carry

        lax.fori_loop(0, tm // ROW_UNROLL, issue, 0)
        for k in range(2):
            pltpu.make_async_copy(hf_ref, xs_ref.at[pl.ds(0, tm), 0], sem).wait()

    first = 0
    for hf_ref, nt in zip(hf_refs, tiles):
        pl.when((i >= first) & (i < first + nt))(functools.partial(scatter_tile, hf_ref))
        first += nt


def _scatter_rows(dest, hf_streams, n_rows):
    nt_all, _, tm = dest.shape
    d = hf_streams[0].shape[1]
    tiles = tuple(h.shape[0] // tm for h in hf_streams)
    assert sum(tiles) == nt_all
    in_specs = [pl.BlockSpec((1, 2, tm), lambda i: (i, 0, 0), memory_space=pltpu.SMEM)]
    first = 0
    for nt in tiles:
        in_specs.append(pl.BlockSpec((tm, d), lambda i, first=first, nt=nt: (jnp.clip(i - first, 0, nt - 1), 0)))
        first += nt
    return pl.pallas_call(
        functools.partial(_scatter_kernel, tiles=tiles),
        grid=(nt_all,),
        in_specs=in_specs,
        out_specs=pl.BlockSpec(memory_space=pl.ANY),
        out_shape=jax.ShapeDtypeStruct((n_rows, 1, d), hf_streams[0].dtype),
        scratch_shapes=[pltpu.SemaphoreType.DMA(())],
        compiler_params=_cparams(("arbitrary",)),
        name="moe_scatter",
    )(dest, *hf_streams)


def _expert_kernel(ve_ref, vblk_ref, lo_ref, hi_ref, x_hbm, wg_ref, wu_ref, wd_ref, y_hbm, wg_s, wu_s, wd_s,
                   xbuf, xsems, ybuf, ysems, *, n_blocks):
    v = pl.program_id(0)
    n_visits = pl.num_programs(0)
    pv = jnp.maximum(v - 1, 0)
    nv = jnp.minimum(v + 1, n_visits - 1)
    blk = vblk_ref[v]
    new_expert = (v == 0) | (ve_ref[v] != ve_ref[pv])
    first_visit = (v == 0) | (blk != vblk_ref[pv])
    last_visit = (v == n_visits - 1) | (blk != vblk_ref[nv])
    lo = lo_ref[v]
    hi = hi_ref[v]
    xslot = v % 2
    yslot = blk % 2

    def x_copy(visit, slot):
        start = pl.multiple_of(vblk_ref[visit] * EXPERT_BLOCK, EXPERT_BLOCK)
        return pltpu.make_async_copy(x_hbm.at[pl.ds(start, EXPERT_BLOCK), 0], xbuf.at[slot], xsems.at[slot])

    def y_copy(slot):
        start = pl.multiple_of(blk * EXPERT_BLOCK, EXPERT_BLOCK)
        return pltpu.make_async_copy(ybuf.at[slot], y_hbm.at[pl.ds(start, EXPERT_BLOCK), 0], ysems.at[slot])

    @pl.when(v == 0)
    def _():
        x_copy(0, 0).start()

    @pl.when(v + 1 < n_visits)
    def _():
        x_copy(v + 1, 1 - xslot).start()

    @pl.when(new_expert)
    def _():
        wg_s[...] = wg_ref[0, 0].astype(BF16)
        wu_s[...] = wu_ref[0, 0].astype(BF16)
        wd_s[...] = wd_ref[0, 0].astype(BF16)

    @pl.when(first_visit & (blk >= 2))
    def _():
        y_copy(yslot).wait()

    x_copy(v, xslot).wait()

    @pl.when(hi > lo)
    def _():
        x_lo, x_hi = _unpack_rows(xbuf[xslot])
        x_lo = x_lo.astype(BF16)
        x_hi = x_hi.astype(BF16)
        half = x_lo.shape[1]
        gate = _dot(x_lo, wg_s[:half]) + _dot(x_hi, wg_s[half:])
        up = _dot(x_lo, wu_s[:half]) + _dot(x_hi, wu_s[half:])
        mid = gate * jax.nn.sigmoid(gate) * up
        y = _pack_rows(_dot(mid.astype(BF16), wd_s[...]))
        row = lax.broadcasted_iota(I32, (EXPERT_BLOCK, 1), 0)
        mine = (row >= lo) & (row < hi)

        @pl.when(first_visit)
        def _():
            ybuf[yslot] = jnp.where(mine, y, jnp.uint32(0))

        @pl.when(jnp.logical_not(first_visit))
        def _():
            ybuf[yslot] = jnp.where(mine, y, ybuf[yslot])

    @pl.when(last_visit)
    def _():
        y_copy(yslot).start()

    @pl.when(v == n_visits - 1)
    def _():
        y_copy(yslot).wait()
        if n_blocks >= 2:
            y_copy(1 - yslot).wait()


def _experts(visits, xs, w_gate, w_up, w_down, layer):
    n_rows, _, dp = xs.shape
    _, _, d, f = w_gate.shape
    assert d == 2 * dp and n_rows % EXPERT_BLOCK == 0
    ve, vblk, lo, hi = visits
    w_map = lambda v, ve_r, vb_r, lo_r, hi_r: (layer, ve_r[v], 0, 0)
    grid_spec = pltpu.PrefetchScalarGridSpec(
        num_scalar_prefetch=4,
        grid=(ve.shape[0],),
        in_specs=[
            pl.BlockSpec(memory_space=pl.ANY),
            pl.BlockSpec((1, 1, d, f), w_map),
            pl.BlockSpec((1, 1, d, f), w_map),
            pl.BlockSpec((1, 1, f, d), w_map),
        ],
        out_specs=pl.BlockSpec(memory_space=pl.ANY),
        scratch_shapes=[pltpu.VMEM((d, f), BF16), pltpu.VMEM((d, f), BF16), pltpu.VMEM((f, d), BF16),
                        pltpu.VMEM((2, EXPERT_BLOCK, dp), U32), pltpu.SemaphoreType.DMA((2,)),
                        pltpu.VMEM((2, EXPERT_BLOCK, dp), U32), pltpu.SemaphoreType.DMA((2,))],
    )
    return pl.pallas_call(
        functools.partial(_expert_kernel, n_blocks=n_rows // EXPERT_BLOCK),
        grid_spec=grid_spec,
        out_shape=jax.ShapeDtypeStruct((n_rows, 1, dp), U32),
        compiler_params=_cparams(("arbitrary",)),
        name="moe_experts",
    )(ve, vblk, lo, hi, xs, w_gate, w_up, w_down)


def _combine_kernel(*refs, final_norm):
    if final_norm:
        dest_ref, dest_next_ref, xn_ref, wcol_ref, gt_ref, gfin_ref, y_ref, o_ref, ybuf, sems = refs
    else:
        dest_ref, dest_next_ref, xn_ref, wcol_ref, gt_ref, y_ref, o_ref, ybuf, sems = refs
    tm = xn_ref.shape[1]
    step = pl.program_id(0) * pl.num_programs(1) + pl.program_id(1)
    n_steps = pl.num_programs(0) * pl.num_programs(1)
    slot = step % 2

    def gather_tile(d_ref, into):
        def issue(t8, carry):
            base = pl.multiple_of(t8 * ROW_UNROLL, ROW_UNROLL)
            for k in range(2):
                rows = ybuf.at[into, k, pl.ds(base, ROW_UNROLL)]
                for u in range(ROW_UNROLL):
                    d = d_ref.at[0, k, pl.ds(base, ROW_UNROLL)][u]
                    pltpu.make_async_copy(y_ref.at[d], rows.at[pl.ds(u, 1)], sems.at[into]).start(priority=k)
            return carry

        lax.fori_loop(0, tm // ROW_UNROLL, issue, 0)

    pl.when(step == 0)(functools.partial(gather_tile, dest_ref, 0))
    pl.when(step + 1 < n_steps)(functools.partial(gather_tile, dest_next_ref, 1 - slot))

    for k in range(2):
        pltpu.make_async_copy(y_ref.at[pl.ds(0, tm), 0], ybuf.at[slot, k], sems.at[slot]).wait()
    w = wcol_ref[...]
    lo0, hi0 = _unpack_rows(ybuf[slot, 0])
    lo1, hi1 = _unpack_rows(ybuf[slot, 1])
    moe = jnp.concatenate([w[:, 0:1] * lo0 + w[:, 1:2] * lo1, w[:, 0:1] * hi0 + w[:, 1:2] * hi1], axis=1)
    out = xn_ref[0] + gt_ref[0] * moe
    if final_norm:
        ms = jnp.mean(out * out, axis=-1, keepdims=True)
        out = out * lax.rsqrt(ms + NORM_EPS) * gfin_ref[...]
    o_ref[0] = out


def _combine(dest, xn, wcol, modr, y, layer, ctx_row, g_final=None):
    b, n, d = xn.shape
    nt_all, _, tm = dest.shape
    nt = n // tm
    final_norm = g_final is not None
    args = [dest, dest, xn, wcol, modr]
    in_specs = [
        pl.BlockSpec((1, 2, tm), lambda bb, t: (bb * nt + t, 0, 0), memory_space=pltpu.SMEM),
        pl.BlockSpec((1, 2, tm), lambda bb, t: (jnp.minimum(bb * nt + t + 1, nt_all - 1), 0, 0),
                     memory_space=pltpu.SMEM),
        pl.BlockSpec((1, tm, d), lambda bb, t: (bb, t, 0)),
        pl.BlockSpec((tm, ROUTER_COLS), lambda bb, t: (bb * nt + t, 0)),
        _mod_spec(layer, 5, ctx_row),
    ]
    if final_norm:
        args.append(g_final.reshape(1, d))
        in_specs.append(pl.BlockSpec((1, d), lambda bb, t: (0, 0)))
    args.append(y)
    in_specs.append(pl.BlockSpec(memory_space=pl.ANY))
    return pl.pallas_call(
        functools.partial(_combine_kernel, final_norm=final_norm),
        grid=(b, nt),
        in_specs=in_specs,
        out_specs=pl.BlockSpec((1, tm, d), lambda bb, t: (bb, t, 0)),
        out_shape=jax.ShapeDtypeStruct((b, n, d), F32),
        scratch_shapes=[pltpu.VMEM((2, 2, tm, y.shape[2]), y.dtype), pltpu.SemaphoreType.DMA((2,))],
        compiler_params=_cparams(("arbitrary", "arbitrary")),
        name="moe_combine",
    )(*args)


def _rope_tables(n):
    t = np.arange(n)
    row = (t // GRID_W).astype(np.float32)
    col = (t % GRID_W).astype(np.float32)
    quarter = HEAD_DIM // 4
    inv = jnp.asarray(ROPE_THETA, F32) ** (-jnp.arange(quarter, dtype=F32) / quarter)
    ar = jnp.asarray(row)[:, None] * inv
    ac = jnp.asarray(col)[:, None] * inv
    ang = jnp.concatenate([ar, ar, ac, ac], axis=-1)
    ang = jnp.concatenate([ang, ang], axis=-1)
    sign = np.where((np.arange(LANES) % (HEAD_DIM // 2)) < quarter, -1.0, 1.0).astype(np.float32)
    return jnp.cos(ang), jnp.sin(ang) * sign


def _na_bias_tables(rpb):
    n_dr, n_dc = 2 * NA_WIN_H - 1, 2 * NA_WIN_W - 1
    blocked = jnp.full((N_HEADS, 2, n_dc), NEG_INF, F32)
    rows = jnp.concatenate([blocked, rpb.astype(F32) * LOG2E, blocked], axis=1)
    assert rows.shape[1] == n_dr + 4 == NA_TABLE + 1
    pair = jnp.stack([rows[:, 0:NA_TABLE], rows[:, 1:NA_TABLE + 1]], axis=2)
    c = np.arange(GRID_W)[:, None]
    kc = np.arange(GRID_W)[None, :]
    cs = np.clip(c - NA_WIN_W // 2, 0, GRID_W - NA_WIN_W)
    in_window = (kc >= cs) & (kc < cs + NA_WIN_W)
    out = jnp.full((N_HEADS, NA_TABLE, GRID_W, 2, GRID_W), NEG_INF, F32)
    for dc in range(-(NA_WIN_W - 1), NA_WIN_W):
        hit = ((kc - c) == dc) & in_window
        if hit.any():
            out = jnp.where(hit[None, None, :, None, :], pair[:, :, None, :, dc + NA_WIN_W - 1, None], out)
    return out.reshape(N_HEADS, NA_TABLE, GRID_W, LANES)


def _router_tables(w_group, b_group, w_router, b_router):
    d = w_group.shape[0]
    w = jnp.zeros((d, ROUTER_COLS), F32)
    w = w.at[:, 0:N_GROUPS].set(w_group).at[:, 8:8 + N_EXPERTS].set(w_router)
    bias = jnp.zeros((1, ROUTER_COLS), F32)
    bias = bias.at[0, 0:N_GROUPS].set(b_group).at[0, 8:8 + N_EXPERTS].set(b_router)
    hi = w.astype(BF16)
    lo = (w - hi.astype(F32)).astype(BF16)
    return jnp.concatenate([hi, lo], axis=1), hi, bias


def _pick(table, onehot):
    return jnp.sum(jnp.where(onehot, table[None, :], 0), axis=1)


def _moe_layout(cnt_tiles, n_rows):
    cnt = cnt_tiles[:, :, 0]
    counts = jnp.sum(cnt, axis=0)
    ends = jnp.cumsum(counts)
    starts = ends - counts
    base = starts[None, :] + jnp.cumsum(cnt, axis=0) - cnt
    first_blk = starts // EXPERT_BLOCK
    n_blk = jnp.where(counts > 0, (ends - 1) // EXPERT_BLOCK - first_blk + 1, 0)
    v_end = jnp.cumsum(n_blk)
    v_off = v_end - n_blk
    n_visits = n_rows // EXPERT_BLOCK + N_EXPERTS
    v = jnp.arange(n_visits, dtype=I32)
    valid = v < v_end[-1]
    vv = jnp.minimum(v, v_end[-1] - 1)
    ve = jnp.sum((v_end[None, :] <= vv[:, None]).astype(I32), axis=1)
    onehot = ve[:, None] == jnp.arange(N_EXPERTS, dtype=I32)[None, :]
    vblk = _pick(first_blk, onehot) + vv - _pick(v_off, onehot)
    lo = jnp.clip(_pick(starts, onehot) - vblk * EXPERT_BLOCK, 0, EXPERT_BLOCK)
    hi = jnp.clip(_pick(ends, onehot) - vblk * EXPERT_BLOCK, 0, EXPERT_BLOCK)
    hi = jnp.where(valid, hi, lo)
    visits = tuple(a.astype(I32) for a in (ve, vblk, lo, hi))
    return base[:, :, None].astype(I32), visits


def kernel(x, c, ctx, c_ctx, w_ada, b_ada, g_attn, w_qkv, w_o, sink_a, gq_b, gk_b, rpb_c, g_ffn,
           w_group, b_group, w_router, b_router, w_gate, w_up, w_down, g_final):
    b, n, d = x.shape
    n_ctx = ctx.shape[1]
    depth = w_ada.shape[0]
    tm_tok = 512
    tm_ctx = 256
    assert b + 1 <= MOD_ROWS and d == D_MODEL and n % tm_tok == 0 and n_ctx % tm_ctx == 0
    assert (b * n_ctx) % tm_tok == 0
    ctx_row = b

    cc = jnp.zeros((MOD_ROWS, d), F32).at[:b].set(c).at[b].set(c_ctx)
    modr = _ada_all(cc, w_ada, b_ada).reshape(depth * MOD_ROWS * N_MOD, 1, d)
    g_attn_rows = g_attn.reshape(depth, 1, d)
    g_ffn_rows = g_ffn.reshape(depth, 1, d)
    rope_tabs = _rope_tables(n)
    head_sum = jnp.asarray(np.kron(np.eye(LANES // HEAD_DIM), np.ones((HEAD_DIM, HEAD_DIM))), BF16)
    tri = jnp.asarray(np.triu(np.ones((tm_tok, tm_tok))), BF16)

    def layer_qkv_config(layer):
        mixer, idx = layer % N_MIXERS, layer // N_MIXERS
        gains = None
        if mixer == 1:
            gains = (jnp.tile(gq_b[idx], LANES // HEAD_DIM).reshape(1, LANES),
                     jnp.tile(gk_b[idx], LANES // HEAD_DIM).reshape(1, LANES), head_sum)
        return w_qkv[layer].astype(BF16), (rope_tabs if mixer != 2 else None), gains

    xc = ctx
    for i in range(depth):
        m = i % N_MIXERS
        j = i // N_MIXERS
        last = i == depth - 1
        wqkv_bf, rope_i, qk_gains = layer_qkv_config(i)
        wo_bf = w_o[i].astype(BF16)

        latent_qkv = _qkv(x, modr, g_attn_rows, wqkv_bf, i, None, tm_tok, rope_i, qk_gains)
        if m == 2:
            qp, k, v = latent_qkv
            qr = None
        else:
            qp, qr, k, v = latent_qkv
        qc, kc, vc = _qkv(xc, modr, g_attn_rows, wqkv_bf, i, ctx_row, tm_ctx, None, qk_gains)
        sink = sink_a[j] if m == 0 else None
        extra = sink if m == 0 else (_na_bias_tables(rpb_c[j]) if m == 2 else None)
        o = _attention(m, qp, qr, k, v, kc, vc, extra)

        router = _router_tables(w_group[i], b_group[i], w_router[i], b_router[i])
        streams = [(o, x, None)]
        if not last:
            oc = _ctx_attention(qc, kc, vc, sink)
            streams.append((oc.reshape(1, b * n_ctx, d), xc.reshape(1, b * n_ctx, d), ctx_row))
        routed = [_post_attn(o_s, x_s, wo_bf, modr, g_ffn_rows, router, i, row, tm_tok) for (o_s, x_s, row) in streams]

        n_rows = 2 * sum(r[2].shape[1] for r in routed)
        assert n_rows % EXPERT_BLOCK == 0
        base, visits = _moe_layout(jnp.concatenate([r[4] for r in routed], axis=0), n_rows)
        dest = _dest_rows(jnp.concatenate([r[2] for r in routed], axis=1), base, tri)
        xs = _scatter_rows(dest, [r[1].reshape(-1, d // 2) for r in routed], n_rows)
        y = _experts(visits, xs, w_gate, w_up, w_down, i)
        nt_lat = routed[0][4].shape[0]
        x = _combine(dest[:nt_lat], routed[0][0], routed[0][3], modr, y, i, None, g_final if last else None)
        if not last:
            xc = _combine(dest[nt_lat:], routed[1][0], routed[1][3], modr, y, i, ctx_row).reshape(b, n_ctx, d)
    return x
```

```python
import functools

import numpy as np
import jax
import jax.numpy as jnp
from jax import lax
from jax.experimental import pallas as pl
from jax.experimental.pallas import tpu as pltpu

F32 = jnp.float32
BF16 = jnp.bfloat16
I32 = jnp.int32
U32 = jnp.uint32

D_MODEL = 1024
GRID_W = 64
N_MIXERS = 3
N_HEADS = 16
N_KV_HEADS = 4
HEAD_DIM = D_MODEL // N_HEADS
Q_PER_KV = N_HEADS // N_KV_HEADS
QKV_DIM = (N_HEADS + 2 * N_KV_HEADS) * HEAD_DIM
ATTN_SCALE = HEAD_DIM ** -0.5
ROPE_THETA = 10000.0
ATTN_BLOCK_Q = (128, 256, 128)
WINDOW = 128
NA_WIN_H = 8
NA_WIN_W = 16
N_GROUPS = 4
EXPERTS_PER_GROUP = 8
N_EXPERTS = N_GROUPS * EXPERTS_PER_GROUP
EXPERT_HIDDEN = D_MODEL // 2
NORM_EPS = 1e-6
NEG_INF = -1e30
LOG2E = 1.4426950408889634

LANES = 128
MOD_ROWS = 24
N_MOD = 6
EXPERT_BLOCK = 512
ROW_UNROLL = 8
ROUTER_COLS = LANES
NA_KEY_ROWS = 10
NA_TABLE = 18
VMEM_LIMIT = 56 * 1024 * 1024


def _cparams(sem, vmem=VMEM_LIMIT):
    return pltpu.CompilerParams(dimension_semantics=sem, vmem_limit_bytes=vmem)


def _dot(a, b):
    return jnp.dot(a, b, preferred_element_type=F32)


def _dot_nt(a, b):
    return lax.dot_general(a, b, (((1,), (1,)), ((), ())), preferred_element_type=F32)


def _split_bf16(x):
    hi = x.astype(BF16)
    lo = (x - hi.astype(F32)).astype(BF16)
    return hi, lo


def _pack_rows(x):
    w = x.shape[1] // 2
    bits = pltpu.bitcast(x.astype(BF16).astype(F32), U32)
    return (bits[:, w:] & jnp.uint32(0xFFFF0000)) | (bits[:, :w] >> 16)


def _unpack_rows(p):
    lo = pltpu.bitcast(p << 16, F32)
    hi = pltpu.bitcast(p & jnp.uint32(0xFFFF0000), F32)
    return lo, hi


def _rms_mod(x, g, sh, sc):
    ms = jnp.mean(x * x, axis=-1, keepdims=True)
    y = x * lax.rsqrt(ms + NORM_EPS) * g
    return y * (1.0 + sc) + sh


def _ada_kernel(c_ref, w_ref, b_ref, o_ref):
    c = c_ref[...]
    s = c * jax.nn.sigmoid(c)
    o_ref[0] = _dot(s.astype(BF16), w_ref[0].astype(BF16)) + b_ref[0]


def _ada_all(cc, w_ada, b_ada):
    depth, d, d6 = w_ada.shape
    tn = 1536
    return pl.pallas_call(
        _ada_kernel,
        grid=(depth, d6 // tn),
        in_specs=[
            pl.BlockSpec((MOD_ROWS, d), lambda i, j: (0, 0)),
            pl.BlockSpec((1, d, tn), lambda i, j: (i, 0, j)),
            pl.BlockSpec((1, 1, tn), lambda i, j: (i, 0, j)),
        ],
        out_specs=pl.BlockSpec((1, MOD_ROWS, tn), lambda i, j: (i, 0, j)),
        out_shape=jax.ShapeDtypeStruct((depth, MOD_ROWS, d6), F32),
        compiler_params=_cparams(("parallel", "parallel")),
        name="ada_mod",
    )(cc, w_ada, b_ada.reshape(depth, 1, d6))


def _mod_spec(layer, chunk, ctx_row):
    if ctx_row is None:
        return pl.BlockSpec((1, 1, D_MODEL), lambda b, t: ((layer * MOD_ROWS + b) * N_MOD + chunk, 0, 0))
    return pl.BlockSpec((1, 1, D_MODEL), lambda b, t: ((layer * MOD_ROWS + ctx_row) * N_MOD + chunk, 0, 0))


def _layer_row_spec(layer):
    return pl.BlockSpec((1, 1, D_MODEL), lambda b, t: (layer, 0, 0))


def _qkv_kernel(*refs, qk_norm, rope):
    _qkv_project(refs[0][0], refs[1:], qk_norm=qk_norm, rope=rope)


def _qkv_project(x, refs, *, qk_norm, rope):
    g_ref, sh_ref, sc_ref, w_ref = refs[:4]
    pos = 4
    if rope:
        cos_ref, sin_ref = refs[pos:pos + 2]
        pos += 2
    if qk_norm:
        gq_ref, gk_ref, bm_ref = refs[pos:pos + 3]
        pos += 3
    if rope:
        qp_ref, qr_ref, k_ref, v_ref = refs[pos:]
    else:
        qp_ref, k_ref, v_ref = refs[pos:]

    h = _rms_mod(x, g_ref[0], sh_ref[0], sc_ref[0])
    res = _dot(h.astype(BF16), w_ref[...])
    half = LANES // 2
    n_q = N_HEADS * HEAD_DIM // LANES
    n_k = N_KV_HEADS * HEAD_DIM // LANES
    if rope:
        lane = lax.broadcasted_iota(I32, (1, LANES), 1)
        first_half = (lane & (HEAD_DIM // 2 - 1)) < (HEAD_DIM // 4)
    for cb in range(QKV_DIM // LANES):
        xs = res[:, cb * LANES:(cb + 1) * LANES]
        is_q = cb < n_q
        is_k = n_q <= cb < n_q + n_k
        if qk_norm and (is_q or is_k):
            hi, lo = _split_bf16(xs * xs)
            ms = (_dot(hi, bm_ref[...]) + _dot(lo, bm_ref[...])) * (1.0 / HEAD_DIM)
            gain = gq_ref[...] if is_q else gk_ref[...]
            xs = xs * lax.rsqrt(ms + NORM_EPS) * gain
        if is_q:
            xs = xs * (ATTN_SCALE * LOG2E)
        if rope and (is_q or is_k):
            rot = jnp.where(first_half, pltpu.roll(xs, LANES - HEAD_DIM // 4, 1), pltpu.roll(xs, HEAD_DIM // 4, 1))
            xr = xs * cos_ref[...] + rot * sin_ref[...]
        if is_q:
            hd = 2 * cb
            qp_ref[0, hd] = xs[:, :half].astype(BF16)
            qp_ref[0, hd + 1] = xs[:, half:].astype(BF16)
            if rope:
                qr_ref[0, hd] = xr[:, :half].astype(BF16)
                qr_ref[0, hd + 1] = xr[:, half:].astype(BF16)
        elif is_k:
            hd = 2 * (cb - n_q)
            kk = xr if rope else xs
            k_ref[0, hd] = kk[:, :half].astype(BF16)
            k_ref[0, hd + 1] = kk[:, half:].astype(BF16)
        else:
            hd = 2 * (cb - n_q - n_k)
            low = lax.broadcasted_iota(I32, (1, LANES), 1) < HEAD_DIM
            swapped = pltpu.roll(xs, half, 1)
            for par, (a, b) in enumerate(((xs, swapped), (swapped, xs))):
                v_aug = jnp.concatenate([jnp.where(low, a, 1.0), jnp.where(low, 1.0, b)], axis=1)
                v_ref[0, hd + par] = v_aug.astype(BF16)


def _qkv_operands(b, n, modr, g_rows, w_bf, layer, ctx_row, tm, rope_tabs, qk_gains):
    d = D_MODEL
    rope = rope_tabs is not None
    args = [g_rows, modr, modr, w_bf]
    in_specs = [
        _layer_row_spec(layer),
        _mod_spec(layer, 0, ctx_row),
        _mod_spec(layer, 1, ctx_row),
        pl.BlockSpec((d, QKV_DIM), lambda bb, t: (0, 0)),
    ]
    if rope:
        args += list(rope_tabs)
        in_specs += [pl.BlockSpec((tm, LANES), lambda bb, t: (t, 0))] * 2
    if qk_gains is not None:
        args += list(qk_gains)
        in_specs += [pl.BlockSpec((1, LANES), lambda bb, t: (0, 0))] * 2
        in_specs += [pl.BlockSpec((LANES, LANES), lambda bb, t: (0, 0))]
    q_shape = jax.ShapeDtypeStruct((b, N_HEADS, n, HEAD_DIM), BF16)
    kv_shape = jax.ShapeDtypeStruct((b, N_KV_HEADS, n, HEAD_DIM), BF16)
    q_spec = pl.BlockSpec((1, N_HEADS, tm, HEAD_DIM), lambda bb, t: (bb, 0, t, 0))
    kv_spec = pl.BlockSpec((1, N_KV_HEADS, tm, HEAD_DIM), lambda bb, t: (bb, 0, t, 0))
    v_shape = jax.ShapeDtypeStruct((b, N_KV_HEADS, n, 2 * LANES), BF16)
    v_spec = pl.BlockSpec((1, N_KV_HEADS, tm, 2 * LANES), lambda bb, t: (bb, 0, t, 0))
    if rope:
        out_shape, out_specs = [q_shape, q_shape, kv_shape, v_shape], [q_spec, q_spec, kv_spec, v_spec]
    else:
        out_shape, out_specs = [q_shape, kv_shape, v_shape], [q_spec, kv_spec, v_spec]
    return args, in_specs, out_shape, out_specs


def _qkv(x, modr, g_rows, w_bf, layer, ctx_row, tm, rope_tabs=None, qk_gains=None):
    b, n, d = x.shape
    args, in_specs, out_shape, out_specs = _qkv_operands(b, n, modr, g_rows, w_bf, layer, ctx_row, tm,
                                                         rope_tabs, qk_gains)
    return pl.pallas_call(
        functools.partial(_qkv_kernel, qk_norm=qk_gains is not None, rope=rope_tabs is not None),
        grid=(b, n // tm),
        in_specs=[pl.BlockSpec((1, tm, d), lambda bb, t: (bb, t, 0))] + in_specs,
        out_specs=out_specs,
        out_shape=out_shape,
        compiler_params=_cparams(("parallel", "parallel")),
        name="qkv_rope" if rope_tabs is not None else "qkv",
    )(x, *args)


def _attend(pieces, m_prev=None, acc_prev=None):
    m = m_prev
    for s, _ in pieces:
        m_s = jnp.max(s, axis=-1, keepdims=True)
        m = m_s if m is None else jnp.maximum(m, m_s)
    acc = None if acc_prev is None else jnp.exp2(m_prev - m) * acc_prev
    for s, v_aug in pieces:
        pv = _dot(jnp.exp2(s - m).astype(BF16), v_aug)
        acc = pv if acc is None else acc + pv
    return m, acc


def _head_pair_outputs(m, acc, rows_per_head, sink_ref=None, kv_head=None):
    low = lax.broadcasted_iota(I32, (1, LANES), 1) < HEAD_DIM
    blocks = []
    for pair in range(Q_PER_KV // 2):
        forms = []
        for par in range(2):
            g = 2 * pair + par
            rows = slice(g * rows_per_head, (g + 1) * rows_per_head)
            num = acc[rows, par * LANES:(par + 1) * LANES]
            den = acc[rows, (1 - par) * LANES:(2 - par) * LANES]
            if sink_ref is not None:
                sink = sink_ref[kv_head * Q_PER_KV + g] * LOG2E
                m2 = jnp.maximum(m[rows], sink)
                scale = jnp.exp2(m[rows] - m2)
                num = num * scale
                den = den * scale + jnp.exp2(sink - m2)
            forms.append(num / den)
        blocks.append(jnp.where(low, forms[0], forms[1]))
    return blocks


def _attn_kernel(*refs, mode, bq, key_chunk):
    if mode == 0:
        qp_ref, qr_ref, k_ref, v_ref, kc_ref, vc_ref, sink_ref, o_ref = refs
    elif mode == 1:
        qp_ref, qr_ref, k_ref, v_ref, kc_ref, vc_ref, o_ref = refs
    else:
        qp_ref, k_ref, v_ref, kc_ref, vc_ref, t2_ref, o_ref = refs
    qb = pl.program_id(1)
    n = k_ref.shape[2]
    rows = Q_PER_KV * bq

    if mode == 0:
        span = bq + 2 * WINDOW
        kstart = pl.multiple_of(jnp.clip(qb * bq - WINDOW, 0, n - span), WINDOW)
        qpos = qb * bq + (lax.broadcasted_iota(I32, (rows, span), 0) & (bq - 1))
        kpos = kstart + lax.broadcasted_iota(I32, (rows, span), 1)
        band = jnp.abs(kpos - qpos) <= WINDOW
    elif mode == 2:
        rows_grid = n // GRID_W
        r0 = qb * (bq // GRID_W)
        wr = jnp.clip(r0 - NA_WIN_H // 2, 0, rows_grid - NA_KEY_ROWS)
        kstart = pl.multiple_of(wr * GRID_W, GRID_W)
        lane = lax.broadcasted_iota(I32, (GRID_W, LANES), 1)

    pieces = []
    for h in range(N_KV_HEADS):
        qp = qp_ref[0, h * Q_PER_KV:(h + 1) * Q_PER_KV].reshape(rows, HEAD_DIM)
        ctx_piece = (_dot_nt(qp, kc_ref[0, h]), vc_ref[0, h])
        if mode == 0:
            qr = qr_ref[0, h * Q_PER_KV:(h + 1) * Q_PER_KV].reshape(rows, HEAD_DIM)
            s = jnp.where(band, _dot_nt(qr, k_ref[0, h, pl.ds(kstart, span), :]), NEG_INF)
            m, acc = _attend([(s, v_ref[0, h, pl.ds(kstart, span), :]), ctx_piece])
            pieces += _head_pair_outputs(m, acc, bq, sink_ref, h)
        elif mode == 1:
            qr = qr_ref[0, h * Q_PER_KV:(h + 1) * Q_PER_KV].reshape(rows, HEAD_DIM)
            n_chunks = n // key_chunk
            m, acc = None, None
            for c in range(n_chunks):
                keys = slice(c * key_chunk, (c + 1) * key_chunk)
                step = [(_dot_nt(qr, k_ref[0, h, keys, :]), v_ref[0, h, keys, :])]
                if c == n_chunks - 1:
                    step.append(ctx_piece)
                m, acc = _attend(step, m, acc)
            pieces += _head_pair_outputs(None, acc, bq)
        else:
            n_keys = NA_KEY_ROWS * GRID_W
            s = _dot_nt(qp, k_ref[0, h, pl.ds(kstart, n_keys), :])
            s_ctx, v_ctx = ctx_piece
            p_lat, p_ctx = [], []
            for g in range(Q_PER_KV):
                for a in range(bq // GRID_W):
                    r = r0 + a
                    rs = jnp.clip(r - NA_WIN_H // 2, 0, rows_grid - NA_WIN_H)
                    tiles = []
                    for j in range(n_keys // LANES):
                        kr0 = wr + 2 * j
                        tab = t2_ref[h * Q_PER_KV + g, kr0 - r + NA_TABLE // 2]
                        pen0 = jnp.where((kr0 >= rs) & (kr0 < rs + NA_WIN_H), 0.0, NEG_INF)
                        pen1 = jnp.where((kr0 + 1 >= rs) & (kr0 + 1 < rs + NA_WIN_H), 0.0, NEG_INF)
                        tiles.append(tab + jnp.where(lane < GRID_W, pen0, pen1))
                    qrows = slice(g * bq + a * GRID_W, g * bq + (a + 1) * GRID_W)
                    sb = s[qrows] + jnp.concatenate(tiles, axis=1)
                    sc = s_ctx[qrows]
                    m = jnp.maximum(jnp.max(sb, axis=-1, keepdims=True), jnp.max(sc, axis=-1, keepdims=True))
                    p_lat.append(jnp.exp2(sb - m).astype(BF16))
                    p_ctx.append(jnp.exp2(sc - m).astype(BF16))
            acc = (_dot(jnp.concatenate(p_lat, axis=0), v_ref[0, h, pl.ds(kstart, n_keys), :])
                   + _dot(jnp.concatenate(p_ctx, axis=0), v_ctx))
            pieces += _head_pair_outputs(None, acc, bq)
    o_ref[0] = jnp.concatenate(pieces, axis=-1).astype(o_ref.dtype)


def _attention(mode, qp, qr, k, v, kc, vc, extra):
    b, _, n, _ = qp.shape
    c = kc.shape[2]
    bq = ATTN_BLOCK_Q[mode]
    q_spec = pl.BlockSpec((1, N_HEADS, bq, HEAD_DIM), lambda bb, i: (bb, 0, i, 0))
    k_spec = pl.BlockSpec((1, N_KV_HEADS, n, HEAD_DIM), lambda bb, i: (bb, 0, 0, 0))
    v_spec = pl.BlockSpec((1, N_KV_HEADS, n, 2 * LANES), lambda bb, i: (bb, 0, 0, 0))
    kc_spec = pl.BlockSpec((1, N_KV_HEADS, c, HEAD_DIM), lambda bb, i: (bb, 0, 0, 0))
    vc_spec = pl.BlockSpec((1, N_KV_HEADS, c, 2 * LANES), lambda bb, i: (bb, 0, 0, 0))
    if mode == 2:
        args = [qp, k, v, kc, vc, extra]
        in_specs = [q_spec, k_spec, v_spec, kc_spec, vc_spec,
                    pl.BlockSpec(extra.shape, lambda bb, i: (0, 0, 0, 0))]
    else:
        args = [qp, qr, k, v, kc, vc]
        in_specs = [q_spec, q_spec, k_spec, v_spec, kc_spec, vc_spec]
        if mode == 0:
            args.append(extra)
            in_specs.append(pl.BlockSpec(memory_space=pltpu.SMEM))
    return pl.pallas_call(
        functools.partial(_attn_kernel, mode=mode, bq=bq, key_chunk=512),
        grid=(b, n // bq),
        in_specs=in_specs,
        out_specs=pl.BlockSpec((1, bq, D_MODEL), lambda bb, i: (bb, i, 0)),
        out_shape=jax.ShapeDtypeStruct((b, n, D_MODEL), BF16),
        compiler_params=_cparams(("parallel", "arbitrary")),
        name=("attn_window", "attn_global", "attn_na")[mode],
    )(*args)


def _ctx_attn_kernel(*refs, has_sink):
    if has_sink:
        q_ref, k_ref, v_ref, sink_ref, o_ref = refs
    else:
        q_ref, k_ref, v_ref, o_ref = refs
    c = k_ref.shape[2]
    rows = Q_PER_KV * c
    pieces = []
    for h in range(N_KV_HEADS):
        q = q_ref[0, h * Q_PER_KV:(h + 1) * Q_PER_KV].reshape(rows, HEAD_DIM)
        m, acc = _attend([(_dot_nt(q, k_ref[0, h]), v_ref[0, h])])
        if has_sink:
            pieces += _head_pair_outputs(m, acc, c, sink_ref, h)
        else:
            pieces += _head_pair_outputs(None, acc, c)
    o_ref[0] = jnp.concatenate(pieces, axis=-1).astype(o_ref.dtype)


def _ctx_attention(qc, kc, vc, sink):
    b, _, c, _ = qc.shape
    args = [qc, kc, vc]
    in_specs = [pl.BlockSpec((1, N_HEADS, c, HEAD_DIM), lambda bb: (bb, 0, 0, 0)),
                pl.BlockSpec((1, N_KV_HEADS, c, HEAD_DIM), lambda bb: (bb, 0, 0, 0)),
                pl.BlockSpec((1, N_KV_HEADS, c, 2 * LANES), lambda bb: (bb, 0, 0, 0))]
    if sink is not None:
        args.append(sink)
        in_specs.append(pl.BlockSpec(memory_space=pltpu.SMEM))
    return pl.pallas_call(
        functools.partial(_ctx_attn_kernel, has_sink=sink is not None),
        grid=(b,),
        in_specs=in_specs,
        out_specs=pl.BlockSpec((1, c, D_MODEL), lambda bb: (bb, 0, 0)),
        out_shape=jax.ShapeDtypeStruct((b, c, D_MODEL), BF16),
        compiler_params=_cparams(("parallel",)),
        name="attn_ctx",
    )(*args)


def _post_attn_kernel(o_ref, x_ref, wo_ref, gt_ref, g_ref, sh_ref, sc_ref, wr_hilo_ref, wr_hi_ref, br_ref,
                      xn_ref, hf_ref, eid_ref, wcol_ref, cnt_ref):
    xn = x_ref[0] + gt_ref[0] * _dot(o_ref[0], wo_ref[...])
    xn_ref[0] = xn
    hf = _rms_mod(xn, g_ref[0], sh_ref[0], sc_ref[0])
    hf_ref[0] = _pack_rows(hf)
    tm = hf.shape[0]

    hh, hl = _split_bf16(hf)
    both = _dot(hh, wr_hilo_ref[...])
    lg = both[:, :ROUTER_COLS] + both[:, ROUTER_COLS:] + _dot(hl, wr_hi_ref[...]) + br_ref[...]
    lgt = lg.T

    gl = lgt[0:N_GROUPS]
    ge = jnp.exp(gl - jnp.max(gl, axis=0, keepdims=True))
    pg = ge / jnp.sum(ge, axis=0, keepdims=True)
    p_top = jnp.max(pg, axis=0, keepdims=True)
    grow = lax.broadcasted_iota(I32, (N_GROUPS, tm), 0)
    grp = jnp.min(jnp.where(pg == p_top, grow, N_GROUPS), axis=0, keepdims=True)

    el = lgt[8:8 + EXPERTS_PER_GROUP]
    for gi in range(1, N_GROUPS):
        el = jnp.where(grp == gi, lgt[8 + gi * EXPERTS_PER_GROUP:8 + (gi + 1) * EXPERTS_PER_GROUP], el)
    ee = jnp.exp(el - jnp.max(el, axis=0, keepdims=True))
    pe = ee / jnp.sum(ee, axis=0, keepdims=True)
    erow = lax.broadcasted_iota(I32, (EXPERTS_PER_GROUP, tm), 0)
    p1 = jnp.max(pe, axis=0, keepdims=True)
    i1 = jnp.min(jnp.where(pe == p1, erow, EXPERTS_PER_GROUP), axis=0, keepdims=True)
    pe2 = jnp.where(erow == i1, -1.0, pe)
    p2 = jnp.max(pe2, axis=0, keepdims=True)
    i2 = jnp.min(jnp.where(pe2 == p2, erow, EXPERTS_PER_GROUP), axis=0, keepdims=True)
    den = p1 + p2
    w1 = p_top * p1 / den
    w2 = p_top * p2 / den
    e1 = grp * EXPERTS_PER_GROUP + i1
    e2 = grp * EXPERTS_PER_GROUP + i2

    eid_ref[0:1, :] = e1
    eid_ref[1:2, :] = e2
    wrows = jnp.concatenate([w1, w2, jnp.zeros((ROUTER_COLS - 2, tm), F32)], axis=0)
    wcol_ref[...] = wrows.T
    xrow = lax.broadcasted_iota(I32, (N_EXPERTS, tm), 0)
    hits = (xrow == e1).astype(F32) + (xrow == e2).astype(F32)
    cnt_ref[0] = jnp.sum(hits, axis=1, keepdims=True).astype(I32)


def _post_attn(o, x, wo_bf, modr, g_rows, router, layer, ctx_row, tm):
    b, n, d = x.shape
    nt = n // tm
    t_all = b * n
    wrh, wrl, br = router
    tile = lambda bb, t: (bb, t, 0)
    const2 = lambda bb, t: (0, 0)
    return pl.pallas_call(
        _post_attn_kernel,
        grid=(b, nt),
        in_specs=[
            pl.BlockSpec((1, tm, d), tile),
            pl.BlockSpec((1, tm, d), tile),
            pl.BlockSpec((d, d), const2),
            _mod_spec(layer, 2, ctx_row),
            _layer_row_spec(layer),
            _mod_spec(layer, 3, ctx_row),
            _mod_spec(layer, 4, ctx_row),
            pl.BlockSpec((d, 2 * ROUTER_COLS), const2),
            pl.BlockSpec((d, ROUTER_COLS), const2),
            pl.BlockSpec((1, ROUTER_COLS), const2),
        ],
        out_specs=[
            pl.BlockSpec((1, tm, d), tile),
            pl.BlockSpec((1, tm, d // 2), tile),
            pl.BlockSpec((2, tm), lambda bb, t: (0, bb * nt + t)),
            pl.BlockSpec((tm, ROUTER_COLS), lambda bb, t: (bb * nt + t, 0)),
            pl.BlockSpec((1, N_EXPERTS, 1), lambda bb, t: (bb * nt + t, 0, 0)),
        ],
        out_shape=[
            jax.ShapeDtypeStruct((b, n, d), F32),
            jax.ShapeDtypeStruct((b, n, d // 2), U32),
            jax.ShapeDtypeStruct((2, t_all), I32),
            jax.ShapeDtypeStruct((t_all, ROUTER_COLS), F32),
            jax.ShapeDtypeStruct((b * nt, N_EXPERTS, 1), I32),
        ],
        compiler_params=_cparams(("parallel", "parallel")),
        name="post_attn_router",
    )(o, x, wo_bf, modr, g_rows, modr, modr, wrh, wrl, br)


def _dest_kernel(eid_ref, base_ref, tri_ref, dest_ref):
    tm = eid_ref.shape[1]
    xrow = lax.broadcasted_iota(I32, (N_EXPERTS, tm), 0)
    oh0 = xrow == eid_ref[0:1, :]
    oh1 = xrow == eid_ref[1:2, :]
    both = oh0.astype(F32) + oh1.astype(F32)
    incl = _dot(both.astype(BF16), tri_ref[...])
    before = incl - both + base_ref[0].astype(F32)
    d0 = jnp.sum(jnp.where(oh0, before, 0.0), axis=0, keepdims=True)
    d1 = jnp.sum(jnp.where(oh1, before, 0.0), axis=0, keepdims=True)
    dest_ref[0, 0:1, :] = d0.astype(I32)
    dest_ref[0, 1:2, :] = d1.astype(I32)


def _dest_rows(eid, base, tri):
    tm = tri.shape[0]
    nt = eid.shape[1] // tm
    return pl.pallas_call(
        _dest_kernel,
        grid=(nt,),
        in_specs=[
            pl.BlockSpec((2, tm), lambda i: (0, i)),
            pl.BlockSpec((1, N_EXPERTS, 1), lambda i: (i, 0, 0)),
            pl.BlockSpec((tm, tm), lambda i: (0, 0)),
        ],
        out_specs=pl.BlockSpec((1, 2, tm), lambda i: (i, 0, 0)),
        out_shape=jax.ShapeDtypeStruct((nt, 2, tm), I32),
        compiler_params=_cparams(("parallel",)),
        name="moe_dest",
    )(eid, base, tri)


def _scatter_kernel(*refs, tiles):
    dest_ref = refs[0]
    hf_refs = refs[1:1 + len(tiles)]
    xs_ref, sem = refs[1 + len(tiles):]
    tm = hf_refs[0].shape[0]
    i = pl.program_id(0)

    def scatter_tile(hf_ref):
        def issue(t8, carry):
            base = pl.multiple_of(t8 * ROW_UNROLL, ROW_UNROLL)
            rows = hf_ref.at[pl.ds(base, ROW_UNROLL)]
            for u in range(ROW_UNROLL):
                for k in range(2):
                    d = dest_ref.at[0, k, pl.ds(base, ROW_UNROLL)][u]
                    pltpu.make_async_copy(rows.at[pl.ds(u, 1)], xs_ref.at[d], sem).start(priority=k)
            return carry

        lax.fori_loop(0, tm // ROW_UNROLL, issue, 0)
        for k in range(2):
            pltpu.make_async_copy(hf_ref, xs_ref.at[pl.ds(0, tm), 0], sem).wait()

    first = 0
    for hf_ref, nt in zip(hf_refs, tiles):
        pl.when((i >= first) & (i < first + nt))(functools.partial(scatter_tile, hf_ref))
        first += nt


def _scatter_rows(dest, hf_streams, n_rows):
    nt_all, _, tm = dest.shape
    d = hf_streams[0].shape[1]
    tiles = tuple(h.shape[0] // tm for h in hf_streams)
    assert sum(tiles) == nt_all
    in_specs = [pl.BlockSpec((1, 2, tm), lambda i: (i, 0, 0), memory_space=pltpu.SMEM)]
    first = 0
    for nt in tiles:
        in_specs.append(pl.BlockSpec((tm, d), lambda i, first=first, nt=nt: (jnp.clip(i - first, 0, nt - 1), 0)))
        first += nt
    return pl.pallas_call(
        functools.partial(_scatter_kernel, tiles=tiles),
        grid=(nt_all,),
        in_specs=in_specs,
        out_specs=pl.BlockSpec(memory_space=pl.ANY),
        out_shape=jax.ShapeDtypeStruct((n_rows, 1, d), hf_streams[0].dtype),
        scratch_shapes=[pltpu.SemaphoreType.DMA(())],
        compiler_params=_cparams(("arbitrary",)),
        name="moe_scatter",
    )(dest, *hf_streams)


def _expert_kernel(ve_ref, vblk_ref, lo_ref, hi_ref, x_hbm, wg_ref, wu_ref, wd_ref, y_hbm, wg_s, wu_s, wd_s,
                   xbuf, xsems, ybuf, ysems, *, n_blocks):
    v = pl.program_id(0)
    n_visits = pl.num_programs(0)
    pv = jnp.maximum(v - 1, 0)
    nv = jnp.minimum(v + 1, n_visits - 1)
    blk = vblk_ref[v]
    new_expert = (v == 0) | (ve_ref[v] != ve_ref[pv])
    first_visit = (v == 0) | (blk != vblk_ref[pv])
    last_visit = (v == n_visits - 1) | (blk != vblk_ref[nv])
    lo = lo_ref[v]
    hi = hi_ref[v]
    xslot = v % 2
    yslot = blk % 2

    def x_copy(visit, slot):
        start = pl.multiple_of(vblk_ref[visit] * EXPERT_BLOCK, EXPERT_BLOCK)
        return pltpu.make_async_copy(x_hbm.at[pl.ds(start, EXPERT_BLOCK), 0], xbuf.at[slot], xsems.at[slot])

    def y_copy(slot):
        start = pl.multiple_of(blk * EXPERT_BLOCK, EXPERT_BLOCK)
        return pltpu.make_async_copy(ybuf.at[slot], y_hbm.at[pl.ds(start, EXPERT_BLOCK), 0], ysems.at[slot])

    @pl.when(v == 0)
    def _():
        x_copy(0, 0).start()

    @pl.when(v + 1 < n_visits)
    def _():
        x_copy(v + 1, 1 - xslot).start()

    @pl.when(new_expert)
    def _():
        wg_s[...] = wg_ref[0, 0].astype(BF16)
        wu_s[...] = wu_ref[0, 0].astype(BF16)
        wd_s[...] = wd_ref[0, 0].astype(BF16)

    @pl.when(first_visit & (blk >= 2))
    def _():
        y_copy(yslot).wait()

    x_copy(v, xslot).wait()

    @pl.when(hi > lo)
    def _():
        x_lo, x_hi = _unpack_rows(xbuf[xslot])
        x_lo = x_lo.astype(BF16)
        x_hi = x_hi.astype(BF16)
        half = x_lo.shape[1]
        gate = _dot(x_lo, wg_s[:half]) + _dot(x_hi, wg_s[half:])
        up = _dot(x_lo, wu_s[:half]) + _dot(x_hi, wu_s[half:])
        mid = gate * jax.nn.sigmoid(gate) * up
        y = _pack_rows(_dot(mid.astype(BF16), wd_s[...]))
        row = lax.broadcasted_iota(I32, (EXPERT_BLOCK, 1), 0)
        mine = (row >= lo) & (row < hi)

        @pl.when(first_visit)
        def _():
            ybuf[yslot] = jnp.where(mine, y, jnp.uint32(0))

        @pl.when(jnp.logical_not(first_visit))
        def _():
            ybuf[yslot] = jnp.where(mine, y, ybuf[yslot])

    @pl.when(last_visit)
    def _():
        y_copy(yslot).start()

    @pl.when(v == n_visits - 1)
    def _():
        y_copy(yslot).wait()
        if n_blocks >= 2:
            y_copy(1 - yslot).wait()


def _experts(visits, xs, w_gate, w_up, w_down, layer):
    n_rows, _, dp = xs.shape
    _, _, d, f = w_gate.shape
    assert d == 2 * dp and n_rows % EXPERT_BLOCK == 0
    ve, vblk, lo, hi = visits
    w_map = lambda v, ve_r, vb_r, lo_r, hi_r: (layer, ve_r[v], 0, 0)
    grid_spec = pltpu.PrefetchScalarGridSpec(
        num_scalar_prefetch=4,
        grid=(ve.shape[0],),
        in_specs=[
            pl.BlockSpec(memory_space=pl.ANY),
            pl.BlockSpec((1, 1, d, f), w_map),
            pl.BlockSpec((1, 1, d, f), w_map),
            pl.BlockSpec((1, 1, f, d), w_map),
        ],
        out_specs=pl.BlockSpec(memory_space=pl.ANY),
        scratch_shapes=[pltpu.VMEM((d, f), BF16), pltpu.VMEM((d, f), BF16), pltpu.VMEM((f, d), BF16),
                        pltpu.VMEM((2, EXPERT_BLOCK, dp), U32), pltpu.SemaphoreType.DMA((2,)),
                        pltpu.VMEM((2, EXPERT_BLOCK, dp), U32), pltpu.SemaphoreType.DMA((2,))],
    )
    return pl.pallas_call(
        functools.partial(_expert_kernel, n_blocks=n_rows // EXPERT_BLOCK),
        grid_spec=grid_spec,
        out_shape=jax.ShapeDtypeStruct((n_rows, 1, dp), U32),
        compiler_params=_cparams(("arbitrary",)),
        name="moe_experts",
    )(ve, vblk, lo, hi, xs, w_gate, w_up, w_down)


def _combine_kernel(*refs, final_norm):
    if final_norm:
        dest_ref, dest_next_ref, xn_ref, wcol_ref, gt_ref, gfin_ref, y_ref, o_ref, ybuf, sems = refs
    else:
        dest_ref, dest_next_ref, xn_ref, wcol_ref, gt_ref, y_ref, o_ref, ybuf, sems = refs
    tm = xn_ref.shape[1]
    step = pl.program_id(0) * pl.num_programs(1) + pl.program_id(1)
    n_steps = pl.num_programs(0) * pl.num_programs(1)
    slot = step % 2

    def gather_tile(d_ref, into):
        def issue(t8, carry):
            base = pl.multiple_of(t8 * ROW_UNROLL, ROW_UNROLL)
            for k in range(2):
                rows = ybuf.at[into, k, pl.ds(base, ROW_UNROLL)]
                for u in range(ROW_UNROLL):
                    d = d_ref.at[0, k, pl.ds(base, ROW_UNROLL)][u]
                    pltpu.make_async_copy(y_ref.at[d], rows.at[pl.ds(u, 1)], sems.at[into]).start(priority=k)
            return carry

        lax.fori_loop(0, tm // ROW_UNROLL, issue, 0)

    pl.when(step == 0)(functools.partial(gather_tile, dest_ref, 0))
    pl.when(step + 1 < n_steps)(functools.partial(gather_tile, dest_next_ref, 1 - slot))

    for k in range(2):
        pltpu.make_async_copy(y_ref.at[pl.ds(0, tm), 0], ybuf.at[slot, k], sems.at[slot]).wait()
    w = wcol_ref[...]
    lo0, hi0 = _unpack_rows(ybuf[slot, 0])
    lo1, hi1 = _unpack_rows(ybuf[slot, 1])
    moe = jnp.concatenate([w[:, 0:1] * lo0 + w[:, 1:2] * lo1, w[:, 0:1] * hi0 + w[:, 1:2] * hi1], axis=1)
    out = xn_ref[0] + gt_ref[0] * moe
    if final_norm:
        ms = jnp.mean(out * out, axis=-1, keepdims=True)
        out = out * lax.rsqrt(ms + NORM_EPS) * gfin_ref[...]
    o_ref[0] = out


def _combine(dest, xn, wcol, modr, y, layer, ctx_row, g_final=None):
    b, n, d = xn.shape
    nt_all, _, tm = dest.shape
    nt = n // tm
    final_norm = g_final is not None
    args = [dest, dest, xn, wcol, modr]
    in_specs = [
        pl.BlockSpec((1, 2, tm), lambda bb, t: (bb * nt + t, 0, 0), memory_space=pltpu.SMEM),
        pl.BlockSpec((1, 2, tm), lambda bb, t: (jnp.minimum(bb * nt + t + 1, nt_all - 1), 0, 0),
                     memory_space=pltpu.SMEM),
        pl.BlockSpec((1, tm, d), lambda bb, t: (bb, t, 0)),
        pl.BlockSpec((tm, ROUTER_COLS), lambda bb, t: (bb * nt + t, 0)),
        _mod_spec(layer, 5, ctx_row),
    ]
    if final_norm:
        args.append(g_final.reshape(1, d))
        in_specs.append(pl.BlockSpec((1, d), lambda bb, t: (0, 0)))
    args.append(y)
    in_specs.append(pl.BlockSpec(memory_space=pl.ANY))
    return pl.pallas_call(
        functools.partial(_combine_kernel, final_norm=final_norm),
        grid=(b, nt),
        in_specs=in_specs,
        out_specs=pl.BlockSpec((1, tm, d), lambda bb, t: (bb, t, 0)),
        out_shape=jax.ShapeDtypeStruct((b, n, d), F32),
        scratch_shapes=[pltpu.VMEM((2, 2, tm, y.shape[2]), y.dtype), pltpu.SemaphoreType.DMA((2,))],
        compiler_params=_cparams(("arbitrary", "arbitrary")),
        name="moe_combine",
    )(*args)


def _rope_tables(n):
    t = np.arange(n)
    row = (t // GRID_W).astype(np.float32)
    col = (t % GRID_W).astype(np.float32)
    quarter = HEAD_DIM // 4
    inv = jnp.asarray(ROPE_THETA, F32) ** (-jnp.arange(quarter, dtype=F32) / quarter)
    ar = jnp.asarray(row)[:, None] * inv
    ac = jnp.asarray(col)[:, None] * inv
    ang = jnp.concatenate([ar, ar, ac, ac], axis=-1)
    ang = jnp.concatenate([ang, ang], axis=-1)
    sign = np.where((np.arange(LANES) % (HEAD_DIM // 2)) < quarter, -1.0, 1.0).astype(np.float32)
    return jnp.cos(ang), jnp.sin(ang) * sign


def _na_bias_tables(rpb):
    n_dr, n_dc = 2 * NA_WIN_H - 1, 2 * NA_WIN_W - 1
    blocked = jnp.full((N_HEADS, 2, n_dc), NEG_INF, F32)
    rows = jnp.concatenate([blocked, rpb.astype(F32) * LOG2E, blocked], axis=1)
    assert rows.shape[1] == n_dr + 4 == NA_TABLE + 1
    pair = jnp.stack([rows[:, 0:NA_TABLE], rows[:, 1:NA_TABLE + 1]], axis=2)
    c = np.arange(GRID_W)[:, None]
    kc = np.arange(GRID_W)[None, :]
    cs = np.clip(c - NA_WIN_W // 2, 0, GRID_W - NA_WIN_W)
    in_window = (kc >= cs) & (kc < cs + NA_WIN_W)
    out = jnp.full((N_HEADS, NA_TABLE, GRID_W, 2, GRID_W), NEG_INF, F32)
    for dc in range(-(NA_WIN_W - 1), NA_WIN_W):
        hit = ((kc - c) == dc) & in_window
        if hit.any():
            out = jnp.where(hit[None, None, :, None, :], pair[:, :, None, :, dc + NA_WIN_W - 1, None], out)
    return out.reshape(N_HEADS, NA_TABLE, GRID_W, LANES)


def _router_tables(w_group, b_group, w_router, b_router):
    d = w_group.shape[0]
    w = jnp.zeros((d, ROUTER_COLS), F32)
    w = w.at[:, 0:N_GROUPS].set(w_group).at[:, 8:8 + N_EXPERTS].set(w_router)
    bias = jnp.zeros((1, ROUTER_COLS), F32)
    bias = bias.at[0, 0:N_GROUPS].set(b_group).at[0, 8:8 + N_EXPERTS].set(b_router)
    hi = w.astype(BF16)
    lo = (w - hi.astype(F32)).astype(BF16)
    return jnp.concatenate([hi, lo], axis=1), hi, bias


def _pick(table, onehot):
    return jnp.sum(jnp.where(onehot, table[None, :], 0), axis=1)


def _moe_layout(cnt_tiles, n_rows):
    cnt = cnt_tiles[:, :, 0]
    counts = jnp.sum(cnt, axis=0)
    ends = jnp.cumsum(counts)
    starts = ends - counts
    base = starts[None, :] + jnp.cumsum(cnt, axis=0) - cnt
    first_blk = starts // EXPERT_BLOCK
    n_blk = jnp.where(counts > 0, (ends - 1) // EXPERT_BLOCK - first_blk + 1, 0)
    v_end = jnp.cumsum(n_blk)
    v_off = v_end - n_blk
    n_visits = n_rows // EXPERT_BLOCK + N_EXPERTS
    v = jnp.arange(n_visits, dtype=I32)
    valid = v < v_end[-1]
    vv = jnp.minimum(v, v_end[-1] - 1)
    ve = jnp.sum((v_end[None, :] <= vv[:, None]).astype(I32), axis=1)
    onehot = ve[:, None] == jnp.arange(N_EXPERTS, dtype=I32)[None, :]
    vblk = _pick(first_blk, onehot) + vv - _pick(v_off, onehot)
    lo = jnp.clip(_pick(starts, onehot) - vblk * EXPERT_BLOCK, 0, EXPERT_BLOCK)
    hi = jnp.clip(_pick(ends, onehot) - vblk * EXPERT_BLOCK, 0, EXPERT_BLOCK)
    hi = jnp.where(valid, hi, lo)
    visits = tuple(a.astype(I32) for a in (ve, vblk, lo, hi))
    return base[:, :, None].astype(I32), visits


def kernel(x, c, ctx, c_ctx, w_ada, b_ada, g_attn, w_qkv, w_o, sink_a, gq_b, gk_b, rpb_c, g_ffn,
           w_group, b_group, w_router, b_router, w_gate, w_up, w_down, g_final):
    b, n, d = x.shape
    n_ctx = ctx.shape[1]
    depth = w_ada.shape[0]
    tm_tok = 1024
    tm_qkv = 512
    tm_ctx = 256
    assert b + 1 <= MOD_ROWS and d == D_MODEL and n % tm_tok == 0 and n_ctx % tm_ctx == 0
    assert (b * n_ctx) % tm_tok == 0
    ctx_row = b

    cc = jnp.zeros((MOD_ROWS, d), F32).at[:b].set(c).at[b].set(c_ctx)
    modr = _ada_all(cc, w_ada, b_ada).reshape(depth * MOD_ROWS * N_MOD, 1, d)
    g_attn_rows = g_attn.reshape(depth, 1, d)
    g_ffn_rows = g_ffn.reshape(depth, 1, d)
    rope_tabs = _rope_tables(n)
    head_sum = jnp.asarray(np.kron(np.eye(LANES // HEAD_DIM), np.ones((HEAD_DIM, HEAD_DIM))), BF16)
    tri = jnp.asarray(np.triu(np.ones((tm_tok, tm_tok))), BF16)

    def layer_qkv_config(layer):
        mixer, idx = layer % N_MIXERS, layer // N_MIXERS
        gains = None
        if mixer == 1:
            gains = (jnp.tile(gq_b[idx], LANES // HEAD_DIM).reshape(1, LANES),
                     jnp.tile(gk_b[idx], LANES // HEAD_DIM).reshape(1, LANES), head_sum)
        return w_qkv[layer].astype(BF16), (rope_tabs if mixer != 2 else None), gains

    xc = ctx
    for i in range(depth):
        m = i % N_MIXERS
        j = i // N_MIXERS
        last = i == depth - 1
        wqkv_bf, rope_i, qk_gains = layer_qkv_config(i)
        wo_bf = w_o[i].astype(BF16)

        latent_qkv = _qkv(x, modr, g_attn_rows, wqkv_bf, i, None, tm_qkv, rope_i, qk_gains)
        if m == 2:
            qp, k, v = latent_qkv
            qr = None
        else:
            qp, qr, k, v = latent_qkv
        qc, kc, vc = _qkv(xc, modr, g_attn_rows, wqkv_bf, i, ctx_row, tm_ctx, None, qk_gains)
        sink = sink_a[j] if m == 0 else None
        extra = sink if m == 0 else (_na_bias_tables(rpb_c[j]) if m == 2 else None)
        o = _attention(m, qp, qr, k, v, kc, vc, extra)

        router = _router_tables(w_group[i], b_group[i], w_router[i], b_router[i])
        streams = [(o, x, None)]
        if not last:
            oc = _ctx_attention(qc, kc, vc, sink)
            streams.append((oc.reshape(1, b * n_ctx, d), xc.reshape(1, b * n_ctx, d), ctx_row))
        routed = [_post_attn(o_s, x_s, wo_bf, modr, g_ffn_rows, router, i, row, tm_tok) for (o_s, x_s, row) in streams]

        n_rows = 2 * sum(r[2].shape[1] for r in routed)
        assert n_rows % EXPERT_BLOCK == 0
        base, visits = _moe_layout(jnp.concatenate([r[4] for r in routed], axis=0), n_rows)
        dest = _dest_rows(jnp.concatenate([r[2] for r in routed], axis=1), base, tri)
        xs = _scatter_rows(dest, [r[1].reshape(-1, d // 2) for r in routed], n_rows)
        y = _experts(visits, xs, w_gate, w_up, w_down, i)
        nt_lat = routed[0][4].shape[0]
        x = _combine(dest[:nt_lat], routed[0][0], routed[0][3], modr, y, i, None, g_final if last else None)
        if not last:
            xc = _combine(dest[nt_lat:], routed[1][0], routed[1][3], modr, y, i, ctx_row).reshape(b, n_ctx, d)
    return x
```

```python
import functools

import numpy as np
import jax
import jax.numpy as jnp
from jax import lax
from jax.experimental import pallas as pl
from jax.experimental.pallas import tpu as pltpu

F32 = jnp.float32
BF16 = jnp.bfloat16
I32 = jnp.int32
U32 = jnp.uint32

D_MODEL = 1024
GRID_W = 64
N_MIXERS = 3
N_HEADS = 16
N_KV_HEADS = 4
HEAD_DIM = D_MODEL // N_HEADS
Q_PER_KV = N_HEADS // N_KV_HEADS
QKV_DIM = (N_HEADS + 2 * N_KV_HEADS) * HEAD_DIM
ATTN_SCALE = HEAD_DIM ** -0.5
ROPE_THETA = 10000.0
ATTN_BLOCK_Q = (128, 256, 128)
WINDOW = 128
NA_WIN_H = 8
NA_WIN_W = 16
N_GROUPS = 4
EXPERTS_PER_GROUP = 8
N_EXPERTS = N_GROUPS * EXPERTS_PER_GROUP
EXPERT_HIDDEN = D_MODEL // 2
NORM_EPS = 1e-6
NEG_INF = -1e30
LOG2E = 1.4426950408889634

LANES = 128
MOD_ROWS = 24
N_MOD = 6
EXPERT_BLOCK = 512
ROW_UNROLL = 8
COMBINE_TILE = 512
ROUTER_COLS = LANES
NA_KEY_ROWS = 10
NA_TABLE = 18
VMEM_LIMIT = 56 * 1024 * 1024


def _cparams(sem, vmem=VMEM_LIMIT):
    return pltpu.CompilerParams(dimension_semantics=sem, vmem_limit_bytes=vmem)


def _dot(a, b):
    return jnp.dot(a, b, preferred_element_type=F32)


def _dot_nt(a, b):
    return lax.dot_general(a, b, (((1,), (1,)), ((), ())), preferred_element_type=F32)


def _split_bf16(x):
    hi = x.astype(BF16)
    lo = (x - hi.astype(F32)).astype(BF16)
    return hi, lo


def _pack_rows(x):
    w = x.shape[1] // 2
    bits = pltpu.bitcast(x.astype(BF16).astype(F32), U32)
    return (bits[:, w:] & jnp.uint32(0xFFFF0000)) | (bits[:, :w] >> 16)


def _unpack_rows(p):
    lo = pltpu.bitcast(p << 16, F32)
    hi = pltpu.bitcast(p & jnp.uint32(0xFFFF0000), F32)
    return lo, hi


def _rms_mod(x, g, sh, sc):
    ms = jnp.mean(x * x, axis=-1, keepdims=True)
    y = x * lax.rsqrt(ms + NORM_EPS) * g
    return y * (1.0 + sc) + sh


def _ada_kernel(c_ref, w_ref, b_ref, o_ref):
    c = c_ref[...]
    s = c * jax.nn.sigmoid(c)
    o_ref[0] = _dot(s.astype(BF16), w_ref[0].astype(BF16)) + b_ref[0]


def _ada_all(cc, w_ada, b_ada):
    depth, d, d6 = w_ada.shape
    tn = 1536
    return pl.pallas_call(
        _ada_kernel,
        grid=(depth, d6 // tn),
        in_specs=[
            pl.BlockSpec((MOD_ROWS, d), lambda i, j: (0, 0)),
            pl.BlockSpec((1, d, tn), lambda i, j: (i, 0, j)),
            pl.BlockSpec((1, 1, tn), lambda i, j: (i, 0, j)),
        ],
        out_specs=pl.BlockSpec((1, MOD_ROWS, tn), lambda i, j: (i, 0, j)),
        out_shape=jax.ShapeDtypeStruct((depth, MOD_ROWS, d6), F32),
        compiler_params=_cparams(("parallel", "parallel")),
        name="ada_mod",
    )(cc, w_ada, b_ada.reshape(depth, 1, d6))


def _mod_spec(layer, chunk, ctx_row):
    if ctx_row is None:
        return pl.BlockSpec((1, 1, D_MODEL), lambda b, t: ((layer * MOD_ROWS + b) * N_MOD + chunk, 0, 0))
    return pl.BlockSpec((1, 1, D_MODEL), lambda b, t: ((layer * MOD_ROWS + ctx_row) * N_MOD + chunk, 0, 0))


def _layer_row_spec(layer):
    return pl.BlockSpec((1, 1, D_MODEL), lambda b, t: (layer, 0, 0))


def _qkv_kernel(*refs, qk_norm, rope):
    _qkv_project(refs[0][0], refs[1:], qk_norm=qk_norm, rope=rope)


def _qkv_project(x, refs, *, qk_norm, rope):
    g_ref, sh_ref, sc_ref, w_ref = refs[:4]
    pos = 4
    if rope:
        cos_ref, sin_ref = refs[pos:pos + 2]
        pos += 2
    if qk_norm:
        gq_ref, gk_ref, bm_ref = refs[pos:pos + 3]
        pos += 3
    if rope:
        qp_ref, qr_ref, k_ref, v_ref = refs[pos:]
    else:
        qp_ref, k_ref, v_ref = refs[pos:]

    h = _rms_mod(x, g_ref[0], sh_ref[0], sc_ref[0])
    res = _dot(h.astype(BF16), w_ref[...])
    half = LANES // 2
    n_q = N_HEADS * HEAD_DIM // LANES
    n_k = N_KV_HEADS * HEAD_DIM // LANES
    if rope:
        lane = lax.broadcasted_iota(I32, (1, LANES), 1)
        first_half = (lane & (HEAD_DIM // 2 - 1)) < (HEAD_DIM // 4)
    for cb in range(QKV_DIM // LANES):
        xs = res[:, cb * LANES:(cb + 1) * LANES]
        is_q = cb < n_q
        is_k = n_q <= cb < n_q + n_k
        if qk_norm and (is_q or is_k):
            hi, lo = _split_bf16(xs * xs)
            ms = (_dot(hi, bm_ref[...]) + _dot(lo, bm_ref[...])) * (1.0 / HEAD_DIM)
            gain = gq_ref[...] if is_q else gk_ref[...]
            xs = xs * lax.rsqrt(ms + NORM_EPS) * gain
        if is_q:
            xs = xs * (ATTN_SCALE * LOG2E)
        if rope and (is_q or is_k):
            rot = jnp.where(first_half, pltpu.roll(xs, LANES - HEAD_DIM // 4, 1), pltpu.roll(xs, HEAD_DIM // 4, 1))
            xr = xs * cos_ref[...] + rot * sin_ref[...]
        if is_q:
            hd = 2 * cb
            qp_ref[0, hd] = xs[:, :half].astype(BF16)
            qp_ref[0, hd + 1] = xs[:, half:].astype(BF16)
            if rope:
                qr_ref[0, hd] = xr[:, :half].astype(BF16)
                qr_ref[0, hd + 1] = xr[:, half:].astype(BF16)
        elif is_k:
            hd = 2 * (cb - n_q)
            kk = xr if rope else xs
            k_ref[0, hd] = kk[:, :half].astype(BF16)
            k_ref[0, hd + 1] = kk[:, half:].astype(BF16)
        else:
            hd = 2 * (cb - n_q - n_k)
            low = lax.broadcasted_iota(I32, (1, LANES), 1) < HEAD_DIM
            swapped = pltpu.roll(xs, half, 1)
            for par, (a, b) in enumerate(((xs, swapped), (swapped, xs))):
                v_aug = jnp.concatenate([jnp.where(low, a, 1.0), jnp.where(low, 1.0, b)], axis=1)
                v_ref[0, hd + par] = v_aug.astype(BF16)


def _qkv_operands(b, n, modr, g_rows, w_bf, layer, ctx_row, tm, rope_tabs, qk_gains):
    d = D_MODEL
    rope = rope_tabs is not None
    args = [g_rows, modr, modr, w_bf]
    in_specs = [
        _layer_row_spec(layer),
        _mod_spec(layer, 0, ctx_row),
        _mod_spec(layer, 1, ctx_row),
        pl.BlockSpec((d, QKV_DIM), lambda bb, t: (0, 0)),
    ]
    if rope:
        args += list(rope_tabs)
        in_specs += [pl.BlockSpec((tm, LANES), lambda bb, t: (t, 0))] * 2
    if qk_gains is not None:
        args += list(qk_gains)
        in_specs += [pl.BlockSpec((1, LANES), lambda bb, t: (0, 0))] * 2
        in_specs += [pl.BlockSpec((LANES, LANES), lambda bb, t: (0, 0))]
    q_shape = jax.ShapeDtypeStruct((b, N_HEADS, n, HEAD_DIM), BF16)
    kv_shape = jax.ShapeDtypeStruct((b, N_KV_HEADS, n, HEAD_DIM), BF16)
    q_spec = pl.BlockSpec((1, N_HEADS, tm, HEAD_DIM), lambda bb, t: (bb, 0, t, 0))
    kv_spec = pl.BlockSpec((1, N_KV_HEADS, tm, HEAD_DIM), lambda bb, t: (bb, 0, t, 0))
    v_shape = jax.ShapeDtypeStruct((b, N_KV_HEADS, n, 2 * LANES), BF16)
    v_spec = pl.BlockSpec((1, N_KV_HEADS, tm, 2 * LANES), lambda bb, t: (bb, 0, t, 0))
    if rope:
        out_shape, out_specs = [q_shape, q_shape, kv_shape, v_shape], [q_spec, q_spec, kv_spec, v_spec]
    else:
        out_shape, out_specs = [q_shape, kv_shape, v_shape], [q_spec, kv_spec, v_spec]
    return args, in_specs, out_shape, out_specs


def _qkv(x, modr, g_rows, w_bf, layer, ctx_row, tm, rope_tabs=None, qk_gains=None):
    b, n, d = x.shape
    args, in_specs, out_shape, out_specs = _qkv_operands(b, n, modr, g_rows, w_bf, layer, ctx_row, tm,
                                                         rope_tabs, qk_gains)
    return pl.pallas_call(
        functools.partial(_qkv_kernel, qk_norm=qk_gains is not None, rope=rope_tabs is not None),
        grid=(b, n // tm),
        in_specs=[pl.BlockSpec((1, tm, d), lambda bb, t: (bb, t, 0))] + in_specs,
        out_specs=out_specs,
        out_shape=out_shape,
        compiler_params=_cparams(("parallel", "parallel")),
        name="qkv_rope" if rope_tabs is not None else "qkv",
    )(x, *args)


def _attend(pieces, m_prev=None, acc_prev=None):
    m = m_prev
    for s, _ in pieces:
        m_s = jnp.max(s, axis=-1, keepdims=True)
        m = m_s if m is None else jnp.maximum(m, m_s)
    acc = None if acc_prev is None else jnp.exp2(m_prev - m) * acc_prev
    for s, v_aug in pieces:
        pv = _dot(jnp.exp2(s - m).astype(BF16), v_aug)
        acc = pv if acc is None else acc + pv
    return m, acc


def _head_pair_outputs(m, acc, rows_per_head, sink_ref=None, kv_head=None):
    low = lax.broadcasted_iota(I32, (1, LANES), 1) < HEAD_DIM
    blocks = []
    for pair in range(Q_PER_KV // 2):
        forms = []
        for par in range(2):
            g = 2 * pair + par
            rows = slice(g * rows_per_head, (g + 1) * rows_per_head)
            num = acc[rows, par * LANES:(par + 1) * LANES]
            den = acc[rows, (1 - par) * LANES:(2 - par) * LANES]
            if sink_ref is not None:
                sink = sink_ref[kv_head * Q_PER_KV + g] * LOG2E
                m2 = jnp.maximum(m[rows], sink)
                scale = jnp.exp2(m[rows] - m2)
                num = num * scale
                den = den * scale + jnp.exp2(sink - m2)
            forms.append(num / den)
        blocks.append(jnp.where(low, forms[0], forms[1]))
    return blocks


def _attn_kernel(*refs, mode, bq, key_chunk):
    if mode == 0:
        qp_ref, qr_ref, k_ref, v_ref, kc_ref, vc_ref, sink_ref, o_ref = refs
    elif mode == 1:
        qp_ref, qr_ref, k_ref, v_ref, kc_ref, vc_ref, o_ref = refs
    else:
        qp_ref, k_ref, v_ref, kc_ref, vc_ref, t2_ref, o_ref = refs
    qb = pl.program_id(1)
    n = k_ref.shape[2]
    rows = Q_PER_KV * bq

    if mode == 0:
        span = bq + 2 * WINDOW
        kstart = pl.multiple_of(jnp.clip(qb * bq - WINDOW, 0, n - span), WINDOW)
        qpos = qb * bq + (lax.broadcasted_iota(I32, (rows, span), 0) & (bq - 1))
        kpos = kstart + lax.broadcasted_iota(I32, (rows, span), 1)
        band = jnp.abs(kpos - qpos) <= WINDOW
    elif mode == 2:
        rows_grid = n // GRID_W
        r0 = qb * (bq // GRID_W)
        wr = jnp.clip(r0 - NA_WIN_H // 2, 0, rows_grid - NA_KEY_ROWS)
        kstart = pl.multiple_of(wr * GRID_W, GRID_W)
        lane = lax.broadcasted_iota(I32, (GRID_W, LANES), 1)

    pieces = []
    for h in range(N_KV_HEADS):
        qp = qp_ref[0, h * Q_PER_KV:(h + 1) * Q_PER_KV].reshape(rows, HEAD_DIM)
        ctx_piece = (_dot_nt(qp, kc_ref[0, h]), vc_ref[0, h])
        if mode == 0:
            qr = qr_ref[0, h * Q_PER_KV:(h + 1) * Q_PER_KV].reshape(rows, HEAD_DIM)
            s = jnp.where(band, _dot_nt(qr, k_ref[0, h, pl.ds(kstart, span), :]), NEG_INF)
            m, acc = _attend([(s, v_ref[0, h, pl.ds(kstart, span), :]), ctx_piece])
            pieces += _head_pair_outputs(m, acc, bq, sink_ref, h)
        elif mode == 1:
            qr = qr_ref[0, h * Q_PER_KV:(h + 1) * Q_PER_KV].reshape(rows, HEAD_DIM)
            n_chunks = n // key_chunk
            m, acc = None, None
            for c in range(n_chunks):
                keys = slice(c * key_chunk, (c + 1) * key_chunk)
                step = [(_dot_nt(qr, k_ref[0, h, keys, :]), v_ref[0, h, keys, :])]
                if c == n_chunks - 1:
                    step.append(ctx_piece)
                m, acc = _attend(step, m, acc)
            pieces += _head_pair_outputs(None, acc, bq)
        else:
            n_keys = NA_KEY_ROWS * GRID_W
            s = _dot_nt(qp, k_ref[0, h, pl.ds(kstart, n_keys), :])
            s_ctx, v_ctx = ctx_piece
            p_lat, p_ctx = [], []
            for g in range(Q_PER_KV):
                for a in range(bq // GRID_W):
                    r = r0 + a
                    rs = jnp.clip(r - NA_WIN_H // 2, 0, rows_grid - NA_WIN_H)
                    tiles = []
                    for j in range(n_keys // LANES):
                        kr0 = wr + 2 * j
                        tab = t2_ref[h * Q_PER_KV + g, kr0 - r + NA_TABLE // 2]
                        pen0 = jnp.where((kr0 >= rs) & (kr0 < rs + NA_WIN_H), 0.0, NEG_INF)
                        pen1 = jnp.where((kr0 + 1 >= rs) & (kr0 + 1 < rs + NA_WIN_H), 0.0, NEG_INF)
                        tiles.append(tab + jnp.where(lane < GRID_W, pen0, pen1))
                    qrows = slice(g * bq + a * GRID_W, g * bq + (a + 1) * GRID_W)
                    sb = s[qrows] + jnp.concatenate(tiles, axis=1)
                    sc = s_ctx[qrows]
                    m = jnp.maximum(jnp.max(sb, axis=-1, keepdims=True), jnp.max(sc, axis=-1, keepdims=True))
                    p_lat.append(jnp.exp2(sb - m).astype(BF16))
                    p_ctx.append(jnp.exp2(sc - m).astype(BF16))
            acc = (_dot(jnp.concatenate(p_lat, axis=0), v_ref[0, h, pl.ds(kstart, n_keys), :])
                   + _dot(jnp.concatenate(p_ctx, axis=0), v_ctx))
            pieces += _head_pair_outputs(None, acc, bq)
    o_ref[0] = jnp.concatenate(pieces, axis=-1).astype(o_ref.dtype)


def _attention(mode, qp, qr, k, v, kc, vc, extra):
    b, _, n, _ = qp.shape
    c = kc.shape[2]
    bq = ATTN_BLOCK_Q[mode]
    q_spec = pl.BlockSpec((1, N_HEADS, bq, HEAD_DIM), lambda bb, i: (bb, 0, i, 0))
    k_spec = pl.BlockSpec((1, N_KV_HEADS, n, HEAD_DIM), lambda bb, i: (bb, 0, 0, 0))
    v_spec = pl.BlockSpec((1, N_KV_HEADS, n, 2 * LANES), lambda bb, i: (bb, 0, 0, 0))
    kc_spec = pl.BlockSpec((1, N_KV_HEADS, c, HEAD_DIM), lambda bb, i: (bb, 0, 0, 0))
    vc_spec = pl.BlockSpec((1, N_KV_HEADS, c, 2 * LANES), lambda bb, i: (bb, 0, 0, 0))
    if mode == 2:
        args = [qp, k, v, kc, vc, extra]
        in_specs = [q_spec, k_spec, v_spec, kc_spec, vc_spec,
                    pl.BlockSpec(extra.shape, lambda bb, i: (0, 0, 0, 0))]
    else:
        args = [qp, qr, k, v, kc, vc]
        in_specs = [q_spec, q_spec, k_spec, v_spec, kc_spec, vc_spec]
        if mode == 0:
            args.append(extra)
            in_specs.append(pl.BlockSpec(memory_space=pltpu.SMEM))
    return pl.pallas_call(
        functools.partial(_attn_kernel, mode=mode, bq=bq, key_chunk=512),
        grid=(b, n // bq),
        in_specs=in_specs,
        out_specs=pl.BlockSpec((1, bq, D_MODEL), lambda bb, i: (bb, i, 0)),
        out_shape=jax.ShapeDtypeStruct((b, n, D_MODEL), BF16),
        compiler_params=_cparams(("parallel", "arbitrary")),
        name=("attn_window", "attn_global", "attn_na")[mode],
    )(*args)


def _ctx_attn_kernel(*refs, has_sink):
    if has_sink:
        q_ref, k_ref, v_ref, sink_ref, o_ref = refs
    else:
        q_ref, k_ref, v_ref, o_ref = refs
    c = k_ref.shape[2]
    rows = Q_PER_KV * c
    pieces = []
    for h in range(N_KV_HEADS):
        q = q_ref[0, h * Q_PER_KV:(h + 1) * Q_PER_KV].reshape(rows, HEAD_DIM)
        m, acc = _attend([(_dot_nt(q, k_ref[0, h]), v_ref[0, h])])
        if has_sink:
            pieces += _head_pair_outputs(m, acc, c, sink_ref, h)
        else:
            pieces += _head_pair_outputs(None, acc, c)
    o_ref[0] = jnp.concatenate(pieces, axis=-1).astype(o_ref.dtype)


def _ctx_attention(qc, kc, vc, sink):
    b, _, c, _ = qc.shape
    args = [qc, kc, vc]
    in_specs = [pl.BlockSpec((1, N_HEADS, c, HEAD_DIM), lambda bb: (bb, 0, 0, 0)),
                pl.BlockSpec((1, N_KV_HEADS, c, HEAD_DIM), lambda bb: (bb, 0, 0, 0)),
                pl.BlockSpec((1, N_KV_HEADS, c, 2 * LANES), lambda bb: (bb, 0, 0, 0))]
    if sink is not None:
        args.append(sink)
        in_specs.append(pl.BlockSpec(memory_space=pltpu.SMEM))
    return pl.pallas_call(
        functools.partial(_ctx_attn_kernel, has_sink=sink is not None),
        grid=(b,),
        in_specs=in_specs,
        out_specs=pl.BlockSpec((1, c, D_MODEL), lambda bb: (bb, 0, 0)),
        out_shape=jax.ShapeDtypeStruct((b, c, D_MODEL), BF16),
        compiler_params=_cparams(("parallel",)),
        name="attn_ctx",
    )(*args)


def _post_attn_kernel(o_ref, x_ref, wo_ref, gt_ref, g_ref, sh_ref, sc_ref, wr_hilo_ref, wr_hi_ref, br_ref,
                      xn_ref, hf_ref, eid_ref, wcol_ref, cnt_ref):
    xn = x_ref[0] + gt_ref[0] * _dot(o_ref[0], wo_ref[...])
    xn_ref[0] = xn
    hf = _rms_mod(xn, g_ref[0], sh_ref[0], sc_ref[0])
    hf_ref[0] = _pack_rows(hf)
    tm = hf.shape[0]

    hh, hl = _split_bf16(hf)
    both = _dot(hh, wr_hilo_ref[...])
    lg = both[:, :ROUTER_COLS] + both[:, ROUTER_COLS:] + _dot(hl, wr_hi_ref[...]) + br_ref[...]
    lgt = lg.T

    gl = lgt[0:N_GROUPS]
    ge = jnp.exp(gl - jnp.max(gl, axis=0, keepdims=True))
    pg = ge / jnp.sum(ge, axis=0, keepdims=True)
    p_top = jnp.max(pg, axis=0, keepdims=True)
    grow = lax.broadcasted_iota(I32, (N_GROUPS, tm), 0)
    grp = jnp.min(jnp.where(pg == p_top, grow, N_GROUPS), axis=0, keepdims=True)

    el = lgt[8:8 + EXPERTS_PER_GROUP]
    for gi in range(1, N_GROUPS):
        el = jnp.where(grp == gi, lgt[8 + gi * EXPERTS_PER_GROUP:8 + (gi + 1) * EXPERTS_PER_GROUP], el)
    ee = jnp.exp(el - jnp.max(el, axis=0, keepdims=True))
    pe = ee / jnp.sum(ee, axis=0, keepdims=True)
    erow = lax.broadcasted_iota(I32, (EXPERTS_PER_GROUP, tm), 0)
    p1 = jnp.max(pe, axis=0, keepdims=True)
    i1 = jnp.min(jnp.where(pe == p1, erow, EXPERTS_PER_GROUP), axis=0, keepdims=True)
    pe2 = jnp.where(erow == i1, -1.0, pe)
    p2 = jnp.max(pe2, axis=0, keepdims=True)
    i2 = jnp.min(jnp.where(pe2 == p2, erow, EXPERTS_PER_GROUP), axis=0, keepdims=True)
    den = p1 + p2
    w1 = p_top * p1 / den
    w2 = p_top * p2 / den
    e1 = grp * EXPERTS_PER_GROUP + i1
    e2 = grp * EXPERTS_PER_GROUP + i2

    eid_ref[0:1, :] = e1
    eid_ref[1:2, :] = e2
    wrows = jnp.concatenate([w1, w2, jnp.zeros((ROUTER_COLS - 2, tm), F32)], axis=0)
    wcol_ref[...] = wrows.T
    xrow = lax.broadcasted_iota(I32, (N_EXPERTS, tm), 0)
    hits = (xrow == e1).astype(F32) + (xrow == e2).astype(F32)
    cnt_ref[0] = jnp.sum(hits, axis=1, keepdims=True).astype(I32)


def _post_attn(o, x, wo_bf, modr, g_rows, router, layer, ctx_row, tm):
    b, n, d = x.shape
    nt = n // tm
    t_all = b * n
    wrh, wrl, br = router
    tile = lambda bb, t: (bb, t, 0)
    const2 = lambda bb, t: (0, 0)
    return pl.pallas_call(
        _post_attn_kernel,
        grid=(b, nt),
        in_specs=[
            pl.BlockSpec((1, tm, d), tile),
            pl.BlockSpec((1, tm, d), tile),
            pl.BlockSpec((d, d), const2),
            _mod_spec(layer, 2, ctx_row),
            _layer_row_spec(layer),
            _mod_spec(layer, 3, ctx_row),
            _mod_spec(layer, 4, ctx_row),
            pl.BlockSpec((d, 2 * ROUTER_COLS), const2),
            pl.BlockSpec((d, ROUTER_COLS), const2),
            pl.BlockSpec((1, ROUTER_COLS), const2),
        ],
        out_specs=[
            pl.BlockSpec((1, tm, d), tile),
            pl.BlockSpec((1, tm, d // 2), tile),
            pl.BlockSpec((2, tm), lambda bb, t: (0, bb * nt + t)),
            pl.BlockSpec((tm, ROUTER_COLS), lambda bb, t: (bb * nt + t, 0)),
            pl.BlockSpec((1, N_EXPERTS, 1), lambda bb, t: (bb * nt + t, 0, 0)),
        ],
        out_shape=[
            jax.ShapeDtypeStruct((b, n, d), F32),
            jax.ShapeDtypeStruct((b, n, d // 2), U32),
            jax.ShapeDtypeStruct((2, t_all), I32),
            jax.ShapeDtypeStruct((t_all, ROUTER_COLS), F32),
            jax.ShapeDtypeStruct((b * nt, N_EXPERTS, 1), I32),
        ],
        compiler_params=_cparams(("parallel", "parallel")),
        name="post_attn_router",
    )(o, x, wo_bf, modr, g_rows, modr, modr, wrh, wrl, br)


def _dest_kernel(eid_ref, base_ref, tri_ref, dest_ref):
    tm = eid_ref.shape[1]
    xrow = lax.broadcasted_iota(I32, (N_EXPERTS, tm), 0)
    oh0 = xrow == eid_ref[0:1, :]
    oh1 = xrow == eid_ref[1:2, :]
    both = oh0.astype(F32) + oh1.astype(F32)
    incl = _dot(both.astype(BF16), tri_ref[...])
    before = incl - both + base_ref[0].astype(F32)
    d0 = jnp.sum(jnp.where(oh0, before, 0.0), axis=0, keepdims=True)
    d1 = jnp.sum(jnp.where(oh1, before, 0.0), axis=0, keepdims=True)
    dest_ref[0, 0:1, :] = d0.astype(I32)
    dest_ref[0, 1:2, :] = d1.astype(I32)


def _dest_rows(eid, base, tri):
    tm = tri.shape[0]
    nt = eid.shape[1] // tm
    return pl.pallas_call(
        _dest_kernel,
        grid=(nt,),
        in_specs=[
            pl.BlockSpec((2, tm), lambda i: (0, i)),
            pl.BlockSpec((1, N_EXPERTS, 1), lambda i: (i, 0, 0)),
            pl.BlockSpec((tm, tm), lambda i: (0, 0)),
        ],
        out_specs=pl.BlockSpec((1, 2, tm), lambda i: (i, 0, 0)),
        out_shape=jax.ShapeDtypeStruct((nt, 2, tm), I32),
        compiler_params=_cparams(("parallel",)),
        name="moe_dest",
    )(eid, base, tri)


def _scatter_kernel(*refs, tiles):
    dest_ref = refs[0]
    hf_refs = refs[1:1 + len(tiles)]
    xs_ref, sem = refs[1 + len(tiles):]
    tm = hf_refs[0].shape[0]
    i = pl.program_id(0)

    def scatter_tile(hf_ref):
        def issue(t8, carry):
            base = pl.multiple_of(t8 * ROW_UNROLL, ROW_UNROLL)
            rows = hf_ref.at[pl.ds(base, ROW_UNROLL)]
            for u in range(ROW_UNROLL):
                for k in range(2):
                    d = dest_ref.at[0, k, pl.ds(base, ROW_UNROLL)][u]
                    pltpu.make_async_copy(rows.at[pl.ds(u, 1)], xs_ref.at[d], sem).start(priority=k)
            return carry

        lax.fori_loop(0, tm // ROW_UNROLL, issue, 0)
        for k in range(2):
            pltpu.make_async_copy(hf_ref, xs_ref.at[pl.ds(0, tm), 0], sem).wait()

    first = 0
    for hf_ref, nt in zip(hf_refs, tiles):
        pl.when((i >= first) & (i < first + nt))(functools.partial(scatter_tile, hf_ref))
        first += nt


def _scatter_rows(dest, hf_streams, n_rows):
    nt_all, _, tm = dest.shape
    d = hf_streams[0].shape[1]
    tiles = tuple(h.shape[0] // tm for h in hf_streams)
    assert sum(tiles) == nt_all
    in_specs = [pl.BlockSpec((1, 2, tm), lambda i: (i, 0, 0), memory_space=pltpu.SMEM)]
    first = 0
    for nt in tiles:
        in_specs.append(pl.BlockSpec((tm, d), lambda i, first=first, nt=nt: (jnp.clip(i - first, 0, nt - 1), 0)))
        first += nt
    return pl.pallas_call(
        functools.partial(_scatter_kernel, tiles=tiles),
        grid=(nt_all,),
        in_specs=in_specs,
        out_specs=pl.BlockSpec(memory_space=pl.ANY),
        out_shape=jax.ShapeDtypeStruct((n_rows, 1, d), hf_streams[0].dtype),
        scratch_shapes=[pltpu.SemaphoreType.DMA(())],
        compiler_params=_cparams(("arbitrary",)),
        name="moe_scatter",
    )(dest, *hf_streams)


def _expert_kernel(ve_ref, vblk_ref, lo_ref, hi_ref, x_hbm, wg_ref, wu_ref, wd_ref, y_hbm, wg_s, wu_s, wd_s,
                   xbuf, xsems, ybuf, ysems, *, n_blocks):
    v = pl.program_id(0)
    n_visits = pl.num_programs(0)
    pv = jnp.maximum(v - 1, 0)
    nv = jnp.minimum(v + 1, n_visits - 1)
    blk = vblk_ref[v]
    new_expert = (v == 0) | (ve_ref[v] != ve_ref[pv])
    first_visit = (v == 0) | (blk != vblk_ref[pv])
    last_visit = (v == n_visits - 1) | (blk != vblk_ref[nv])
    lo = lo_ref[v]
    hi = hi_ref[v]
    xslot = v % 2
    yslot = blk % 2

    def x_copy(visit, slot):
        start = pl.multiple_of(vblk_ref[visit] * EXPERT_BLOCK, EXPERT_BLOCK)
        return pltpu.make_async_copy(x_hbm.at[pl.ds(start, EXPERT_BLOCK), 0], xbuf.at[slot], xsems.at[slot])

    def y_copy(slot):
        start = pl.multiple_of(blk * EXPERT_BLOCK, EXPERT_BLOCK)
        return pltpu.make_async_copy(ybuf.at[slot], y_hbm.at[pl.ds(start, EXPERT_BLOCK), 0], ysems.at[slot])

    @pl.when(v == 0)
    def _():
        x_copy(0, 0).start()

    @pl.when(v + 1 < n_visits)
    def _():
        x_copy(v + 1, 1 - xslot).start()

    @pl.when(new_expert)
    def _():
        wg_s[...] = wg_ref[0, 0].astype(BF16)
        wu_s[...] = wu_ref[0, 0].astype(BF16)
        wd_s[...] = wd_ref[0, 0].astype(BF16)

    @pl.when(first_visit & (blk >= 2))
    def _():
        y_copy(yslot).wait()

    x_copy(v, xslot).wait()

    @pl.when(hi > lo)
    def _():
        x_lo, x_hi = _unpack_rows(xbuf[xslot])
        x_lo = x_lo.astype(BF16)
        x_hi = x_hi.astype(BF16)
        half = x_lo.shape[1]
        gate = _dot(x_lo, wg_s[:half]) + _dot(x_hi, wg_s[half:])
        up = _dot(x_lo, wu_s[:half]) + _dot(x_hi, wu_s[half:])
        mid = gate * jax.nn.sigmoid(gate) * up
        y = _pack_rows(_dot(mid.astype(BF16), wd_s[...]))
        row = lax.broadcasted_iota(I32, (EXPERT_BLOCK, 1), 0)
        mine = (row >= lo) & (row < hi)

        @pl.when(first_visit)
        def _():
            ybuf[yslot] = jnp.where(mine, y, jnp.uint32(0))

        @pl.when(jnp.logical_not(first_visit))
        def _():
            ybuf[yslot] = jnp.where(mine, y, ybuf[yslot])

    @pl.when(last_visit)
    def _():
        y_copy(yslot).start()

    @pl.when(v == n_visits - 1)
    def _():
        y_copy(yslot).wait()
        if n_blocks >= 2:
            y_copy(1 - yslot).wait()


def _experts(visits, xs, w_gate, w_up, w_down, layer):
    n_rows, _, dp = xs.shape
    _, _, d, f = w_gate.shape
    assert d == 2 * dp and n_rows % EXPERT_BLOCK == 0
    ve, vblk, lo, hi = visits
    w_map = lambda v, ve_r, vb_r, lo_r, hi_r: (layer, ve_r[v], 0, 0)
    grid_spec = pltpu.PrefetchScalarGridSpec(
        num_scalar_prefetch=4,
        grid=(ve.shape[0],),
        in_specs=[
            pl.BlockSpec(memory_space=pl.ANY),
            pl.BlockSpec((1, 1, d, f), w_map),
            pl.BlockSpec((1, 1, d, f), w_map),
            pl.BlockSpec((1, 1, f, d), w_map),
        ],
        out_specs=pl.BlockSpec(memory_space=pl.ANY),
        scratch_shapes=[pltpu.VMEM((d, f), BF16), pltpu.VMEM((d, f), BF16), pltpu.VMEM((f, d), BF16),
                        pltpu.VMEM((2, EXPERT_BLOCK, dp), U32), pltpu.SemaphoreType.DMA((2,)),
                        pltpu.VMEM((2, EXPERT_BLOCK, dp), U32), pltpu.SemaphoreType.DMA((2,))],
    )
    return pl.pallas_call(
        functools.partial(_expert_kernel, n_blocks=n_rows // EXPERT_BLOCK),
        grid_spec=grid_spec,
        out_shape=jax.ShapeDtypeStruct((n_rows, 1, dp), U32),
        compiler_params=_cparams(("arbitrary",)),
        name="moe_experts",
    )(ve, vblk, lo, hi, xs, w_gate, w_up, w_down)


def _combine_kernel(*refs, final_norm):
    if final_norm:
        dest_ref, dest_next_ref, xn_ref, wcol_ref, gt_ref, gfin_ref, y_ref, o_ref, ybuf, sems = refs
    else:
        dest_ref, dest_next_ref, xn_ref, wcol_ref, gt_ref, y_ref, o_ref, ybuf, sems = refs
    tm = xn_ref.shape[1]
    step = pl.program_id(0) * pl.num_programs(1) + pl.program_id(1)
    n_steps = pl.num_programs(0) * pl.num_programs(1)
    slot = step % 2

    def gather_tile(d_ref, into):
        def issue(t8, carry):
            base = pl.multiple_of(t8 * ROW_UNROLL, ROW_UNROLL)
            for k in range(2):
                rows = ybuf.at[into, k, pl.ds(base, ROW_UNROLL)]
                for u in range(ROW_UNROLL):
                    d = d_ref.at[0, k, pl.ds(base, ROW_UNROLL)][u]
                    pltpu.make_async_copy(y_ref.at[d], rows.at[pl.ds(u, 1)], sems.at[into]).start(priority=k)
            return carry

        lax.fori_loop(0, tm // ROW_UNROLL, issue, 0)

    pl.when(step == 0)(functools.partial(gather_tile, dest_ref, 0))
    pl.when(step + 1 < n_steps)(functools.partial(gather_tile, dest_next_ref, 1 - slot))

    for k in range(2):
        pltpu.make_async_copy(y_ref.at[pl.ds(0, tm), 0], ybuf.at[slot, k], sems.at[slot]).wait()
    w = wcol_ref[...]
    lo0, hi0 = _unpack_rows(ybuf[slot, 0])
    lo1, hi1 = _unpack_rows(ybuf[slot, 1])
    moe = jnp.concatenate([w[:, 0:1] * lo0 + w[:, 1:2] * lo1, w[:, 0:1] * hi0 + w[:, 1:2] * hi1], axis=1)
    out = xn_ref[0] + gt_ref[0] * moe
    if final_norm:
        ms = jnp.mean(out * out, axis=-1, keepdims=True)
        out = out * lax.rsqrt(ms + NORM_EPS) * gfin_ref[...]
    o_ref[0] = out


def _combine(dest, xn, wcol, modr, y, layer, ctx_row, g_final=None):
    b, n, d = xn.shape
    tm = COMBINE_TILE
    sub = dest.shape[2] // tm
    n_steps = dest.shape[0] * sub
    nt = n // tm
    final_norm = g_final is not None
    args = [dest, dest, xn, wcol, modr]

    def dest_block(step):
        return (step // sub, 0, step % sub)

    in_specs = [
        pl.BlockSpec((1, 2, tm), lambda bb, t: dest_block(bb * nt + t), memory_space=pltpu.SMEM),
        pl.BlockSpec((1, 2, tm), lambda bb, t: dest_block(jnp.minimum(bb * nt + t + 1, n_steps - 1)),
                     memory_space=pltpu.SMEM),
        pl.BlockSpec((1, tm, d), lambda bb, t: (bb, t, 0)),
        pl.BlockSpec((tm, ROUTER_COLS), lambda bb, t: (bb * nt + t, 0)),
        _mod_spec(layer, 5, ctx_row),
    ]
    if final_norm:
        args.append(g_final.reshape(1, d))
        in_specs.append(pl.BlockSpec((1, d), lambda bb, t: (0, 0)))
    args.append(y)
    in_specs.append(pl.BlockSpec(memory_space=pl.ANY))
    return pl.pallas_call(
        functools.partial(_combine_kernel, final_norm=final_norm),
        grid=(b, nt),
        in_specs=in_specs,
        out_specs=pl.BlockSpec((1, tm, d), lambda bb, t: (bb, t, 0)),
        out_shape=jax.ShapeDtypeStruct((b, n, d), F32),
        scratch_shapes=[pltpu.VMEM((2, 2, tm, y.shape[2]), y.dtype), pltpu.SemaphoreType.DMA((2,))],
        compiler_params=_cparams(("arbitrary", "arbitrary")),
        name="moe_combine",
    )(*args)


def _rope_tables(n):
    t = np.arange(n)
    row = (t // GRID_W).astype(np.float32)
    col = (t % GRID_W).astype(np.float32)
    quarter = HEAD_DIM // 4
    inv = jnp.asarray(ROPE_THETA, F32) ** (-jnp.arange(quarter, dtype=F32) / quarter)
    ar = jnp.asarray(row)[:, None] * inv
    ac = jnp.asarray(col)[:, None] * inv
    ang = jnp.concatenate([ar, ar, ac, ac], axis=-1)
    ang = jnp.concatenate([ang, ang], axis=-1)
    sign = np.where((np.arange(LANES) % (HEAD_DIM // 2)) < quarter, -1.0, 1.0).astype(np.float32)
    return jnp.cos(ang), jnp.sin(ang) * sign


def _na_bias_tables(rpb):
    n_dr, n_dc = 2 * NA_WIN_H - 1, 2 * NA_WIN_W - 1
    blocked = jnp.full((N_HEADS, 2, n_dc), NEG_INF, F32)
    rows = jnp.concatenate([blocked, rpb.astype(F32) * LOG2E, blocked], axis=1)
    assert rows.shape[1] == n_dr + 4 == NA_TABLE + 1
    pair = jnp.stack([rows[:, 0:NA_TABLE], rows[:, 1:NA_TABLE + 1]], axis=2)
    c = np.arange(GRID_W)[:, None]
    kc = np.arange(GRID_W)[None, :]
    cs = np.clip(c - NA_WIN_W // 2, 0, GRID_W - NA_WIN_W)
    in_window = (kc >= cs) & (kc < cs + NA_WIN_W)
    out = jnp.full((N_HEADS, NA_TABLE, GRID_W, 2, GRID_W), NEG_INF, F32)
    for dc in range(-(NA_WIN_W - 1), NA_WIN_W):
        hit = ((kc - c) == dc) & in_window
        if hit.any():
            out = jnp.where(hit[None, None, :, None, :], pair[:, :, None, :, dc + NA_WIN_W - 1, None], out)
    return out.reshape(N_HEADS, NA_TABLE, GRID_W, LANES)


def _router_tables(w_group, b_group, w_router, b_router):
    d = w_group.shape[0]
    w = jnp.zeros((d, ROUTER_COLS), F32)
    w = w.at[:, 0:N_GROUPS].set(w_group).at[:, 8:8 + N_EXPERTS].set(w_router)
    bias = jnp.zeros((1, ROUTER_COLS), F32)
    bias = bias.at[0, 0:N_GROUPS].set(b_group).at[0, 8:8 + N_EXPERTS].set(b_router)
    hi = w.astype(BF16)
    lo = (w - hi.astype(F32)).astype(BF16)
    return jnp.concatenate([hi, lo], axis=1), hi, bias


def _pick(table, onehot):
    return jnp.sum(jnp.where(onehot, table[None, :], 0), axis=1)


def _moe_layout(cnt_tiles, n_rows):
    cnt = cnt_tiles[:, :, 0]
    counts = jnp.sum(cnt, axis=0)
    ends = jnp.cumsum(counts)
    starts = ends - counts
    base = starts[None, :] + jnp.cumsum(cnt, axis=0) - cnt
    first_blk = starts // EXPERT_BLOCK
    n_blk = jnp.where(counts > 0, (ends - 1) // EXPERT_BLOCK - first_blk + 1, 0)
    v_end = jnp.cumsum(n_blk)
    v_off = v_end - n_blk
    n_visits = n_rows // EXPERT_BLOCK + N_EXPERTS
    v = jnp.arange(n_visits, dtype=I32)
    valid = v < v_end[-1]
    vv = jnp.minimum(v, v_end[-1] - 1)
    ve = jnp.sum((v_end[None, :] <= vv[:, None]).astype(I32), axis=1)
    onehot = ve[:, None] == jnp.arange(N_EXPERTS, dtype=I32)[None, :]
    vblk = _pick(first_blk, onehot) + vv - _pick(v_off, onehot)
    lo = jnp.clip(_pick(starts, onehot) - vblk * EXPERT_BLOCK, 0, EXPERT_BLOCK)
    hi = jnp.clip(_pick(ends, onehot) - vblk * EXPERT_BLOCK, 0, EXPERT_BLOCK)
    hi = jnp.where(valid, hi, lo)
    visits = tuple(a.astype(I32) for a in (ve, vblk, lo, hi))
    return base[:, :, None].astype(I32), visits


def kernel(x, c, ctx, c_ctx, w_ada, b_ada, g_attn, w_qkv, w_o, sink_a, gq_b, gk_b, rpb_c, g_ffn,
           w_group, b_group, w_router, b_router, w_gate, w_up, w_down, g_final):
    b, n, d = x.shape
    n_ctx = ctx.shape[1]
    depth = w_ada.shape[0]
    tm_tok = 1024
    tm_qkv = 512
    tm_ctx = 256
    assert b + 1 <= MOD_ROWS and d == D_MODEL and n % tm_tok == 0 and n_ctx % tm_ctx == 0
    assert (b * n_ctx) % tm_tok == 0
    ctx_row = b

    cc = jnp.zeros((MOD_ROWS, d), F32).at[:b].set(c).at[b].set(c_ctx)
    modr = _ada_all(cc, w_ada, b_ada).reshape(depth * MOD_ROWS * N_MOD, 1, d)
    g_attn_rows = g_attn.reshape(depth, 1, d)
    g_ffn_rows = g_ffn.reshape(depth, 1, d)
    rope_tabs = _rope_tables(n)
    head_sum = jnp.asarray(np.kron(np.eye(LANES // HEAD_DIM), np.ones((HEAD_DIM, HEAD_DIM))), BF16)
    tri = jnp.asarray(np.triu(np.ones((tm_tok, tm_tok))), BF16)

    def layer_qkv_config(layer):
        mixer, idx = layer % N_MIXERS, layer // N_MIXERS
        gains = None
        if mixer == 1:
            gains = (jnp.tile(gq_b[idx], LANES // HEAD_DIM).reshape(1, LANES),
                     jnp.tile(gk_b[idx], LANES // HEAD_DIM).reshape(1, LANES), head_sum)
        return w_qkv[layer].astype(BF16), (rope_tabs if mixer != 2 else None), gains

    xc = ctx
    for i in range(depth):
        m = i % N_MIXERS
        j = i // N_MIXERS
        last = i == depth - 1
        wqkv_bf, rope_i, qk_gains = layer_qkv_config(i)
        wo_bf = w_o[i].astype(BF16)

        latent_qkv = _qkv(x, modr, g_attn_rows, wqkv_bf, i, None, tm_qkv, rope_i, qk_gains)
        if m == 2:
            qp, k, v = latent_qkv
            qr = None
        else:
            qp, qr, k, v = latent_qkv
        qc, kc, vc = _qkv(xc, modr, g_attn_rows, wqkv_bf, i, ctx_row, tm_ctx, None, qk_gains)
        sink = sink_a[j] if m == 0 else None
        extra = sink if m == 0 else (_na_bias_tables(rpb_c[j]) if m == 2 else None)
        o = _attention(m, qp, qr, k, v, kc, vc, extra)

        router = _router_tables(w_group[i], b_group[i], w_router[i], b_router[i])
        streams = [(o, x, None)]
        if not last:
            oc = _ctx_attention(qc, kc, vc, sink)
            streams.append((oc.reshape(1, b * n_ctx, d), xc.reshape(1, b * n_ctx, d), ctx_row))
        routed = [_post_attn(o_s, x_s, wo_bf, modr, g_ffn_rows, router, i, row, tm_tok) for (o_s, x_s, row) in streams]

        n_rows = 2 * sum(r[2].shape[1] for r in routed)
        assert n_rows % EXPERT_BLOCK == 0
        base, visits = _moe_layout(jnp.concatenate([r[4] for r in routed], axis=0), n_rows)
        dest = _dest_rows(jnp.concatenate([r[2] for r in routed], axis=1), base, tri)
        xs = _scatter_rows(dest, [r[1].reshape(-1, d // 2) for r in routed], n_rows)
        y = _experts(visits, xs, w_gate, w_up, w_down, i)
        nt_lat = routed[0][4].shape[0]
        x = _combine(dest[:nt_lat], routed[0][0], routed[0][3], modr, y, i, None, g_final if last else None)
        if not last:
            xc = _combine(dest[nt_lat:], routed[1][0], routed[1][3], modr, y, i, ctx_row).reshape(b, n_ctx, d)
    return x
```
